```python
import jax, jax.numpy as jnp
from jax import lax
import numpy as np

D_MODEL = 4096
BATCH = 4
SEQ = 2048
DEPTH = 1
DEC_BATCH = 128
DEC_SEQ = 8
PAST_LEN = 16384
PAGE_SIZE = 128

MIX_W = D_MODEL
RWKV_W = MIX_W // 2
POOL_W = MIX_W - RWKV_W
HEAD_DIM = 64
N_HEADS = RWKV_W // HEAD_DIM
DECAY_RANK = max(32, round(1.8 * RWKV_W ** 0.5 / 32) * 32)
AAA_RANK = max(32, round(1.8 * RWKV_W ** 0.5 / 32) * 32)
GATE_RANK = max(32, round(0.6 * RWKV_W ** 0.8 / 32) * 32)
SHIFT_W = 3 * RWKV_W + DECAY_RANK + AAA_RANK + GATE_RANK
IN_W = SHIFT_W + POOL_W
SPLITS = (RWKV_W, 2 * RWKV_W, 3 * RWKV_W, 3 * RWKV_W + DECAY_RANK,
          3 * RWKV_W + DECAY_RANK + AAA_RANK)
POOL_WINDOWS = (2, 4, 8, 16)
N_POOL_GROUPS = len(POOL_WINDOWS)
POOL_GW = POOL_W // N_POOL_GROUPS
POOL_BUF = max(POOL_WINDOWS) - 1
MEM_TOKENS = 256
XA_HEADS = 4
XA_HEAD_DIM = 128
XA_W = XA_HEADS * XA_HEAD_DIM
D_FF = 4 * D_MODEL
RMS_EPS = 1e-6
GN_EPS = 64e-5

kernel_name = "rwkv7_pool_hymba_memxattn_step"


def _rmsnorm(x, g):
    xf = x.astype(jnp.float32)
    y = xf * lax.rsqrt(jnp.mean(xf * xf, axis=-1, keepdims=True) + RMS_EPS)
    return (y * g.astype(jnp.float32)).astype(x.dtype)


def _wkv_scan(S0, r, decay, k, v, kk, kka):
    def step(S, inp):
        r_t, w_t, k_t, v_t, kk_t, kka_t = inp
        sa = jnp.einsum("bhvk,bhk->bhv", S, kk_t)
        S = (S * w_t[:, :, None, :] - sa[..., None] * kka_t[:, :, None, :]
             + v_t[..., None] * k_t[:, :, None, :])
        return S, jnp.einsum("bhvk,bhk->bhv", S, r_t)
    xs = tuple(jnp.swapaxes(t, 0, 1) for t in (r, decay, k, v, kk, kka))
    S, y = lax.scan(step, S0, xs)
    return S, jnp.swapaxes(y, 0, 1)


def _mixer(h, shift_prev, pool_prev, wkv_prev, start_pos, lp):
    f32 = jnp.float32
    B, T, _ = h.shape
    z = jnp.einsum("btd,de->bte", h, lp["w_in"])
    zs, zp = z[..., :SHIFT_W], z[..., SHIFT_W:]
    zs_prev = jnp.concatenate([shift_prev[:, None, :].astype(z.dtype), zs[:, :-1]], axis=1)
    zm = zs + lp["mu_shift"] * (zs_prev - zs)
    r, k, v, wd, ad, gd = jnp.split(zm, SPLITS, axis=-1)
    w_log = -jax.nn.softplus(-(lp["w0"] + jnp.tanh(wd) @ lp["w_decay_up"]).astype(f32)) - 0.5
    decay = jnp.exp(-jnp.exp(w_log))
    a = jax.nn.sigmoid((lp["a0"] + ad @ lp["w_a_up"]).astype(f32))
    g = (jax.nn.sigmoid(gd) @ lp["w_g_up"]).astype(f32)
    hd = lambda t: t.reshape(B, T, N_HEADS, HEAD_DIM)
    kk = hd((k * lp["k_k"]).astype(f32))
    kk = kk / jnp.maximum(jnp.sqrt(jnp.sum(kk * kk, axis=-1, keepdims=True)), 1e-12)
    kf = hd(k.astype(f32) * (1.0 + (a - 1.0) * lp["k_a"].astype(f32)))
    rf, vf, ah = hd(r.astype(f32)), hd(v.astype(f32)), hd(a)
    S, y = _wkv_scan(wkv_prev.astype(f32), rf, hd(decay), kf, vf, kk, kk * ah)
    mu = jnp.mean(y, axis=-1, keepdims=True)
    var = jnp.mean(jnp.square(y - mu), axis=-1, keepdims=True)
    y = ((y - mu) * lax.rsqrt(var + GN_EPS)).reshape(B, T, RWKV_W)
    y = y * lp["lnx_g"].astype(f32) + lp["lnx_b"].astype(f32)
    bonus = jnp.sum(rf * kf * lp["r_k"].astype(f32), axis=-1, keepdims=True) * vf
    o_rwkv = ((y + bonus.reshape(B, T, RWKV_W)) * g).astype(h.dtype)
    seq = jnp.concatenate([pool_prev.astype(zp.dtype), zp], axis=1)
    cs = jnp.cumsum(seq.astype(f32), axis=1)
    cs = jnp.concatenate([jnp.zeros((B, 1, POOL_W), f32), cs], axis=1)
    pos = start_pos + jnp.arange(T)
    end = POOL_BUF + 1
    means = []
    for gi, wdw in enumerate(POOL_WINDOWS):
        c0, c1 = gi * POOL_GW, (gi + 1) * POOL_GW
        s = cs[:, end:end + T, c0:c1] - cs[:, end - wdw:end - wdw + T, c0:c1]
        cnt = jnp.minimum(wdw, pos + 1).astype(f32)[None, :, None]
        means.append(s / cnt)
    d = jnp.stack(means, axis=2) - zp.reshape(B, T, N_POOL_GROUPS, POOL_GW).astype(f32)
    o_pool = jnp.einsum("btgc,gce->btge", d.astype(h.dtype), lp["w_pool"])
    o_pool = (o_pool.reshape(B, T, POOL_W) * lp["pool_scale"]).astype(h.dtype)
    out = jnp.concatenate([o_rwkv, o_pool], axis=-1) @ lp["w_out"]
    return out, zs[:, -1], seq[:, -POOL_BUF:], S.astype(wkv_prev.dtype)


def _mem_kv(mem, g, w_mk, w_mv):
    B = mem.shape[0]
    m = _rmsnorm(mem, g)
    mk = jnp.einsum("bmd,de->bme", m, w_mk).reshape(B, MEM_TOKENS, XA_HEADS, XA_HEAD_DIM)
    mv = jnp.einsum("bmd,de->bme", m, w_mv).reshape(B, MEM_TOKENS, XA_HEADS, XA_HEAD_DIM)
    return mk, mv


def _cross_attn(h, mk, mv, w_q, w_o):
    B, T, _ = h.shape
    q = (h @ w_q).reshape(B, T, XA_HEADS, XA_HEAD_DIM)
    s = jnp.einsum("bthd,bmhd->bhtm", q, mk.astype(q.dtype)).astype(jnp.float32)
    p = jax.nn.softmax(s * (XA_HEAD_DIM ** -0.5), axis=-1).astype(h.dtype)
    o = jnp.einsum("bhtm,bmhd->bthd", p, mv.astype(h.dtype)).reshape(B, T, XA_W)
    return o @ w_o


def _block(x, mk, mv, shift_prev, pool_prev, wkv_prev, start_pos, lp):
    mix, sh, pl, wkv = _mixer(_rmsnorm(x, lp["norm_mix_g"]), shift_prev, pool_prev,
                              wkv_prev, start_pos, lp)
    x = x + mix
    x = x + _cross_attn(_rmsnorm(x, lp["norm_xa_g"]), mk, mv, lp["w_xq"], lp["w_xo"])
    hf = _rmsnorm(x, lp["norm_ffn_g"])
    x = x + jnp.square(jax.nn.relu(hf @ lp["w_up"])) @ lp["w_down"]
    return x, sh, pl, wkv


def setup_inputs(seed: int = 0) -> dict:
    key = jax.random.key(seed)
    ks = jax.random.split(key, 48)
    cnt = [0]

    def nk():
        cnt[0] += 1
        return ks[cnt[0] - 1]

    def nrm(shape, scale):
        return jax.random.normal(nk(), shape, jnp.float32) * scale

    def uni(shape, lo, hi):
        return jax.random.uniform(nk(), shape, jnp.float32, lo, hi)

    L = DEPTH
    return {
        "x_prompt": nrm((BATCH, SEQ, D_MODEL), 1.0),
        "x_sample": nrm((DEC_BATCH, DEC_SEQ, D_MODEL), 1.0),
        "mem_prompt": nrm((BATCH, MEM_TOKENS, D_MODEL), 1.0),
        "state_wkv": nrm((L, DEC_BATCH, N_HEADS, HEAD_DIM, HEAD_DIM), 1.0),
        "state_shift": nrm((L, DEC_BATCH, SHIFT_W), 1.0),
        "state_pool": nrm((L, DEC_BATCH, POOL_BUF, POOL_W), 1.0),
        "cache_mem_k": nrm((L, DEC_BATCH, MEM_TOKENS, XA_HEADS, XA_HEAD_DIM), 1.0),
        "cache_mem_v": nrm((L, DEC_BATCH, MEM_TOKENS, XA_HEADS, XA_HEAD_DIM), 1.0),
        "norm_mix_g": 1.0 + nrm((L, D_MODEL), 0.01),
        "w_in": nrm((L, D_MODEL, IN_W), D_MODEL ** -0.5),
        "mu_shift": uni((L, SHIFT_W), 0.0, 1.0),
        "w0": uni((L, RWKV_W), -6.0, 0.0),
        "w_decay_up": nrm((L, DECAY_RANK, RWKV_W), 0.5 * DECAY_RANK ** -0.5),
        "a0": nrm((L, RWKV_W), 0.1),
        "w_a_up": nrm((L, AAA_RANK, RWKV_W), 0.5 * AAA_RANK ** -0.5),
        "w_g_up": nrm((L, GATE_RANK, RWKV_W), GATE_RANK ** -0.5),
        "k_k": 0.85 + nrm((L, RWKV_W), 0.05),
        "k_a": 1.0 + nrm((L, RWKV_W), 0.05),
        "r_k": nrm((L, N_HEADS, HEAD_DIM), 0.1),
        "lnx_g": 1.0 + nrm((L, RWKV_W), 0.01),
        "lnx_b": nrm((L, RWKV_W), 0.01),
        "w_pool": nrm((L, N_POOL_GROUPS, POOL_GW, POOL_GW), POOL_GW ** -0.5),
        "pool_scale": 1.0 + nrm((L, POOL_W), 0.1),
        "w_out": nrm((L, MIX_W, D_MODEL), MIX_W ** -0.5),
        "norm_xa_g": 1.0 + nrm((L, D_MODEL), 0.01),
        "norm_mem_g": 1.0 + nrm((L, D_MODEL), 0.01),
        "w_xq": nrm((L, D_MODEL, XA_W), D_MODEL ** -0.5),
        "w_mk": nrm((L, D_MODEL, XA_W), D_MODEL ** -0.5),
        "w_mv": nrm((L, D_MODEL, XA_W), D_MODEL ** -0.5),
        "w_xo": nrm((L, XA_W, D_MODEL), XA_W ** -0.5),
        "norm_ffn_g": 1.0 + nrm((L, D_MODEL), 0.01),
        "w_up": nrm((L, D_MODEL, D_FF), D_MODEL ** -0.5),
        "w_down": nrm((L, D_FF, D_MODEL), D_FF ** -0.5),
        "norm_final_g": 1.0 + nrm((D_MODEL,), 0.01),
    }


def reference(x_prompt, x_sample, mem_prompt, state_wkv, state_shift, state_pool,
              cache_mem_k, cache_mem_v, norm_mix_g, w_in, mu_shift, w0, w_decay_up, a0,
              w_a_up, w_g_up, k_k, k_a, r_k, lnx_g, lnx_b, w_pool, pool_scale, w_out,
              norm_xa_g, norm_mem_g, w_xq, w_mk, w_mv, w_xo, norm_ffn_g, w_up, w_down,
              norm_final_g):
    xp, xs = x_prompt, x_sample
    Bp = x_prompt.shape[0]
    wkv_p_l, sh_p_l, pl_p_l, mk_p_l, mv_p_l = [], [], [], [], []
    wkv_s_l, sh_s_l, pl_s_l = [], [], []
    for l in range(DEPTH):
        lp = {
            "norm_mix_g": norm_mix_g[l], "w_in": w_in[l], "mu_shift": mu_shift[l],
            "w0": w0[l], "w_decay_up": w_decay_up[l], "a0": a0[l], "w_a_up": w_a_up[l],
            "w_g_up": w_g_up[l], "k_k": k_k[l], "k_a": k_a[l], "r_k": r_k[l],
            "lnx_g": lnx_g[l], "lnx_b": lnx_b[l], "w_pool": w_pool[l],
            "pool_scale": pool_scale[l], "w_out": w_out[l], "norm_xa_g": norm_xa_g[l],
            "w_xq": w_xq[l], "w_xo": w_xo[l], "norm_ffn_g": norm_ffn_g[l],
            "w_up": w_up[l], "w_down": w_down[l],
        }
        mk_p, mv_p = _mem_kv(mem_prompt, norm_mem_g[l], w_mk[l], w_mv[l])
        xp, sh_p, pl_p, wkv_p = _block(
            xp, mk_p, mv_p,
            jnp.zeros((Bp, SHIFT_W), xp.dtype),
            jnp.zeros((Bp, POOL_BUF, POOL_W), xp.dtype),
            jnp.zeros((Bp, N_HEADS, HEAD_DIM, HEAD_DIM), jnp.float32),
            0, lp)
        xs, sh_s, pl_s, wkv_s = _block(
            xs, cache_mem_k[l], cache_mem_v[l], state_shift[l], state_pool[l],
            state_wkv[l], PAST_LEN, lp)
        wkv_p_l.append(wkv_p); sh_p_l.append(sh_p); pl_p_l.append(pl_p)
        mk_p_l.append(mk_p); mv_p_l.append(mv_p)
        wkv_s_l.append(wkv_s); sh_s_l.append(sh_s); pl_s_l.append(pl_s)
    y_prompt = _rmsnorm(xp, norm_final_g)
    y_sample = _rmsnorm(xs, norm_final_g)
    return (y_prompt, y_sample,
            jnp.stack(wkv_p_l), jnp.stack(sh_p_l), jnp.stack(pl_p_l),
            jnp.stack(mk_p_l), jnp.stack(mv_p_l),
            jnp.stack(wkv_s_l), jnp.stack(sh_s_l), jnp.stack(pl_s_l))
```

```python
import functools

import jax
import jax.numpy as jnp
from jax import lax
from jax.experimental import pallas as pl
from jax.experimental.pallas import tpu as pltpu

D_MODEL = 4096
RWKV_W = 2048
POOL_W = 2048
HEAD_DIM = 64
N_HEADS = RWKV_W // HEAD_DIM
DECAY_RANK = 96
AAA_RANK = 96
GATE_RANK = 256
SHIFT_W = 3 * RWKV_W + DECAY_RANK + AAA_RANK + GATE_RANK
POOL_WINDOWS = (2, 4, 8, 16)
POOL_GW = POOL_W // len(POOL_WINDOWS)
POOL_BUF = max(POOL_WINDOWS) - 1
MEM_TOKENS = 256
XA_HEADS = 4
XA_HEAD_DIM = 128
XA_W = XA_HEADS * XA_HEAD_DIM
PAST_LEN = 16384
RMS_EPS = 1e-6
GN_EPS = 64e-5

LORA_PAD = 128
OFF_WD = 3 * RWKV_W
OFF_AD = OFF_WD + LORA_PAD
OFF_GD = OFF_AD + LORA_PAD
SHIFT_WP = OFF_GD + GATE_RANK
IN_WP = SHIFT_WP + POOL_W

VMEM_LIMIT = 56 * 1024 * 1024

_NN = ((1,), (0,))
_NT = ((1,), (1,))
_TN = ((0,), (0,))

f32 = jnp.float32
bf16 = jnp.bfloat16


def _dot(a, b, dims):
    return lax.dot_general(a, b, (dims, ((), ())), preferred_element_type=f32)


def _split(x):
    hi = x.astype(bf16)
    lo = (x - hi.astype(f32)).astype(bf16)
    return hi, lo


def _mm(a, b, dims, passes):
    if passes == 1:
        return _dot(a.astype(bf16), b.astype(bf16), dims)
    ah, al = _split(a)
    bh, bl = _split(b)
    return _dot(ah, bh, dims) + (_dot(ah, bl, dims) + _dot(al, bh, dims))


def _rmsnorm_kernel(x_ref, g_ref, o_ref):
    x = x_ref[...]
    y = x * lax.rsqrt(jnp.mean(x * x, axis=-1, keepdims=True) + RMS_EPS)
    o_ref[...] = (y * g_ref[...]).astype(o_ref.dtype)


def _rmsnorm(x, g, out_dtype, tm=256):
    m, d = x.shape
    return pl.pallas_call(
        _rmsnorm_kernel,
        grid=(m // tm,),
        in_specs=[pl.BlockSpec((tm, d), lambda i: (i, 0)), pl.BlockSpec((1, d), lambda i: (0, 0))],
        out_specs=pl.BlockSpec((tm, d), lambda i: (i, 0)),
        out_shape=jax.ShapeDtypeStruct((m, d), out_dtype),
        compiler_params=pltpu.CompilerParams(dimension_semantics=("parallel",), vmem_limit_bytes=VMEM_LIMIT),
        name="rmsnorm",
    )(x, g.reshape(1, d))


def _matmul_kernel(a_ref, b_ref, *rest, nk, act, has_res):
    if has_res:
        res_ref, o_ref, acc_ref = rest
    else:
        o_ref, acc_ref = rest
    k = pl.program_id(2)

    @pl.when(k == 0)
    def _():
        acc_ref[...] = jnp.zeros_like(acc_ref)

    acc_ref[...] += jnp.dot(a_ref[...], b_ref[...], preferred_element_type=f32)

    @pl.when(k == nk - 1)
    def _():
        r = acc_ref[...]
        if act == "relu2":
            r = jnp.square(jnp.maximum(r, 0.0))
        if has_res:
            r = r + res_ref[...]
        o_ref[...] = r.astype(o_ref.dtype)


def _matmul(a, b, res=None, act=None, out_dtype=f32, tm=1024, tn=1024, tk=1024):
    m, kd = a.shape
    _, n = b.shape
    tm, tn, tk = min(tm, m), min(tn, n), min(tk, kd)
    assert m % tm == 0 and n % tn == 0 and kd % tk == 0, (a.shape, b.shape, tm, tn, tk)
    nk = kd // tk
    in_specs = [pl.BlockSpec((tm, tk), lambda i, j, k: (i, k)), pl.BlockSpec((tk, tn), lambda i, j, k: (k, j))]
    args = [a, b]
    if res is not None:
        in_specs.append(pl.BlockSpec((tm, tn), lambda i, j, k: (i, j)))
        args.append(res)
    return pl.pallas_call(
        functools.partial(_matmul_kernel, nk=nk, act=act, has_res=res is not None),
        grid=(m // tm, n // tn, nk),
        in_specs=in_specs,
        out_specs=pl.BlockSpec((tm, tn), lambda i, j, k: (i, j)),
        out_shape=jax.ShapeDtypeStruct((m, n), out_dtype),
        scratch_shapes=[pltpu.VMEM((tm, tn), f32)],
        compiler_params=pltpu.CompilerParams(
            dimension_semantics=("parallel", "parallel", "arbitrary"), vmem_limit_bytes=VMEM_LIMIT),
        name="matmul",
    )(*args)


def _wkv_kernel(r_ref, lw_ref, kf_ref, kk_ref, kka_ref, v_ref, s0_ref, y_ref, sout_ref, s_scr,
                *, chunk, heads, nchunks, p_small, p_inv, p_state):
    C, Hg = chunk, heads
    L, R = Hg * HEAD_DIM, Hg * C
    c = pl.program_id(2)

    @pl.when(c == 0)
    def _():
        s_scr[...] = s0_ref[0]

    r, lw, kf, kk, kka = r_ref[0], lw_ref[0], kf_ref[0], kk_ref[0], kka_ref[0]

    tri = (lax.broadcasted_iota(jnp.int32, (C, C), 1) <= lax.broadcasted_iota(jnp.int32, (C, C), 0)).astype(bf16)
    h1 = lw.astype(bf16)
    r1 = lw - h1.astype(f32)
    h2 = r1.astype(bf16)
    h3 = (r1 - h2.astype(f32)).astype(bf16)
    cum = _dot(tri, h1, _NN) + (_dot(tri, h2, _NN) + _dot(tri, h3, _NN))

    g = jnp.exp(cum)
    gi = jnp.exp(-cum)
    gprev = jnp.exp(cum - lw)
    g_end = g[C - 1:C, :]
    gl = g_end * gi
    ab, rb = -kk * gprev, r * g
    bb, kb = kka * gi, kf * gi
    bl, kl = kka * gl, kf * gl

    head_mask = (lax.broadcasted_iota(jnp.int32, (R, L), 0) // C) == (lax.broadcasted_iota(jnp.int32, (R, L), 1) // HEAD_DIM)

    def stack(x):
        return jnp.concatenate([x] * Hg, axis=0)

    def blk(x):
        return jnp.where(head_mask, stack(x), 0.0)

    lhs = jnp.concatenate([blk(ab), blk(rb)], axis=0)
    rhs = jnp.concatenate([stack(bb), stack(kb)], axis=0)
    res = _mm(lhs, rhs, _NT, p_small)
    row = lax.broadcasted_iota(jnp.int32, (R, R), 0)
    col = lax.broadcasted_iota(jnp.int32, (R, R), 1)
    same = (row // C) == (col // C)
    strict = same & ((col % C) < (row % C))
    incl = same & ((col % C) <= (row % C))
    m_bd = jnp.where(strict, res[:R, :R], 0.0)
    n_bd = jnp.where(strict, res[:R, R:], 0.0)
    p_bd = jnp.where(incl, res[R:, :R], 0.0)
    q_bd = jnp.where(incl, res[R:, R:], 0.0)

    vs = v_ref[0].reshape(R, HEAD_DIM)
    s = s_scr[...]
    from_state = _mm(lhs, s, _NT, p_state)
    from_v = _mm(jnp.concatenate([n_bd, q_bd], axis=0), vs, _NN, p_small)

    x = from_state[:R] + from_v[:R]
    mp = m_bd
    nsq = C.bit_length() - 1
    for j in range(nsq):
        x = x + _mm(mp, x, _NN, p_inv)
        if j < nsq - 1:
            mp = _mm(mp, mp, _NN, p_inv)
    u = x

    y = from_state[R:] + from_v[R:] + _mm(p_bd, u, _NN, p_small)
    y_ref[0] = y.reshape(Hg, C, HEAD_DIM)

    uv = jnp.concatenate([u, vs], axis=0)
    bk = jnp.concatenate([blk(bl), blk(kl)], axis=0)
    s_new = g_end * s + _mm(uv, bk, _TN, p_state)
    s_scr[...] = s_new

    @pl.when(c == nchunks - 1)
    def _():
        sout_ref[0] = s_new


def _wkv(r, lw, kf, kk, kka, v, s0, *, chunk, heads, p_small=1, p_inv=3, p_state=3):
    B, T, _ = r.shape
    C, Hg = chunk, heads
    L = Hg * HEAD_DIM
    G = N_HEADS // Hg
    nchunks = T // C
    v_st = v.reshape(B, T, N_HEADS, HEAD_DIM).transpose(0, 2, 1, 3)
    s0_g = s0.transpose(0, 2, 1, 3).reshape(B, HEAD_DIM, RWKV_W)
    nat = pl.BlockSpec((1, C, L), lambda b, g, c: (b, c, g))
    st = pl.BlockSpec((1, HEAD_DIM, L), lambda b, g, c: (b, 0, g))
    vsp = pl.BlockSpec((1, Hg, C, HEAD_DIM), lambda b, g, c: (b, g, c, 0))
    y_st, s_out = pl.pallas_call(
        functools.partial(_wkv_kernel, chunk=C, heads=Hg, nchunks=nchunks,
                          p_small=p_small, p_inv=p_inv, p_state=p_state),
        grid=(B, G, nchunks),
        in_specs=[nat, nat, nat, nat, nat, vsp, st],
        out_specs=[vsp, st],
        out_shape=[jax.ShapeDtypeStruct((B, N_HEADS, T, HEAD_DIM), f32),
                   jax.ShapeDtypeStruct((B, HEAD_DIM, RWKV_W), f32)],
        scratch_shapes=[pltpu.VMEM((HEAD_DIM, L), f32)],
        compiler_params=pltpu.CompilerParams(
            dimension_semantics=("parallel", "parallel", "arbitrary"), vmem_limit_bytes=VMEM_LIMIT),
        name="wkv",
    )(r, lw, kf, kk, kka, v_st, s0_g)
    y = y_st.transpose(0, 2, 1, 3).reshape(B, T, RWKV_W)
    s_new = s_out.reshape(B, HEAD_DIM, N_HEADS, HEAD_DIM).transpose(0, 2, 1, 3)
    return y, s_new


def _xattn_kernel(q_ref, k_ref, v_ref, o_ref):
    q = q_ref[0]
    kx = k_ref[0].astype(bf16)
    vx = v_ref[0].astype(bf16)
    outs = []
    for h in range(XA_HEADS):
        sl = slice(h * XA_HEAD_DIM, (h + 1) * XA_HEAD_DIM)
        s = _dot(q[:, sl], kx[:, sl], _NT) * (XA_HEAD_DIM ** -0.5)
        s = s - jnp.max(s, axis=-1, keepdims=True)
        e = jnp.exp(s)
        p = e / jnp.sum(e, axis=-1, keepdims=True)
        outs.append(_dot(p.astype(bf16), vx[:, sl], _NN))
    o_ref[0] = jnp.concatenate(outs, axis=-1).astype(o_ref.dtype)


def _xattn(q, mk, mv, tq):
    B, T, _ = q.shape
    tq = min(tq, T)
    return pl.pallas_call(
        _xattn_kernel,
        grid=(B, T // tq),
        in_specs=[pl.BlockSpec((1, tq, XA_W), lambda b, i: (b, i, 0)),
                  pl.BlockSpec((1, MEM_TOKENS, XA_W), lambda b, i: (b, 0, 0)),
                  pl.BlockSpec((1, MEM_TOKENS, XA_W), lambda b, i: (b, 0, 0))],
        out_specs=pl.BlockSpec((1, tq, XA_W), lambda b, i: (b, i, 0)),
        out_shape=jax.ShapeDtypeStruct((B, T, XA_W), bf16),
        compiler_params=pltpu.CompilerParams(
            dimension_semantics=("parallel", "parallel"), vmem_limit_bytes=VMEM_LIMIT),
        name="xattn",
    )(q, mk, mv)


def _pad_shift(x):
    z = jnp.zeros(x.shape[:-1] + (LORA_PAD - DECAY_RANK,), x.dtype)
    return jnp.concatenate([
        x[..., :OFF_WD], x[..., OFF_WD:OFF_WD + DECAY_RANK], z,
        x[..., OFF_WD + DECAY_RANK:OFF_WD + DECAY_RANK + AAA_RANK], z,
        x[..., OFF_WD + DECAY_RANK + AAA_RANK:]], axis=-1)


def _unpad_shift(x):
    return jnp.concatenate([
        x[..., :OFF_WD + DECAY_RANK], x[..., OFF_AD:OFF_AD + AAA_RANK], x[..., OFF_GD:]], axis=-1)


def _block(x, mk, mv, shift_prev, pool_prev, wkv_prev, start_pos, P, chunk, heads):
    B, T, _ = x.shape
    M = B * T
    x2 = x.reshape(M, D_MODEL)

    h = _rmsnorm(x2, P["norm_mix_g"], bf16)
    z = _matmul(h, P["w_in"], tn=512).reshape(B, T, IN_WP)
    zs, zp = z[..., :SHIFT_WP], z[..., SHIFT_WP:]
    zs_prev = jnp.concatenate([_pad_shift(shift_prev)[:, None, :], zs[:, :-1]], axis=1)
    zm = zs + P["mu_shift"] * (zs_prev - zs)
    r, k, v = zm[..., :RWKV_W], zm[..., RWKV_W:2 * RWKV_W], zm[..., 2 * RWKV_W:3 * RWKV_W]
    wd = zm[..., OFF_WD:OFF_WD + DECAY_RANK]
    ad = zm[..., OFF_AD:OFF_AD + AAA_RANK]
    gd = zm[..., OFF_GD:OFF_GD + GATE_RANK]

    w_log = -jax.nn.softplus(-(P["w0"] + jnp.tanh(wd) @ P["w_decay_up"])) - 0.5
    lw = -jnp.exp(w_log)
    a = jax.nn.sigmoid(P["a0"] + ad @ P["w_a_up"])
    g = jax.nn.sigmoid(gd) @ P["w_g_up"]
    hd = lambda t: t.reshape(B, T, N_HEADS, HEAD_DIM)
    kk = hd(k * P["k_k"])
    kk = kk / jnp.maximum(jnp.sqrt(jnp.sum(kk * kk, axis=-1, keepdims=True)), 1e-12)
    kk = kk.reshape(B, T, RWKV_W)
    kf = k * (1.0 + (a - 1.0) * P["k_a"])
    y, s_new = _wkv(r, lw, kf, kk, kk * a, v, wkv_prev, chunk=chunk, heads=heads)
    y = hd(y)
    mu = jnp.mean(y, axis=-1, keepdims=True)
    var = jnp.mean(jnp.square(y - mu), axis=-1, keepdims=True)
    y = ((y - mu) * lax.rsqrt(var + GN_EPS)).reshape(B, T, RWKV_W)
    y = y * P["lnx_g"] + P["lnx_b"]
    bonus = jnp.sum(hd(r) * hd(kf) * P["r_k"], axis=-1, keepdims=True) * hd(v)
    o_rwkv = ((y + bonus.reshape(B, T, RWKV_W)) * g).astype(bf16)

    seq = jnp.concatenate([pool_prev, zp], axis=1)
    pos = start_pos + jnp.arange(T)
    ds = []
    for gi, wdw in enumerate(POOL_WINDOWS):
        c0, c1 = gi * POOL_GW, (gi + 1) * POOL_GW
        sg = seq[:, :, c0:c1]
        acc = sum(sg[:, POOL_BUF - j:POOL_BUF - j + T] for j in range(wdw))
        cnt = jnp.minimum(wdw, pos + 1).astype(f32)[None, :, None]
        ds.append((acc / cnt - zp[:, :, c0:c1]).astype(bf16).reshape(M, POOL_GW))
    o_pool = jnp.concatenate(
        [_matmul(ds[gi], P["w_pool"][gi], tn=512, tk=512) for gi in range(len(POOL_WINDOWS))], axis=-1)
    o_pool = (o_pool * P["pool_scale"]).astype(bf16)

    mix_in = jnp.concatenate([o_rwkv.reshape(M, RWKV_W), o_pool], axis=-1)
    x2 = _matmul(mix_in, P["w_out"], res=x2)

    h = _rmsnorm(x2, P["norm_xa_g"], bf16)
    q = _matmul(h, P["w_xq"], out_dtype=bf16, tn=512).reshape(B, T, XA_W)
    o = _xattn(q, mk.reshape(B, MEM_TOKENS, XA_W), mv.reshape(B, MEM_TOKENS, XA_W), tq=512)
    x2 = _matmul(o.reshape(M, XA_W), P["w_xo"], res=x2, tk=512)

    h = _rmsnorm(x2, P["norm_ffn_g"], bf16)
    hid = _matmul(h, P["w_up"], act="relu2", out_dtype=bf16)
    x2 = _matmul(hid, P["w_down"], res=x2)
    y_out = _rmsnorm(x2, P["norm_final_g"], f32).reshape(B, T, D_MODEL)

    return y_out, _unpad_shift(zs[:, -1]), seq[:, -POOL_BUF:], s_new


def kernel(x_prompt, x_sample, mem_prompt, state_wkv, state_shift, state_pool, cache_mem_k, cache_mem_v, norm_mix_g, w_in, mu_shift, w0, w_decay_up, a0, w_a_up, w_g_up, k_k, k_a, r_k, lnx_g, lnx_b, w_pool, pool_scale, w_out, norm_xa_g, norm_mem_g, w_xq, w_mk, w_mv, w_xo, norm_ffn_g, w_up, w_down, norm_final_g):
    l = 0
    Bp = x_prompt.shape[0]
    zcol = jnp.zeros((D_MODEL, LORA_PAD - DECAY_RANK), f32)
    wi = w_in[l]
    w_in_p = jnp.concatenate([
        wi[:, :OFF_WD + DECAY_RANK], zcol,
        wi[:, OFF_WD + DECAY_RANK:OFF_WD + DECAY_RANK + AAA_RANK], zcol,
        wi[:, OFF_WD + DECAY_RANK + AAA_RANK:]], axis=1).astype(bf16)
    P = {
        "norm_mix_g": norm_mix_g[l], "w_in": w_in_p, "mu_shift": _pad_shift(mu_shift[l]),
        "w0": w0[l], "w_decay_up": w_decay_up[l], "a0": a0[l], "w_a_up": w_a_up[l], "w_g_up": w_g_up[l],
        "k_k": k_k[l], "k_a": k_a[l], "r_k": r_k[l], "lnx_g": lnx_g[l], "lnx_b": lnx_b[l],
        "w_pool": w_pool[l].astype(bf16), "pool_scale": pool_scale[l], "w_out": w_out[l].astype(bf16),
        "norm_xa_g": norm_xa_g[l], "w_xq": w_xq[l].astype(bf16), "w_xo": w_xo[l].astype(bf16),
        "norm_ffn_g": norm_ffn_g[l], "w_up": w_up[l].astype(bf16), "w_down": w_down[l].astype(bf16),
        "norm_final_g": norm_final_g,
    }
    mem = _rmsnorm(mem_prompt.reshape(Bp * MEM_TOKENS, D_MODEL), norm_mem_g[l], bf16)
    mkv = _matmul(mem, jnp.concatenate([w_mk[l], w_mv[l]], axis=1).astype(bf16), tn=512)
    mk_p = mkv[:, :XA_W].reshape(Bp, MEM_TOKENS, XA_HEADS, XA_HEAD_DIM)
    mv_p = mkv[:, XA_W:].reshape(Bp, MEM_TOKENS, XA_HEADS, XA_HEAD_DIM)

    yp, sh_p, pl_p, wkv_p = _block(
        x_prompt, mk_p, mv_p, jnp.zeros((Bp, SHIFT_W), f32), jnp.zeros((Bp, POOL_BUF, POOL_W), f32),
        jnp.zeros((Bp, N_HEADS, HEAD_DIM, HEAD_DIM), f32), 0, P, chunk=64, heads=4)
    ys, sh_s, pl_s, wkv_s = _block(
        x_sample, cache_mem_k[l], cache_mem_v[l], state_shift[l], state_pool[l], state_wkv[l],
        PAST_LEN, P, chunk=8, heads=32)
    ex = lambda t: t[None]
    return (yp, ys, ex(wkv_p), ex(sh_p), ex(pl_p), ex(mk_p), ex(mv_p), ex(wkv_s), ex(sh_s), ex(pl_s))
```

```python
import functools

import jax
import jax.numpy as jnp
from jax import lax
from jax.experimental import pallas as pl
from jax.experimental.pallas import tpu as pltpu

D_MODEL = 4096
RWKV_W = 2048
POOL_W = 2048
HEAD_DIM = 64
N_HEADS = RWKV_W // HEAD_DIM
DECAY_RANK = 96
AAA_RANK = 96
GATE_RANK = 256
SHIFT_W = 3 * RWKV_W + DECAY_RANK + AAA_RANK + GATE_RANK
POOL_WINDOWS = (2, 4, 8, 16)
POOL_GW = POOL_W // len(POOL_WINDOWS)
POOL_BUF = max(POOL_WINDOWS) - 1
MEM_TOKENS = 256
XA_HEADS = 4
XA_HEAD_DIM = 128
XA_W = XA_HEADS * XA_HEAD_DIM
PAST_LEN = 16384
RMS_EPS = 1e-6
GN_EPS = 64e-5

LORA_PAD = 128
OFF_WD = 3 * RWKV_W
OFF_AD = OFF_WD + LORA_PAD
OFF_GD = OFF_AD + LORA_PAD
SHIFT_WP = OFF_GD + GATE_RANK
IN_WP = SHIFT_WP + POOL_W

VMEM_LIMIT = 56 * 1024 * 1024

_NN = ((1,), (0,))
_NT = ((1,), (1,))
_TN = ((0,), (0,))

f32 = jnp.float32
bf16 = jnp.bfloat16


def _dot(a, b, dims):
    return lax.dot_general(a, b, (dims, ((), ())), preferred_element_type=f32)


def _rmsnorm_kernel(x_ref, g_ref, o_ref):
    x = x_ref[...]
    y = x * lax.rsqrt(jnp.mean(x * x, axis=-1, keepdims=True) + RMS_EPS)
    o_ref[...] = (y * g_ref[...]).astype(o_ref.dtype)


def _rmsnorm(x, g, out_dtype, tm=256):
    m, d = x.shape
    return pl.pallas_call(
        _rmsnorm_kernel,
        grid=(m // tm,),
        in_specs=[pl.BlockSpec((tm, d), lambda i: (i, 0)), pl.BlockSpec((1, d), lambda i: (0, 0))],
        out_specs=pl.BlockSpec((tm, d), lambda i: (i, 0)),
        out_shape=jax.ShapeDtypeStruct((m, d), out_dtype),
        compiler_params=pltpu.CompilerParams(dimension_semantics=("parallel",), vmem_limit_bytes=VMEM_LIMIT),
        name="rmsnorm",
    )(x, g.reshape(1, d))


def _matmul_kernel(a_ref, b_ref, *rest, nk, act, has_res):
    if has_res:
        res_ref, o_ref, acc_ref = rest
    else:
        o_ref, acc_ref = rest
    k = pl.program_id(2)

    @pl.when(k == 0)
    def _():
        acc_ref[...] = jnp.zeros_like(acc_ref)

    acc_ref[...] += jnp.dot(a_ref[...], b_ref[...], preferred_element_type=f32)

    @pl.when(k == nk - 1)
    def _():
        r = acc_ref[...]
        if act == "relu2":
            r = jnp.square(jnp.maximum(r, 0.0))
        if has_res:
            r = r + res_ref[...]
        o_ref[...] = r.astype(o_ref.dtype)


def _matmul(a, b, res=None, act=None, out_dtype=f32, tm=1024, tn=1024, tk=1024):
    m, kd = a.shape
    _, n = b.shape
    tm, tn, tk = min(tm, m), min(tn, n), min(tk, kd)
    assert m % tm == 0 and n % tn == 0 and kd % tk == 0, (a.shape, b.shape, tm, tn, tk)
    nk = kd // tk
    in_specs = [pl.BlockSpec((tm, tk), lambda i, j, k: (i, k)), pl.BlockSpec((tk, tn), lambda i, j, k: (k, j))]
    args = [a, b]
    if res is not None:
        in_specs.append(pl.BlockSpec((tm, tn), lambda i, j, k: (i, j)))
        args.append(res)
    return pl.pallas_call(
        functools.partial(_matmul_kernel, nk=nk, act=act, has_res=res is not None),
        grid=(m // tm, n // tn, nk),
        in_specs=in_specs,
        out_specs=pl.BlockSpec((tm, tn), lambda i, j, k: (i, j)),
        out_shape=jax.ShapeDtypeStruct((m, n), out_dtype),
        scratch_shapes=[pltpu.VMEM((tm, tn), f32)],
        compiler_params=pltpu.CompilerParams(
            dimension_semantics=("parallel", "parallel", "arbitrary"), vmem_limit_bytes=VMEM_LIMIT),
        name="matmul",
    )(*args)


def _wkv_kernel(r_ref, lw_ref, kf_ref, kk_ref, kka_ref, v_ref, s0_ref, y_ref, sout_ref,
                s_scr, hm_scr, tm_scr, *, chunk, heads, nsub, nchunks):
    C, Hg = chunk, heads
    L, R = Hg * HEAD_DIM, Hg * C
    c = pl.program_id(2)

    @pl.when(c == 0)
    def _():
        s_scr[...] = s0_ref[0]
        hm_scr[...] = ((lax.broadcasted_iota(jnp.int32, (R, L), 0) // C)
                       == (lax.broadcasted_iota(jnp.int32, (R, L), 1) // HEAD_DIM)).astype(bf16)
        row = lax.broadcasted_iota(jnp.int32, (R, R), 0)
        col = lax.broadcasted_iota(jnp.int32, (R, R), 1)
        same = (row // C) == (col // C)
        tm_scr[0] = (same & ((col % C) < (row % C))).astype(bf16)
        tm_scr[1] = (same & ((col % C) <= (row % C))).astype(bf16)

    tri = (lax.broadcasted_iota(jnp.int32, (C, C), 1) <= lax.broadcasted_iota(jnp.int32, (C, C), 0)).astype(bf16)
    hm = hm_scr[...]
    strict, incl = tm_scr[0], tm_scr[1]

    def stack16(x):
        if C % 16 == 0:
            return jnp.concatenate([x.astype(bf16)] * Hg, axis=0)
        return jnp.concatenate([x] * Hg, axis=0).astype(bf16)

    subs = range(nsub)
    lanes = [slice(j * L, (j + 1) * L) for j in subs]
    hsl = [slice(j * Hg, (j + 1) * Hg) for j in subs]

    def cumsum_t(lw):
        h1 = lw.astype(bf16)
        r1 = lw - h1.astype(f32)
        h2 = r1.astype(bf16)
        h3 = (r1 - h2.astype(f32)).astype(bf16)
        return _dot(tri, h1, _NN) + (_dot(tri, h2, _NN) + _dot(tri, h3, _NN))

    lw = [lw_ref[0, :, ln] for ln in lanes]
    cum = [cumsum_t(lw[j]) for j in subs]
    g = [jnp.exp(cum[j]) for j in subs]
    gi = [jnp.exp(-cum[j]) for j in subs]
    gprev = [jnp.exp(cum[j] - lw[j]) for j in subs]
    g_end = [g[j][C - 1:C, :] for j in subs]
    kk = [kk_ref[0, :, ln] for ln in lanes]
    kka = [kka_ref[0, :, ln] for ln in lanes]
    kf = [kf_ref[0, :, ln] for ln in lanes]
    lhs = [jnp.concatenate([stack16(-kk[j] * gprev[j]) * hm, stack16(r_ref[0, :, lanes[j]] * g[j]) * hm], axis=0)
           for j in subs]
    rhs = [jnp.concatenate([stack16(kka[j] * gi[j]), stack16(kf[j] * gi[j])], axis=0) for j in subs]
    res = [_dot(lhs[j], rhs[j], _NT).astype(bf16) for j in subs]
    m_bd = [res[j][:R, :R] * strict for j in subs]
    nq = [jnp.concatenate([res[j][:R, R:] * strict, res[j][R:, R:] * incl], axis=0) for j in subs]
    p_bd = [res[j][R:, :R] * incl for j in subs]

    vs = [v_ref[0, hsl[j]].reshape(R, HEAD_DIM).astype(bf16) for j in subs]
    s = [s_scr[:, ln] for ln in lanes]
    base = [_dot(lhs[j], s[j].astype(bf16), _NT) + _dot(nq[j], vs[j], _NN) for j in subs]

    x = [base[j][:R] for j in subs]
    mp = m_bd
    nsq = C.bit_length() - 1
    for i in range(nsq):
        x = [x[j] + _dot(mp[j], x[j].astype(bf16), _NN) for j in subs]
        if i < nsq - 1:
            mp = [_dot(mp[j], mp[j], _NN).astype(bf16) for j in subs]
    u = [x[j].astype(bf16) for j in subs]

    for j in subs:
        y = base[j][R:] + _dot(p_bd[j], u[j], _NN)
        y_ref[0, hsl[j]] = y.reshape(Hg, C, HEAD_DIM)
    for j in subs:
        gl = g_end[j] * gi[j]
        uv = jnp.concatenate([u[j], vs[j]], axis=0)
        bk = jnp.concatenate([stack16(kka[j] * gl) * hm, stack16(kf[j] * gl) * hm], axis=0)
        s_scr[:, lanes[j]] = g_end[j] * s[j] + _dot(uv, bk, _TN)

    @pl.when(c == nchunks - 1)
    def _():
        sout_ref[0] = s_scr[...]


def _wkv(r, lw, kf, kk, kka, v, s0, *, chunk, heads, nsub):
    B, T, _ = r.shape
    C, Hg = chunk, heads
    L, R = Hg * HEAD_DIM, Hg * C
    Lb, Hb = nsub * L, nsub * Hg
    G = N_HEADS // Hb
    nchunks = T // C
    v_st = v.reshape(B, T, N_HEADS, HEAD_DIM).transpose(0, 2, 1, 3)
    s0_g = s0.transpose(0, 2, 1, 3).reshape(B, HEAD_DIM, RWKV_W)
    nat = pl.BlockSpec((1, C, Lb), lambda b, g, c: (b, c, g))
    st = pl.BlockSpec((1, HEAD_DIM, Lb), lambda b, g, c: (b, 0, g))
    vsp = pl.BlockSpec((1, Hb, C, HEAD_DIM), lambda b, g, c: (b, g, c, 0))
    y_st, s_out = pl.pallas_call(
        functools.partial(_wkv_kernel, chunk=C, heads=Hg, nsub=nsub, nchunks=nchunks),
        grid=(B, G, nchunks),
        in_specs=[nat, nat, nat, nat, nat, vsp, st],
        out_specs=[vsp, st],
        out_shape=[jax.ShapeDtypeStruct((B, N_HEADS, T, HEAD_DIM), f32),
                   jax.ShapeDtypeStruct((B, HEAD_DIM, RWKV_W), f32)],
        scratch_shapes=[pltpu.VMEM((HEAD_DIM, Lb), f32), pltpu.VMEM((R, L), bf16), pltpu.VMEM((2, R, R), bf16)],
        compiler_params=pltpu.CompilerParams(
            dimension_semantics=("parallel", "parallel", "arbitrary"), vmem_limit_bytes=VMEM_LIMIT),
        name="wkv",
    )(r, lw, kf, kk, kka, v_st, s0_g)
    y = y_st.transpose(0, 2, 1, 3).reshape(B, T, RWKV_W)
    s_new = s_out.reshape(B, HEAD_DIM, N_HEADS, HEAD_DIM).transpose(0, 2, 1, 3)
    return y, s_new


def _xattn_kernel(q_ref, k_ref, v_ref, o_ref):
    q = q_ref[0]
    kx = k_ref[0].astype(bf16)
    vx = v_ref[0].astype(bf16)
    outs = []
    for h in range(XA_HEADS):
        sl = slice(h * XA_HEAD_DIM, (h + 1) * XA_HEAD_DIM)
        s = _dot(q[:, sl], kx[:, sl], _NT) * (XA_HEAD_DIM ** -0.5)
        s = s - jnp.max(s, axis=-1, keepdims=True)
        e = jnp.exp(s)
        p = e / jnp.sum(e, axis=-1, keepdims=True)
        outs.append(_dot(p.astype(bf16), vx[:, sl], _NN))
    o_ref[0] = jnp.concatenate(outs, axis=-1).astype(o_ref.dtype)


def _xattn(q, mk, mv, tq):
    B, T, _ = q.shape
    tq = min(tq, T)
    return pl.pallas_call(
        _xattn_kernel,
        grid=(B, T // tq),
        in_specs=[pl.BlockSpec((1, tq, XA_W), lambda b, i: (b, i, 0)),
                  pl.BlockSpec((1, MEM_TOKENS, XA_W), lambda b, i: (b, 0, 0)),
                  pl.BlockSpec((1, MEM_TOKENS, XA_W), lambda b, i: (b, 0, 0))],
        out_specs=pl.BlockSpec((1, tq, XA_W), lambda b, i: (b, i, 0)),
        out_shape=jax.ShapeDtypeStruct((B, T, XA_W), bf16),
        compiler_params=pltpu.CompilerParams(
            dimension_semantics=("parallel", "parallel"), vmem_limit_bytes=VMEM_LIMIT),
        name="xattn",
    )(q, mk, mv)


def _pad_shift(x):
    z = jnp.zeros(x.shape[:-1] + (LORA_PAD - DECAY_RANK,), x.dtype)
    return jnp.concatenate([
        x[..., :OFF_WD], x[..., OFF_WD:OFF_WD + DECAY_RANK], z,
        x[..., OFF_WD + DECAY_RANK:OFF_WD + DECAY_RANK + AAA_RANK], z,
        x[..., OFF_WD + DECAY_RANK + AAA_RANK:]], axis=-1)


def _unpad_shift(x):
    return jnp.concatenate([
        x[..., :OFF_WD + DECAY_RANK], x[..., OFF_AD:OFF_AD + AAA_RANK], x[..., OFF_GD:]], axis=-1)


def _block(x, mk, mv, shift_prev, pool_prev, wkv_prev, start_pos, P, chunk, heads, nsub):
    B, T, _ = x.shape
    M = B * T
    x2 = x.reshape(M, D_MODEL)

    h = _rmsnorm(x2, P["norm_mix_g"], bf16)
    z = _matmul(h, P["w_in"], tn=512).reshape(B, T, IN_WP)
    zs, zp = z[..., :SHIFT_WP], z[..., SHIFT_WP:]
    zs_prev = jnp.concatenate([_pad_shift(shift_prev)[:, None, :], zs[:, :-1]], axis=1)
    zm = zs + P["mu_shift"] * (zs_prev - zs)
    r, k, v = zm[..., :RWKV_W], zm[..., RWKV_W:2 * RWKV_W], zm[..., 2 * RWKV_W:3 * RWKV_W]
    wd = zm[..., OFF_WD:OFF_WD + DECAY_RANK]
    ad = zm[..., OFF_AD:OFF_AD + AAA_RANK]
    gd = zm[..., OFF_GD:OFF_GD + GATE_RANK]

    w_log = -jax.nn.softplus(-(P["w0"] + jnp.tanh(wd) @ P["w_decay_up"])) - 0.5
    lw = -jnp.exp(w_log)
    a = jax.nn.sigmoid(P["a0"] + ad @ P["w_a_up"])
    g = jax.nn.sigmoid(gd) @ P["w_g_up"]
    hd = lambda t: t.reshape(B, T, N_HEADS, HEAD_DIM)
    kk = hd(k * P["k_k"])
    kk = kk / jnp.maximum(jnp.sqrt(jnp.sum(kk * kk, axis=-1, keepdims=True)), 1e-12)
    kk = kk.reshape(B, T, RWKV_W)
    kf = k * (1.0 + (a - 1.0) * P["k_a"])
    y, s_new = _wkv(r, lw, kf, kk, kk * a, v, wkv_prev, chunk=chunk, heads=heads, nsub=nsub)
    y = hd(y)
    mu = jnp.mean(y, axis=-1, keepdims=True)
    var = jnp.mean(jnp.square(y - mu), axis=-1, keepdims=True)
    y = ((y - mu) * lax.rsqrt(var + GN_EPS)).reshape(B, T, RWKV_W)
    y = y * P["lnx_g"] + P["lnx_b"]
    bonus = jnp.sum(hd(r) * hd(kf) * P["r_k"], axis=-1, keepdims=True) * hd(v)
    o_rwkv = ((y + bonus.reshape(B, T, RWKV_W)) * g).astype(bf16)

    seq = jnp.concatenate([pool_prev, zp], axis=1)
    pos = start_pos + jnp.arange(T)
    ds = []
    for gi, wdw in enumerate(POOL_WINDOWS):
        c0, c1 = gi * POOL_GW, (gi + 1) * POOL_GW
        sg = seq[:, :, c0:c1]
        acc = sum(sg[:, POOL_BUF - j:POOL_BUF - j + T] for j in range(wdw))
        cnt = jnp.minimum(wdw, pos + 1).astype(f32)[None, :, None]
        ds.append((acc / cnt - zp[:, :, c0:c1]).astype(bf16).reshape(M, POOL_GW))
    o_pool = jnp.concatenate(
        [_matmul(ds[gi], P["w_pool"][gi], tn=512, tk=512) for gi in range(len(POOL_WINDOWS))], axis=-1)
    o_pool = (o_pool * P["pool_scale"]).astype(bf16)

    mix_in = jnp.concatenate([o_rwkv.reshape(M, RWKV_W), o_pool], axis=-1)
    x2 = _matmul(mix_in, P["w_out"], res=x2)

    h = _rmsnorm(x2, P["norm_xa_g"], bf16)
    q = _matmul(h, P["w_xq"], out_dtype=bf16, tn=512).reshape(B, T, XA_W)
    o = _xattn(q, mk.reshape(B, MEM_TOKENS, XA_W), mv.reshape(B, MEM_TOKENS, XA_W), tq=512)
    x2 = _matmul(o.reshape(M, XA_W), P["w_xo"], res=x2, tk=512)

    h = _rmsnorm(x2, P["norm_ffn_g"], bf16)
    hid = _matmul(h, P["w_up"], act="relu2", out_dtype=bf16)
    x2 = _matmul(hid, P["w_down"], res=x2)
    y_out = _rmsnorm(x2, P["norm_final_g"], f32).reshape(B, T, D_MODEL)

    return y_out, _unpad_shift(zs[:, -1]), seq[:, -POOL_BUF:], s_new


def kernel(x_prompt, x_sample, mem_prompt, state_wkv, state_shift, state_pool, cache_mem_k, cache_mem_v, norm_mix_g, w_in, mu_shift, w0, w_decay_up, a0, w_a_up, w_g_up, k_k, k_a, r_k, lnx_g, lnx_b, w_pool, pool_scale, w_out, norm_xa_g, norm_mem_g, w_xq, w_mk, w_mv, w_xo, norm_ffn_g, w_up, w_down, norm_final_g):
    l = 0
    Bp = x_prompt.shape[0]
    zcol = jnp.zeros((D_MODEL, LORA_PAD - DECAY_RANK), f32)
    wi = w_in[l]
    w_in_p = jnp.concatenate([
        wi[:, :OFF_WD + DECAY_RANK], zcol,
        wi[:, OFF_WD + DECAY_RANK:OFF_WD + DECAY_RANK + AAA_RANK], zcol,
        wi[:, OFF_WD + DECAY_RANK + AAA_RANK:]], axis=1).astype(bf16)
    P = {
        "norm_mix_g": norm_mix_g[l], "w_in": w_in_p, "mu_shift": _pad_shift(mu_shift[l]),
        "w0": w0[l], "w_decay_up": w_decay_up[l], "a0": a0[l], "w_a_up": w_a_up[l], "w_g_up": w_g_up[l],
        "k_k": k_k[l], "k_a": k_a[l], "r_k": r_k[l], "lnx_g": lnx_g[l], "lnx_b": lnx_b[l],
        "w_pool": w_pool[l].astype(bf16), "pool_scale": pool_scale[l], "w_out": w_out[l].astype(bf16),
        "norm_xa_g": norm_xa_g[l], "w_xq": w_xq[l].astype(bf16), "w_xo": w_xo[l].astype(bf16),
        "norm_ffn_g": norm_ffn_g[l], "w_up": w_up[l].astype(bf16), "w_down": w_down[l].astype(bf16),
        "norm_final_g": norm_final_g,
    }
    mem = _rmsnorm(mem_prompt.reshape(Bp * MEM_TOKENS, D_MODEL), norm_mem_g[l], bf16)
    mkv = _matmul(mem, jnp.concatenate([w_mk[l], w_mv[l]], axis=1).astype(bf16), tn=512)
    mk_p = mkv[:, :XA_W].reshape(Bp, MEM_TOKENS, XA_HEADS, XA_HEAD_DIM)
    mv_p = mkv[:, XA_W:].reshape(Bp, MEM_TOKENS, XA_HEADS, XA_HEAD_DIM)

    yp, sh_p, pl_p, wkv_p = _block(
        x_prompt, mk_p, mv_p, jnp.zeros((Bp, SHIFT_W), f32), jnp.zeros((Bp, POOL_BUF, POOL_W), f32),
        jnp.zeros((Bp, N_HEADS, HEAD_DIM, HEAD_DIM), f32), 0, P, chunk=64, heads=4, nsub=4)
    ys, sh_s, pl_s, wkv_s = _block(
        x_sample, cache_mem_k[l], cache_mem_v[l], state_shift[l], state_pool[l], state_wkv[l],
        PAST_LEN, P, chunk=8, heads=32, nsub=1)
    ex = lambda t: t[None]
    return (yp, ys, ex(wkv_p), ex(sh_p), ex(pl_p), ex(mk_p), ex(mv_p), ex(wkv_s), ex(sh_s), ex(pl_s))
```

```python
import functools

import jax
import jax.numpy as jnp
from jax import lax
from jax.experimental import pallas as pl
from jax.experimental.pallas import tpu as pltpu

D_MODEL = 4096
RWKV_W = 2048
POOL_W = 2048
HEAD_DIM = 64
N_HEADS = RWKV_W // HEAD_DIM
DECAY_RANK = 96
AAA_RANK = 96
GATE_RANK = 256
SHIFT_W = 3 * RWKV_W + DECAY_RANK + AAA_RANK + GATE_RANK
POOL_WINDOWS = (2, 4, 8, 16)
N_POOL = len(POOL_WINDOWS)
POOL_GW = POOL_W // N_POOL
POOL_BUF = max(POOL_WINDOWS) - 1
POOL_CARRY = POOL_BUF + 1
MEM_TOKENS = 256
XA_HEADS = 4
XA_HEAD_DIM = 128
XA_W = XA_HEADS * XA_HEAD_DIM
PAST_LEN = 16384
RMS_EPS = 1e-6
GN_EPS = 64e-5

LORA_PAD = 128
OFF_WD = 3 * RWKV_W
OFF_AD = OFF_WD + LORA_PAD
OFF_GD = OFF_AD + LORA_PAD
LORA_W = 2 * LORA_PAD + GATE_RANK
SHIFT_WP = OFF_GD + GATE_RANK
IN_WP = SHIFT_WP + POOL_W
SUB_HEADS = 4
SUB_L = SUB_HEADS * HEAD_DIM
N_PVEC = 16

VMEM_LIMIT = 56 * 1024 * 1024

_NN = ((1,), (0,))
_NT = ((1,), (1,))
_TN = ((0,), (0,))

f32 = jnp.float32
bf16 = jnp.bfloat16


def _dot(a, b, dims):
    return lax.dot_general(a, b, (dims, ((), ())), preferred_element_type=f32)


def _sigmoid(x):
    return 1.0 / (1.0 + jnp.exp(-x))


def _rmsnorm_kernel(x_ref, g_ref, o_ref):
    x = x_ref[...]
    y = x * lax.rsqrt(jnp.mean(x * x, axis=-1, keepdims=True) + RMS_EPS)
    o_ref[...] = (y * g_ref[...]).astype(o_ref.dtype)


def _rmsnorm(x, g, out_dtype, tm=256):
    m, d = x.shape
    return pl.pallas_call(
        _rmsnorm_kernel,
        grid=(m // tm,),
        in_specs=[pl.BlockSpec((tm, d), lambda i: (i, 0)), pl.BlockSpec((1, d), lambda i: (0, 0))],
        out_specs=pl.BlockSpec((tm, d), lambda i: (i, 0)),
        out_shape=jax.ShapeDtypeStruct((m, d), out_dtype),
        compiler_params=pltpu.CompilerParams(dimension_semantics=("parallel",), vmem_limit_bytes=VMEM_LIMIT),
        name="rmsnorm",
    )(x, g.reshape(1, d))


def _matmul_kernel(*refs, nk, nk1, act, has_res, two_lhs):
    refs = list(refs)
    a_ref = refs.pop(0)
    a2_ref = refs.pop(0) if two_lhs else None
    b_ref = refs.pop(0)
    res_ref = refs.pop(0) if has_res else None

    def finish(r):
        if act == "relu2":
            r = jnp.square(jnp.maximum(r, 0.0))
        if has_res:
            r = r + res_ref[...]
        return r

    if nk == 1:
        (o_ref,) = refs
        o_ref[...] = finish(jnp.dot(a_ref[...], b_ref[...], preferred_element_type=f32)).astype(o_ref.dtype)
        return

    o_ref, acc_ref = refs
    k = pl.program_id(2)

    @pl.when(k == 0)
    def _():
        acc_ref[...] = jnp.zeros_like(acc_ref)

    if two_lhs:
        @pl.when(k < nk1)
        def _():
            acc_ref[...] += jnp.dot(a_ref[...], b_ref[...], preferred_element_type=f32)

        @pl.when(k >= nk1)
        def _():
            acc_ref[...] += jnp.dot(a2_ref[...], b_ref[...], preferred_element_type=f32)
    else:
        acc_ref[...] += jnp.dot(a_ref[...], b_ref[...], preferred_element_type=f32)

    @pl.when(k == nk - 1)
    def _():
        o_ref[...] = finish(acc_ref[...]).astype(o_ref.dtype)


def _matmul(a, b, res=None, act=None, out_dtype=f32, a2=None, tm=1024, tn=1024, tk=1024):
    m, k1 = a.shape
    kd, n = b.shape
    tm, tn, tk = min(tm, m), min(tn, n), min(tk, k1)
    assert m % tm == 0 and n % tn == 0 and kd % tk == 0 and k1 % tk == 0, (a.shape, b.shape, tm, tn, tk)
    nk, nk1 = kd // tk, k1 // tk
    two_lhs = a2 is not None
    if two_lhs:
        assert a2.shape == (m, kd - k1)
        in_specs = [pl.BlockSpec((tm, tk), lambda i, j, k: (i, jnp.minimum(k, nk1 - 1))),
                    pl.BlockSpec((tm, tk), lambda i, j, k: (i, jnp.maximum(k - nk1, 0)))]
        args = [a, a2]
    else:
        assert k1 == kd
        in_specs = [pl.BlockSpec((tm, tk), lambda i, j, k: (i, k))]
        args = [a]
    in_specs.append(pl.BlockSpec((tk, tn), lambda i, j, k: (k, j)))
    args.append(b)
    if res is not None:
        in_specs.append(pl.BlockSpec((tm, tn), lambda i, j, k: (i, j)))
        args.append(res)
    return pl.pallas_call(
        functools.partial(_matmul_kernel, nk=nk, nk1=nk1, act=act, has_res=res is not None, two_lhs=two_lhs),
        grid=(m // tm, n // tn, nk),
        in_specs=in_specs,
        out_specs=pl.BlockSpec((tm, tn), lambda i, j, k: (i, j)),
        out_shape=jax.ShapeDtypeStruct((m, n), out_dtype),
        scratch_shapes=[pltpu.VMEM((tm, tn), f32)] if nk > 1 else [],
        compiler_params=pltpu.CompilerParams(
            dimension_semantics=("parallel", "parallel", "arbitrary"), vmem_limit_bytes=VMEM_LIMIT),
        name="matmul",
    )(*args)


PV_MU_R, PV_MU_K, PV_MU_V, PV_W0, PV_A0, PV_KK, PV_KA, PV_RK, PV_LNG, PV_LNB = range(10)


def _rwkv_kernel(zr_ref, zk_ref, zv_ref, zl_ref, prkv_ref, pl_ref, pv_ref, mul_ref, wdu_ref, wau_ref, wgu_ref,
                 s0_ref, o_ref, sout_ref, s_scr, hm_scr, tm_scr, ones_scr, carry_scr,
                 *, chunk, heads, nsub, nchunks):
    C, Hg = chunk, heads
    L, R = Hg * HEAD_DIM, Hg * C
    Lb, Hb = nsub * L, nsub * Hg
    c = pl.program_id(2)

    @pl.when(c == 0)
    def _():
        for h in range(Hb):
            s_scr[:, h * HEAD_DIM:(h + 1) * HEAD_DIM] = s0_ref[0, h]
        for i in range(3):
            carry_scr[i, 0:1, :Lb] = prkv_ref[0, i:i + 1, :]
        carry_scr[3, 0:1, :LORA_W] = pl_ref[0]
        hm_scr[...] = ((lax.broadcasted_iota(jnp.int32, (R, L), 0) // C)
                       == (lax.broadcasted_iota(jnp.int32, (R, L), 1) // HEAD_DIM)).astype(bf16)
        row = lax.broadcasted_iota(jnp.int32, (R, R), 0)
        col = lax.broadcasted_iota(jnp.int32, (R, R), 1)
        same = (row // C) == (col // C)
        tm_scr[0] = (same & ((col % C) < (row % C))).astype(bf16)
        tm_scr[1] = (same & ((col % C) <= (row % C))).astype(bf16)
        ones_scr[...] = ((lax.broadcasted_iota(jnp.int32, (SUB_L, SUB_L), 0) // HEAD_DIM)
                         == (lax.broadcasted_iota(jnp.int32, (SUB_L, SUB_L), 1) // HEAD_DIM)).astype(bf16)

    def mixed(z_ref, idx, width, mu):
        z = z_ref[0]
        prev = jnp.where(lax.broadcasted_iota(jnp.int32, z.shape, 0) == 0,
                         carry_scr[idx, 0:1, :width], pltpu.roll(z, 1, 0))
        carry_scr[idx, 0:1, :width] = z[C - 1:C, :]
        return z + mu * (prev - z)

    pv = lambda i: pv_ref[i:i + 1, :]
    r = mixed(zr_ref, 0, Lb, pv(PV_MU_R))
    k = mixed(zk_ref, 1, Lb, pv(PV_MU_K))
    v = mixed(zv_ref, 2, Lb, pv(PV_MU_V))
    zl = mixed(zl_ref, 3, LORA_W, mul_ref[...])

    dec = pv(PV_W0) + _dot(jnp.tanh(zl[:, :LORA_PAD]).astype(bf16), wdu_ref[...], _NN)
    sp = jnp.maximum(-dec, 0.0) + jnp.log(1.0 + jnp.exp(-jnp.abs(dec)))
    lw_all = -jnp.exp(-sp - 0.5)
    a = _sigmoid(pv(PV_A0) + _dot(zl[:, LORA_PAD:2 * LORA_PAD].astype(bf16), wau_ref[...], _NN))
    gate = _dot(_sigmoid(zl[:, 2 * LORA_PAD:]).astype(bf16), wgu_ref[...], _NN)

    ones = ones_scr[...]

    def headsum(x):
        outs = []
        for p in range(Lb // SUB_L):
            xp = x[:, p * SUB_L:(p + 1) * SUB_L]
            hi = xp.astype(bf16)
            lo = (xp - hi.astype(f32)).astype(bf16)
            outs.append(_dot(hi, ones, _NN) + _dot(lo, ones, _NN))
        return jnp.concatenate(outs, axis=-1) if len(outs) > 1 else outs[0]

    kkr = k * pv(PV_KK)
    kk_all = kkr * lax.rsqrt(jnp.maximum(headsum(kkr * kkr), 1e-24))
    kf_all = k * (1.0 + (a - 1.0) * pv(PV_KA))
    kka_all = kk_all * a

    tri = (lax.broadcasted_iota(jnp.int32, (C, C), 1) <= lax.broadcasted_iota(jnp.int32, (C, C), 0)).astype(bf16)
    hm = hm_scr[...]
    strict, incl = tm_scr[0], tm_scr[1]

    def stack16(x):
        if C % 16 == 0:
            return jnp.concatenate([x.astype(bf16)] * Hg, axis=0)
        return jnp.concatenate([x] * Hg, axis=0).astype(bf16)

    subs = range(nsub)
    lanes = [slice(j * L, (j + 1) * L) for j in subs]

    def cumsum_t(lw):
        h1 = lw.astype(bf16)
        r1 = lw - h1.astype(f32)
        h2 = r1.astype(bf16)
        h3 = (r1 - h2.astype(f32)).astype(bf16)
        return _dot(tri, h1, _NN) + (_dot(tri, h2, _NN) + _dot(tri, h3, _NN))

    lw = [lw_all[:, ln] for ln in lanes]
    cum = [cumsum_t(lw[j]) for j in subs]
    g = [jnp.exp(cum[j]) for j in subs]
    gi = [jnp.exp(-cum[j]) for j in subs]
    gprev = [jnp.exp(cum[j] - lw[j]) for j in subs]
    g_end = [g[j][C - 1:C, :] for j in subs]
    kk = [kk_all[:, ln] for ln in lanes]
    kka = [kka_all[:, ln] for ln in lanes]
    kf = [kf_all[:, ln] for ln in lanes]
    lhs = [jnp.concatenate([stack16(-kk[j] * gprev[j]) * hm, stack16(r[:, lanes[j]] * g[j]) * hm], axis=0)
           for j in subs]
    rhs = [jnp.concatenate([stack16(kka[j] * gi[j]), stack16(kf[j] * gi[j])], axis=0) for j in subs]
    res = [_dot(lhs[j], rhs[j], _NT).astype(bf16) for j in subs]
    m_bd = [res[j][:R, :R] * strict for j in subs]
    nq = [jnp.concatenate([res[j][:R, R:] * strict, res[j][R:, R:] * incl], axis=0) for j in subs]
    p_bd = [res[j][R:, :R] * incl for j in subs]

    def rows_by_head(x):
        return jnp.concatenate([x[:, h * HEAD_DIM:(h + 1) * HEAD_DIM] for h in range(Hg)], axis=0)

    def lanes_by_head(x):
        return jnp.concatenate([x[h * C:(h + 1) * C, :] for h in range(Hg)], axis=1)

    vs = [rows_by_head(v[:, lanes[j]]).astype(bf16) for j in subs]
    s = [s_scr[:, ln] for ln in lanes]
    base = [_dot(lhs[j], s[j].astype(bf16), _NT) + _dot(nq[j], vs[j], _NN) for j in subs]

    x = [base[j][:R] for j in subs]
    mp = m_bd
    nsq = C.bit_length() - 1
    for i in range(nsq):
        x = [x[j] + _dot(mp[j], x[j].astype(bf16), _NN) for j in subs]
        if i < nsq - 1:
            mp = [_dot(mp[j], mp[j], _NN).astype(bf16) for j in subs]
    u = [x[j].astype(bf16) for j in subs]

    y = [lanes_by_head(base[j][R:] + _dot(p_bd[j], u[j], _NN)) for j in subs]
    for j in subs:
        gl = g_end[j] * gi[j]
        uv = jnp.concatenate([u[j], vs[j]], axis=0)
        bk = jnp.concatenate([stack16(kka[j] * gl) * hm, stack16(kf[j] * gl) * hm], axis=0)
        s_scr[:, lanes[j]] = g_end[j] * s[j] + _dot(uv, bk, _TN)
    y = jnp.concatenate(y, axis=-1) if nsub > 1 else y[0]

    mean = headsum(y) * (1.0 / HEAD_DIM)
    d = y - mean
    var = headsum(d * d) * (1.0 / HEAD_DIM)
    yn = d * lax.rsqrt(var + GN_EPS) * pv(PV_LNG) + pv(PV_LNB)
    bonus = headsum(r * kf_all * pv(PV_RK)) * v
    o_ref[0] = ((yn + bonus) * gate).astype(o_ref.dtype)

    @pl.when(c == nchunks - 1)
    def _():
        for h in range(Hb):
            sout_ref[0, h] = s_scr[:, h * HEAD_DIM:(h + 1) * HEAD_DIM]


def _rwkv(z, shift_prev_p, s0, P, *, chunk, heads, nsub):
    B, T, _ = z.shape
    C, Hg = chunk, heads
    L, R = Hg * HEAD_DIM, Hg * C
    Lb, Hb = nsub * L, nsub * Hg
    G = N_HEADS // Hb
    nchunks = T // C
    kb = RWKV_W // Lb
    prkv = shift_prev_p[:, :3 * RWKV_W].reshape(B, 3, RWKV_W)
    plora = shift_prev_p[:, OFF_WD:].reshape(B, 1, LORA_W)
    zcol = lambda off: pl.BlockSpec((1, C, Lb), lambda b, g, c: (b, c, off + g))
    wspec = lambda rows: pl.BlockSpec((rows, Lb), lambda b, g, c: (0, g))
    st = pl.BlockSpec((1, Hb, HEAD_DIM, HEAD_DIM), lambda b, g, c: (b, g, 0, 0))
    return pl.pallas_call(
        functools.partial(_rwkv_kernel, chunk=C, heads=Hg, nsub=nsub, nchunks=nchunks),
        grid=(B, G, nchunks),
        in_specs=[zcol(0), zcol(kb), zcol(2 * kb),
                  pl.BlockSpec((1, C, LORA_W), lambda b, g, c: (b, c, OFF_WD // LORA_W)),
                  pl.BlockSpec((1, 3, Lb), lambda b, g, c: (b, 0, g)),
                  pl.BlockSpec((1, 1, LORA_W), lambda b, g, c: (b, 0, 0)),
                  wspec(N_PVEC),
                  pl.BlockSpec((1, LORA_W), lambda b, g, c: (0, 0)),
                  wspec(LORA_PAD), wspec(LORA_PAD), wspec(GATE_RANK), st],
        out_specs=[pl.BlockSpec((1, C, Lb), lambda b, g, c: (b, c, g)), st],
        out_shape=[jax.ShapeDtypeStruct((B, T, RWKV_W), bf16),
                   jax.ShapeDtypeStruct((B, N_HEADS, HEAD_DIM, HEAD_DIM), f32)],
        scratch_shapes=[pltpu.VMEM((HEAD_DIM, Lb), f32), pltpu.VMEM((R, L), bf16), pltpu.VMEM((2, R, R), bf16),
                        pltpu.VMEM((SUB_L, SUB_L), bf16), pltpu.VMEM((4, 8, max(Lb, LORA_W)), f32)],
        compiler_params=pltpu.CompilerParams(
            dimension_semantics=("parallel", "parallel", "arbitrary"), vmem_limit_bytes=VMEM_LIMIT),
        name="rwkv",
    )(z, z, z, z, prkv, plora, P["pvec"], P["mu_lora"], P["w_decay_up"], P["w_a_up"], P["w_g_up"], s0)


def _pool_kernel(z0_ref, z1_ref, z2_ref, z3_ref, prev_ref, w_ref, sc_ref, o_ref, carry_scr, *, tt, start_pos):
    i = pl.program_id(1)

    @pl.when(i == 0)
    def _():
        for gi in range(N_POOL):
            carry_scr[gi, 0:1, :] = jnp.zeros((1, POOL_GW), f32)
            carry_scr[gi, 1:POOL_CARRY, :] = prev_ref[0, :, gi * POOL_GW:(gi + 1) * POOL_GW]

    pos1 = start_pos + 1 + i * tt + lax.broadcasted_iota(jnp.int32, (tt, POOL_GW), 0)
    outs = []
    for gi, (z_ref, wdw) in enumerate(zip((z0_ref, z1_ref, z2_ref, z3_ref), POOL_WINDOWS)):
        x = z_ref[0]
        ext = jnp.concatenate([carry_scr[gi], x], axis=0)
        carry_scr[gi] = ext[tt:, :]
        acc, span = ext, 1
        while span < wdw:
            acc = acc + pltpu.roll(acc, span, 0)
            span *= 2
        mean = acc[POOL_CARRY:, :] / jnp.minimum(wdw, pos1).astype(f32)
        o = _dot((mean - x).astype(bf16), w_ref[gi], _NN) * sc_ref[:, gi * POOL_GW:(gi + 1) * POOL_GW]
        outs.append(o.astype(o_ref.dtype))
    o_ref[0] = jnp.concatenate(outs, axis=-1)


def _pool(z, pool_prev, P, start_pos, tt):
    B, T, _ = z.shape
    tt = min(tt, T)
    zcol = lambda gi: pl.BlockSpec((1, tt, POOL_GW), lambda b, i: (b, i, SHIFT_WP // POOL_GW + gi))
    return pl.pallas_call(
        functools.partial(_pool_kernel, tt=tt, start_pos=start_pos),
        grid=(B, T // tt),
        in_specs=[zcol(0), zcol(1), zcol(2), zcol(3),
                  pl.BlockSpec((1, POOL_BUF, POOL_W), lambda b, i: (b, 0, 0)),
                  pl.BlockSpec((N_POOL, POOL_GW, POOL_GW), lambda b, i: (0, 0, 0)),
                  pl.BlockSpec((1, POOL_W), lambda b, i: (0, 0))],
        out_specs=pl.BlockSpec((1, tt, POOL_W), lambda b, i: (b, i, 0)),
        out_shape=jax.ShapeDtypeStruct((B, T, POOL_W), bf16),
        scratch_shapes=[pltpu.VMEM((N_POOL, POOL_CARRY, POOL_GW), f32)],
        compiler_params=pltpu.CompilerParams(
            dimension_semantics=("parallel", "arbitrary"), vmem_limit_bytes=VMEM_LIMIT),
        name="pool",
    )(z, z, z, z, pool_prev, P["w_pool"], P["pool_scale"])


def _xattn_kernel(q_ref, k_ref, v_ref, o_ref):
    q = q_ref[0]
    kx = k_ref[0].astype(bf16)
    vx = v_ref[0].astype(bf16)
    outs = []
    for h in range(XA_HEADS):
        sl = slice(h * XA_HEAD_DIM, (h + 1) * XA_HEAD_DIM)
        s = _dot(q[:, sl], kx[:, sl], _NT) * (XA_HEAD_DIM ** -0.5)
        s = s - jnp.max(s, axis=-1, keepdims=True)
        e = jnp.exp(s)
        p = e / jnp.sum(e, axis=-1, keepdims=True)
        outs.append(_dot(p.astype(bf16), vx[:, sl], _NN))
    o_ref[0] = jnp.concatenate(outs, axis=-1).astype(o_ref.dtype)


def _xattn(q, mk, mv, tq):
    B, T, _ = q.shape
    tq = min(tq, T)
    return pl.pallas_call(
        _xattn_kernel,
        grid=(B, T // tq),
        in_specs=[pl.BlockSpec((1, tq, XA_W), lambda b, i: (b, i, 0)),
                  pl.BlockSpec((1, MEM_TOKENS, XA_W), lambda b, i: (b, 0, 0)),
                  pl.BlockSpec((1, MEM_TOKENS, XA_W), lambda b, i: (b, 0, 0))],
        out_specs=pl.BlockSpec((1, tq, XA_W), lambda b, i: (b, i, 0)),
        out_shape=jax.ShapeDtypeStruct((B, T, XA_W), bf16),
        compiler_params=pltpu.CompilerParams(
            dimension_semantics=("parallel", "parallel"), vmem_limit_bytes=VMEM_LIMIT),
        name="xattn",
    )(q, mk, mv)


def _pad_shift(x):
    z = jnp.zeros(x.shape[:-1] + (LORA_PAD - DECAY_RANK,), x.dtype)
    return jnp.concatenate([
        x[..., :OFF_WD], x[..., OFF_WD:OFF_WD + DECAY_RANK], z,
        x[..., OFF_WD + DECAY_RANK:OFF_WD + DECAY_RANK + AAA_RANK], z,
        x[..., OFF_WD + DECAY_RANK + AAA_RANK:]], axis=-1)


def _unpad_shift(x):
    return jnp.concatenate([
        x[..., :OFF_WD + DECAY_RANK], x[..., OFF_AD:OFF_AD + AAA_RANK], x[..., OFF_GD:]], axis=-1)


def _block(x, mk, mv, shift_prev, pool_prev, wkv_prev, start_pos, P, chunk, heads, nsub, pool_tt):
    B, T, _ = x.shape
    M = B * T
    x2 = x.reshape(M, D_MODEL)

    h = _rmsnorm(x2, P["norm_mix_g"], bf16)
    z = _matmul(h, P["w_in"], tn=512, tk=D_MODEL).reshape(B, T, IN_WP)
    o_rwkv, s_new = _rwkv(z, _pad_shift(shift_prev), wkv_prev, P, chunk=chunk, heads=heads, nsub=nsub)
    o_pool = _pool(z, pool_prev, P, start_pos, pool_tt)
    x2 = _matmul(o_rwkv.reshape(M, RWKV_W), P["w_out"], a2=o_pool.reshape(M, POOL_W), res=x2, tk=RWKV_W)
    new_shift = _unpad_shift(z[:, -1, :SHIFT_WP])
    new_pool = jnp.concatenate([pool_prev, z[:, :, SHIFT_WP:]], axis=1)[:, -POOL_BUF:]

    h = _rmsnorm(x2, P["norm_xa_g"], bf16)
    q = _matmul(h, P["w_xq"], out_dtype=bf16, tn=512, tk=D_MODEL).reshape(B, T, XA_W)
    o = _xattn(q, mk.reshape(B, MEM_TOKENS, XA_W), mv.reshape(B, MEM_TOKENS, XA_W), tq=512)
    x2 = _matmul(o.reshape(M, XA_W), P["w_xo"], res=x2, tk=512)

    h = _rmsnorm(x2, P["norm_ffn_g"], bf16)
    hid = _matmul(h, P["w_up"], act="relu2", out_dtype=bf16, tk=D_MODEL)
    x2 = _matmul(hid, P["w_down"], res=x2, tk=2048)
    y_out = _rmsnorm(x2, P["norm_final_g"], f32).reshape(B, T, D_MODEL)

    return y_out, new_shift, new_pool, s_new


def _params(norm_mix_g, w_in, mu_shift, w0, w_decay_up, a0, w_a_up, w_g_up, k_k, k_a, r_k, lnx_g, lnx_b, w_pool,
            pool_scale, w_out, norm_xa_g, norm_mem_g, w_xq, w_mk, w_mv, w_xo, norm_ffn_g, w_up, w_down,
            norm_final_g):
    l = 0
    zcol = jnp.zeros((D_MODEL, LORA_PAD - DECAY_RANK), f32)
    wi = w_in[l]
    w_in_p = jnp.concatenate([
        wi[:, :OFF_WD + DECAY_RANK], zcol,
        wi[:, OFF_WD + DECAY_RANK:OFF_WD + DECAY_RANK + AAA_RANK], zcol,
        wi[:, OFF_WD + DECAY_RANK + AAA_RANK:]], axis=1).astype(bf16)
    mu_p = _pad_shift(mu_shift[l])
    pvec = jnp.stack([mu_p[:RWKV_W], mu_p[RWKV_W:2 * RWKV_W], mu_p[2 * RWKV_W:3 * RWKV_W], w0[l], a0[l], k_k[l],
                      k_a[l], r_k[l].reshape(RWKV_W), lnx_g[l], lnx_b[l]])
    pvec = jnp.concatenate([pvec, jnp.zeros((N_PVEC - pvec.shape[0], RWKV_W), f32)], axis=0)
    pad_rows = lambda w: jnp.concatenate([w, jnp.zeros((LORA_PAD - w.shape[0], RWKV_W), f32)], axis=0).astype(bf16)
    P = {
        "norm_mix_g": norm_mix_g[l], "w_in": w_in_p, "pvec": pvec, "mu_lora": mu_p[OFF_WD:].reshape(1, LORA_W),
        "w_decay_up": pad_rows(w_decay_up[l]), "w_a_up": pad_rows(w_a_up[l]), "w_g_up": w_g_up[l].astype(bf16),
        "w_pool": w_pool[l].astype(bf16), "pool_scale": pool_scale[l].reshape(1, POOL_W),
        "w_out": w_out[l].astype(bf16),
        "norm_xa_g": norm_xa_g[l], "w_xq": w_xq[l].astype(bf16), "w_xo": w_xo[l].astype(bf16),
        "norm_ffn_g": norm_ffn_g[l], "w_up": w_up[l].astype(bf16), "w_down": w_down[l].astype(bf16),
        "norm_final_g": norm_final_g, "norm_mem_g": norm_mem_g[l],
        "w_mkv": jnp.concatenate([w_mk[l], w_mv[l]], axis=1).astype(bf16),
    }
    return P


def kernel(x_prompt, x_sample, mem_prompt, state_wkv, state_shift, state_pool, cache_mem_k, cache_mem_v, norm_mix_g, w_in, mu_shift, w0, w_decay_up, a0, w_a_up, w_g_up, k_k, k_a, r_k, lnx_g, lnx_b, w_pool, pool_scale, w_out, norm_xa_g, norm_mem_g, w_xq, w_mk, w_mv, w_xo, norm_ffn_g, w_up, w_down, norm_final_g):
    l = 0
    Bp = x_prompt.shape[0]
    P = _params(norm_mix_g, w_in, mu_shift, w0, w_decay_up, a0, w_a_up, w_g_up, k_k, k_a, r_k, lnx_g, lnx_b, w_pool,
                pool_scale, w_out, norm_xa_g, norm_mem_g, w_xq, w_mk, w_mv, w_xo, norm_ffn_g, w_up, w_down,
                norm_final_g)
    mem = _rmsnorm(mem_prompt.reshape(Bp * MEM_TOKENS, D_MODEL), P["norm_mem_g"], bf16)
    mkv = _matmul(mem, P["w_mkv"], tn=512, tk=D_MODEL)
    mk_p = mkv[:, :XA_W].reshape(Bp, MEM_TOKENS, XA_HEADS, XA_HEAD_DIM)
    mv_p = mkv[:, XA_W:].reshape(Bp, MEM_TOKENS, XA_HEADS, XA_HEAD_DIM)

    yp, sh_p, pl_p, wkv_p = _block(
        x_prompt, mk_p, mv_p, jnp.zeros((Bp, SHIFT_W), f32), jnp.zeros((Bp, POOL_BUF, POOL_W), f32),
        jnp.zeros((Bp, N_HEADS, HEAD_DIM, HEAD_DIM), f32), 0, P, chunk=64, heads=SUB_HEADS, nsub=4, pool_tt=256)
    ys, sh_s, pl_s, wkv_s = _block(
        x_sample, cache_mem_k[l], cache_mem_v[l], state_shift[l], state_pool[l], state_wkv[l],
        PAST_LEN, P, chunk=8, heads=N_HEADS, nsub=1, pool_tt=8)
    ex = lambda t: t[None]
    return (yp, ys, ex(wkv_p), ex(sh_p), ex(pl_p), ex(mk_p), ex(mv_p), ex(wkv_s), ex(sh_s), ex(pl_s))
```

```python
import functools

import jax
import jax.numpy as jnp
from jax import lax
from jax.experimental import pallas as pl
from jax.experimental.pallas import tpu as pltpu

D_MODEL = 4096
RWKV_W = 2048
POOL_W = 2048
HEAD_DIM = 64
N_HEADS = RWKV_W // HEAD_DIM
DECAY_RANK = 96
AAA_RANK = 96
GATE_RANK = 256
SHIFT_W = 3 * RWKV_W + DECAY_RANK + AAA_RANK + GATE_RANK
POOL_WINDOWS = (2, 4, 8, 16)
N_POOL = len(POOL_WINDOWS)
POOL_GW = POOL_W // N_POOL
POOL_BUF = max(POOL_WINDOWS) - 1
POOL_CARRY = POOL_BUF + 1
MEM_TOKENS = 256
XA_HEADS = 4
XA_HEAD_DIM = 128
XA_W = XA_HEADS * XA_HEAD_DIM
PAST_LEN = 16384
RMS_EPS = 1e-6
GN_EPS = 64e-5

RKV_W = 3 * RWKV_W
LORA_RANKS = DECAY_RANK + AAA_RANK + GATE_RANK
LORA_W = 512
SUB_HEADS = 4
SUB_L = SUB_HEADS * HEAD_DIM
N_PVEC = 16

VMEM_LIMIT = 56 * 1024 * 1024

_NN = ((1,), (0,))
_NT = ((1,), (1,))
_TN = ((0,), (0,))

f32 = jnp.float32
bf16 = jnp.bfloat16


def _dot(a, b, dims):
    return lax.dot_general(a, b, (dims, ((), ())), preferred_element_type=f32)


def _sigmoid(x):
    return 1.0 / (1.0 + jnp.exp(-x))


def _rmsnorm_kernel(x_ref, g_ref, o_ref):
    x = x_ref[...]
    y = x * lax.rsqrt(jnp.mean(x * x, axis=-1, keepdims=True) + RMS_EPS)
    o_ref[...] = (y * g_ref[...]).astype(o_ref.dtype)


def _rmsnorm(x, g, out_dtype, tm=256):
    m, d = x.shape
    return pl.pallas_call(
        _rmsnorm_kernel,
        grid=(m // tm,),
        in_specs=[pl.BlockSpec((tm, d), lambda i: (i, 0)), pl.BlockSpec((1, d), lambda i: (0, 0))],
        out_specs=pl.BlockSpec((tm, d), lambda i: (i, 0)),
        out_shape=jax.ShapeDtypeStruct((m, d), out_dtype),
        compiler_params=pltpu.CompilerParams(dimension_semantics=("parallel",), vmem_limit_bytes=VMEM_LIMIT),
        name="rmsnorm",
    )(x, g.reshape(1, d))


def _matmul_kernel(*refs, nk, nk1, act, has_res, two_lhs):
    refs = list(refs)
    a_ref = refs.pop(0)
    a2_ref = refs.pop(0) if two_lhs else None
    b_ref = refs.pop(0)
    res_ref = refs.pop(0) if has_res else None

    def finish(r):
        if act == "relu2":
            r = jnp.square(jnp.maximum(r, 0.0))
        if has_res:
            r = r + res_ref[...]
        return r

    def prod(lhs_ref):
        return jnp.dot(lhs_ref[...], b_ref[...].astype(bf16), preferred_element_type=f32)

    if nk == 1:
        (o_ref,) = refs
        o_ref[...] = finish(prod(a_ref)).astype(o_ref.dtype)
        return

    o_ref, acc_ref = refs
    k = pl.program_id(2)

    @pl.when(k == 0)
    def _():
        acc_ref[...] = jnp.zeros_like(acc_ref)

    if two_lhs:
        @pl.when(k < nk1)
        def _():
            acc_ref[...] += prod(a_ref)

        @pl.when(k >= nk1)
        def _():
            acc_ref[...] += prod(a2_ref)
    else:
        acc_ref[...] += prod(a_ref)

    @pl.when(k == nk - 1)
    def _():
        o_ref[...] = finish(acc_ref[...]).astype(o_ref.dtype)


def _matmul(a, b, res=None, act=None, out_dtype=f32, a2=None, col0=0, n=None, tm=1024, tn=1024, tk=1024):
    m, k1 = a.shape
    kd = b.shape[0]
    n = b.shape[1] - col0 if n is None else n
    tm, tn, tk = min(tm, m), min(tn, n), min(tk, k1)
    assert m % tm == 0 and n % tn == 0 and kd % tk == 0 and k1 % tk == 0 and col0 % tn == 0, (
        a.shape, b.shape, col0, n, tm, tn, tk)
    nk, nk1, j0 = kd // tk, k1 // tk, col0 // tn
    two_lhs = a2 is not None
    if two_lhs:
        assert a2.shape == (m, kd - k1)
        in_specs = [pl.BlockSpec((tm, tk), lambda i, j, k: (i, jnp.minimum(k, nk1 - 1))),
                    pl.BlockSpec((tm, tk), lambda i, j, k: (i, jnp.maximum(k - nk1, 0)))]
        args = [a, a2]
    else:
        assert k1 == kd
        in_specs = [pl.BlockSpec((tm, tk), lambda i, j, k: (i, k))]
        args = [a]
    in_specs.append(pl.BlockSpec((tk, tn), lambda i, j, k: (k, j0 + j)))
    args.append(b)
    if res is not None:
        in_specs.append(pl.BlockSpec((tm, tn), lambda i, j, k: (i, j)))
        args.append(res)
    return pl.pallas_call(
        functools.partial(_matmul_kernel, nk=nk, nk1=nk1, act=act, has_res=res is not None, two_lhs=two_lhs),
        grid=(m // tm, n // tn, nk),
        in_specs=in_specs,
        out_specs=pl.BlockSpec((tm, tn), lambda i, j, k: (i, j)),
        out_shape=jax.ShapeDtypeStruct((m, n), out_dtype),
        scratch_shapes=[pltpu.VMEM((tm, tn), f32)] if nk > 1 else [],
        compiler_params=pltpu.CompilerParams(
            dimension_semantics=("parallel", "parallel", "arbitrary"), vmem_limit_bytes=VMEM_LIMIT),
        name="matmul",
    )(*args)


PV_MU_R, PV_MU_K, PV_MU_V, PV_W0, PV_A0, PV_KK, PV_KA, PV_RK, PV_LNG, PV_LNB = range(10)


def _rwkv_kernel(zr_ref, zk_ref, zv_ref, zl_ref, prkv_ref, pl_ref, pv_ref, mul_ref, wdu_ref, wau_ref, wgu_ref,
                 s0_ref, o_ref, sout_ref, s_scr, hm_scr, tm_scr, ones_scr, carry_scr,
                 *, chunk, heads, nsub, nchunks):
    C, Hg = chunk, heads
    L, R = Hg * HEAD_DIM, Hg * C
    Lb, Hb = nsub * L, nsub * Hg
    c = pl.program_id(2)

    @pl.when(c == 0)
    def _():
        for h in range(Hb):
            s_scr[:, h * HEAD_DIM:(h + 1) * HEAD_DIM] = s0_ref[0, h]
        for i in range(3):
            carry_scr[i, 0:1, :Lb] = prkv_ref[0, i:i + 1, :]
        carry_scr[3, 0:1, :LORA_W] = pl_ref[0]
        hm_scr[...] = ((lax.broadcasted_iota(jnp.int32, (R, L), 0) // C)
                       == (lax.broadcasted_iota(jnp.int32, (R, L), 1) // HEAD_DIM)).astype(bf16)
        row = lax.broadcasted_iota(jnp.int32, (R, R), 0)
        col = lax.broadcasted_iota(jnp.int32, (R, R), 1)
        same = (row // C) == (col // C)
        tm_scr[0] = (same & ((col % C) < (row % C))).astype(bf16)
        tm_scr[1] = (same & ((col % C) <= (row % C))).astype(bf16)
        ones_scr[...] = ((lax.broadcasted_iota(jnp.int32, (SUB_L, SUB_L), 0) // HEAD_DIM)
                         == (lax.broadcasted_iota(jnp.int32, (SUB_L, SUB_L), 1) // HEAD_DIM)).astype(bf16)

    def mixed(z_ref, idx, width, mu):
        z = z_ref[0]
        prev = jnp.where(lax.broadcasted_iota(jnp.int32, z.shape, 0) == 0,
                         carry_scr[idx, 0:1, :width], pltpu.roll(z, 1, 0))
        carry_scr[idx, 0:1, :width] = z[C - 1:C, :]
        return z + mu * (prev - z)

    pv = lambda i: pv_ref[i:i + 1, :]
    r = mixed(zr_ref, 0, Lb, pv(PV_MU_R))
    k = mixed(zk_ref, 1, Lb, pv(PV_MU_K))
    v = mixed(zv_ref, 2, Lb, pv(PV_MU_V))
    zl = mixed(zl_ref, 3, LORA_W, mul_ref[...])

    dec = pv(PV_W0) + _dot(jnp.tanh(zl).astype(bf16), wdu_ref[...], _NN)
    sp = jnp.maximum(-dec, 0.0) + jnp.log(1.0 + jnp.exp(-jnp.abs(dec)))
    lw_all = -jnp.exp(-sp - 0.5)
    a = _sigmoid(pv(PV_A0) + _dot(zl.astype(bf16), wau_ref[...], _NN))
    gate = _dot(_sigmoid(zl).astype(bf16), wgu_ref[...], _NN)

    ones = ones_scr[...]

    def headsum(x):
        npc = Lb // SUB_L
        xs = jnp.concatenate([x[:, p * SUB_L:(p + 1) * SUB_L] for p in range(npc)], axis=0)
        sums = _dot(xs.astype(bf16), ones, _NN)
        return jnp.concatenate([sums[p * C:(p + 1) * C, :] for p in range(npc)], axis=1)

    kkr = k * pv(PV_KK)
    kk_all = kkr * lax.rsqrt(jnp.maximum(headsum(kkr * kkr), 1e-24))
    kf_all = k * (1.0 + (a - 1.0) * pv(PV_KA))
    kka_all = kk_all * a

    tri = (lax.broadcasted_iota(jnp.int32, (C, C), 1) <= lax.broadcasted_iota(jnp.int32, (C, C), 0)).astype(bf16)
    hm = hm_scr[...]
    strict, incl = tm_scr[0], tm_scr[1]

    def stack16(x):
        if C % 16 == 0:
            return jnp.concatenate([x.astype(bf16)] * Hg, axis=0)
        return jnp.concatenate([x] * Hg, axis=0).astype(bf16)

    subs = range(nsub)
    lanes = [slice(j * L, (j + 1) * L) for j in subs]

    def cumsum_t(lw):
        h1 = lw.astype(bf16)
        r1 = lw - h1.astype(f32)
        h2 = r1.astype(bf16)
        h3 = (r1 - h2.astype(f32)).astype(bf16)
        return _dot(tri, h1, _NN) + (_dot(tri, h2, _NN) + _dot(tri, h3, _NN))

    lw = [lw_all[:, ln] for ln in lanes]
    cum = [cumsum_t(lw[j]) for j in subs]
    g = [jnp.exp(cum[j]) for j in subs]
    gi = [jnp.exp(-cum[j]) for j in subs]
    gprev = [jnp.exp(cum[j] - lw[j]) for j in subs]
    g_end = [g[j][C - 1:C, :] for j in subs]
    kk = [kk_all[:, ln] for ln in lanes]
    kka = [kka_all[:, ln] for ln in lanes]
    kf = [kf_all[:, ln] for ln in lanes]
    lhs = [jnp.concatenate([stack16(-kk[j] * gprev[j]) * hm, stack16(r[:, lanes[j]] * g[j]) * hm], axis=0)
           for j in subs]
    rhs = [jnp.concatenate([stack16(kka[j] * gi[j]), stack16(kf[j] * gi[j])], axis=0) for j in subs]
    res = [_dot(lhs[j], rhs[j], _NT).astype(bf16) for j in subs]
    m_bd = [res[j][:R, :R] * strict for j in subs]
    nq = [jnp.concatenate([res[j][:R, R:] * strict, res[j][R:, R:] * incl], axis=0) for j in subs]
    p_bd = [res[j][R:, :R] * incl for j in subs]

    def rows_by_head(x):
        return jnp.concatenate([x[:, h * HEAD_DIM:(h + 1) * HEAD_DIM] for h in range(Hg)], axis=0)

    def lanes_by_head(x):
        return jnp.concatenate([x[h * C:(h + 1) * C, :] for h in range(Hg)], axis=1)

    vs = [rows_by_head(v[:, lanes[j]]).astype(bf16) for j in subs]
    s = [s_scr[:, ln] for ln in lanes]
    base = [_dot(lhs[j], s[j].astype(bf16), _NT) + _dot(nq[j], vs[j], _NN) for j in subs]

    x = [base[j][:R] for j in subs]
    mp = m_bd
    nsq = C.bit_length() - 1
    for i in range(nsq):
        x = [x[j] + _dot(mp[j], x[j].astype(bf16), _NN) for j in subs]
        if i < nsq - 1:
            mp = [_dot(mp[j], mp[j], _NN).astype(bf16) for j in subs]
    u = [x[j].astype(bf16) for j in subs]

    y = [lanes_by_head(base[j][R:] + _dot(p_bd[j], u[j], _NN)) for j in subs]
    for j in subs:
        gl = g_end[j] * gi[j]
        uv = jnp.concatenate([u[j], vs[j]], axis=0)
        bk = jnp.concatenate([stack16(kka[j] * gl) * hm, stack16(kf[j] * gl) * hm], axis=0)
        s_scr[:, lanes[j]] = g_end[j] * s[j] + _dot(uv, bk, _TN)
    y = jnp.concatenate(y, axis=-1) if nsub > 1 else y[0]

    mean = headsum(y) * (1.0 / HEAD_DIM)
    d = y - mean
    var = headsum(d * d) * (1.0 / HEAD_DIM)
    yn = d * lax.rsqrt(var + GN_EPS) * pv(PV_LNG) + pv(PV_LNB)
    bonus = headsum(r * kf_all * pv(PV_RK)) * v
    o_ref[0] = ((yn + bonus) * gate).astype(o_ref.dtype)

    @pl.when(c == nchunks - 1)
    def _():
        for h in range(Hb):
            sout_ref[0, h] = s_scr[:, h * HEAD_DIM:(h + 1) * HEAD_DIM]


def _rwkv(z_rkv, z_lora, shift_prev, s0, P, *, chunk, heads, nsub):
    B, T, _ = z_rkv.shape
    C, Hg = chunk, heads
    L, R = Hg * HEAD_DIM, Hg * C
    Lb, Hb = nsub * L, nsub * Hg
    G = N_HEADS // Hb
    nchunks = T // C
    kb = RWKV_W // Lb
    prkv = shift_prev[:, :RKV_W].reshape(B, 3, RWKV_W)
    plora = jnp.pad(shift_prev[:, RKV_W:], ((0, 0), (0, LORA_W - LORA_RANKS))).reshape(B, 1, LORA_W)
    zcol = lambda off: pl.BlockSpec((1, C, Lb), lambda b, g, c: (b, c, off + g))
    wspec = lambda rows: pl.BlockSpec((rows, Lb), lambda b, g, c: (0, g))
    st = pl.BlockSpec((1, Hb, HEAD_DIM, HEAD_DIM), lambda b, g, c: (b, g, 0, 0))
    return pl.pallas_call(
        functools.partial(_rwkv_kernel, chunk=C, heads=Hg, nsub=nsub, nchunks=nchunks),
        grid=(B, G, nchunks),
        in_specs=[zcol(0), zcol(kb), zcol(2 * kb),
                  pl.BlockSpec((1, C, LORA_W), lambda b, g, c: (b, c, 0)),
                  pl.BlockSpec((1, 3, Lb), lambda b, g, c: (b, 0, g)),
                  pl.BlockSpec((1, 1, LORA_W), lambda b, g, c: (b, 0, 0)),
                  wspec(N_PVEC),
                  pl.BlockSpec((1, LORA_W), lambda b, g, c: (0, 0)),
                  wspec(LORA_W), wspec(LORA_W), wspec(LORA_W), st],
        out_specs=[pl.BlockSpec((1, C, Lb), lambda b, g, c: (b, c, g)), st],
        out_shape=[jax.ShapeDtypeStruct((B, T, RWKV_W), bf16),
                   jax.ShapeDtypeStruct((B, N_HEADS, HEAD_DIM, HEAD_DIM), f32)],
        scratch_shapes=[pltpu.VMEM((HEAD_DIM, Lb), f32), pltpu.VMEM((R, L), bf16), pltpu.VMEM((2, R, R), bf16),
                        pltpu.VMEM((SUB_L, SUB_L), bf16), pltpu.VMEM((4, 8, max(Lb, LORA_W)), f32)],
        compiler_params=pltpu.CompilerParams(
            dimension_semantics=("parallel", "parallel", "arbitrary"), vmem_limit_bytes=VMEM_LIMIT),
        name="rwkv",
    )(z_rkv, z_rkv, z_rkv, z_lora, prkv, plora, P["pvec"], P["mu_lora"], P["w_decay_up"], P["w_a_up"],
      P["w_g_up"], s0)


def _pool_kernel(z0_ref, z1_ref, z2_ref, z3_ref, prev_ref, w_ref, sc_ref, o_ref, carry_scr, *, tt, start_pos):
    i = pl.program_id(1)

    @pl.when(i == 0)
    def _():
        for gi in range(N_POOL):
            carry_scr[gi, 0:1, :] = jnp.zeros((1, POOL_GW), f32)
            carry_scr[gi, 1:POOL_CARRY, :] = prev_ref[0, :, gi * POOL_GW:(gi + 1) * POOL_GW]

    pos1 = start_pos + 1 + i * tt + lax.broadcasted_iota(jnp.int32, (tt, POOL_GW), 0)
    outs = []
    for gi, (z_ref, wdw) in enumerate(zip((z0_ref, z1_ref, z2_ref, z3_ref), POOL_WINDOWS)):
        x = z_ref[0]
        ext = jnp.concatenate([carry_scr[gi], x], axis=0)
        carry_scr[gi] = ext[tt:, :]
        acc, span = ext, 1
        while span < wdw:
            acc = acc + pltpu.roll(acc, span, 0)
            span *= 2
        mean = acc[POOL_CARRY:, :] / jnp.minimum(wdw, pos1).astype(f32)
        o = _dot((mean - x).astype(bf16), w_ref[gi].astype(bf16), _NN) * sc_ref[:, gi * POOL_GW:(gi + 1) * POOL_GW]
        outs.append(o.astype(o_ref.dtype))
    o_ref[0] = jnp.concatenate(outs, axis=-1)


def _pool(z, pool_prev, P, start_pos, tt):
    B, T, _ = z.shape
    tt = min(tt, T)
    zcol = lambda gi: pl.BlockSpec((1, tt, POOL_GW), lambda b, i: (b, i, gi))
    return pl.pallas_call(
        functools.partial(_pool_kernel, tt=tt, start_pos=start_pos),
        grid=(B, T // tt),
        in_specs=[zcol(0), zcol(1), zcol(2), zcol(3),
                  pl.BlockSpec((1, POOL_BUF, POOL_W), lambda b, i: (b, 0, 0)),
                  pl.BlockSpec((N_POOL, POOL_GW, POOL_GW), lambda b, i: (0, 0, 0)),
                  pl.BlockSpec((1, POOL_W), lambda b, i: (0, 0))],
        out_specs=pl.BlockSpec((1, tt, POOL_W), lambda b, i: (b, i, 0)),
        out_shape=jax.ShapeDtypeStruct((B, T, POOL_W), bf16),
        scratch_shapes=[pltpu.VMEM((N_POOL, POOL_CARRY, POOL_GW), f32)],
        compiler_params=pltpu.CompilerParams(
            dimension_semantics=("parallel", "arbitrary"), vmem_limit_bytes=VMEM_LIMIT),
        name="pool",
    )(z, z, z, z, pool_prev, P["w_pool"], P["pool_scale"])


def _xattn_kernel(q_ref, k_ref, v_ref, o_ref, *, head_axis):
    q = q_ref[0]
    outs = []
    for h in range(XA_HEADS):
        sl = slice(h * XA_HEAD_DIM, (h + 1) * XA_HEAD_DIM)
        if head_axis:
            kh, vh = k_ref[0, :, h, :], v_ref[0, :, h, :]
        else:
            kh, vh = k_ref[0, :, sl], v_ref[0, :, sl]
        s = _dot(q[:, sl], kh.astype(bf16), _NT) * (XA_HEAD_DIM ** -0.5)
        s = s - jnp.max(s, axis=-1, keepdims=True)
        e = jnp.exp(s)
        p = e / jnp.sum(e, axis=-1, keepdims=True)
        outs.append(_dot(p.astype(bf16), vh.astype(bf16), _NN))
    o_ref[0] = jnp.concatenate(outs, axis=-1).astype(o_ref.dtype)


def _xattn(q, mk, mv, tq):
    B, T, _ = q.shape
    tq = min(tq, T)
    head_axis = mk.ndim == 4
    if head_axis:
        kv_spec = pl.BlockSpec((1, MEM_TOKENS, XA_HEADS, XA_HEAD_DIM), lambda b, i: (b, 0, 0, 0))
    else:
        kv_spec = pl.BlockSpec((1, MEM_TOKENS, XA_W), lambda b, i: (b, 0, 0))
    return pl.pallas_call(
        functools.partial(_xattn_kernel, head_axis=head_axis),
        grid=(B, T // tq),
        in_specs=[pl.BlockSpec((1, tq, XA_W), lambda b, i: (b, i, 0)), kv_spec, kv_spec],
        out_specs=pl.BlockSpec((1, tq, XA_W), lambda b, i: (b, i, 0)),
        out_shape=jax.ShapeDtypeStruct((B, T, XA_W), bf16),
        compiler_params=pltpu.CompilerParams(
            dimension_semantics=("parallel", "parallel"), vmem_limit_bytes=VMEM_LIMIT),
        name="xattn",
    )(q, mk, mv)


def _block(x, mk, mv, shift_prev, pool_prev, wkv_prev, start_pos, P, chunk, heads, nsub, pool_tt):
    B, T, _ = x.shape
    M = B * T
    x2 = x.reshape(M, D_MODEL)

    h = _rmsnorm(x2, P["norm_mix_g"], bf16)
    z_rkv = _matmul(h, P["w_in"], n=RKV_W, tn=512, tk=D_MODEL).reshape(B, T, RKV_W)
    z_lora = _matmul(h, P["w_in"], col0=RKV_W, n=LORA_W, tn=512, tk=D_MODEL).reshape(B, T, LORA_W)
    z_pool = _matmul(h, P["w_in_pool"], tn=512, tk=D_MODEL).reshape(B, T, POOL_W)
    o_rwkv, s_new = _rwkv(z_rkv, z_lora, shift_prev, wkv_prev, P, chunk=chunk, heads=heads, nsub=nsub)
    o_pool = _pool(z_pool, pool_prev, P, start_pos, pool_tt)
    x2 = _matmul(o_rwkv.reshape(M, RWKV_W), P["w_out"], a2=o_pool.reshape(M, POOL_W), res=x2, tn=512, tk=RWKV_W)
    new_shift = jnp.concatenate([z_rkv[:, -1], z_lora[:, -1, :LORA_RANKS]], axis=-1)
    new_pool = jnp.concatenate([pool_prev, z_pool], axis=1)[:, -POOL_BUF:]

    h = _rmsnorm(x2, P["norm_xa_g"], bf16)
    q = _matmul(h, P["w_xq"], out_dtype=bf16, tn=512, tk=D_MODEL).reshape(B, T, XA_W)
    o = _xattn(q, mk, mv, tq=512)
    x2 = _matmul(o.reshape(M, XA_W), P["w_xo"], res=x2, tk=512)

    h = _rmsnorm(x2, P["norm_ffn_g"], bf16)
    hid = _matmul(h, P["w_up"], act="relu2", out_dtype=bf16, tn=512, tk=D_MODEL)
    x2 = _matmul(hid, P["w_down"], res=x2, tk=2048)
    y_out = _rmsnorm(x2, P["norm_final_g"], f32).reshape(B, T, D_MODEL)

    return y_out, new_shift, new_pool, s_new


def _params(norm_mix_g, w_in, mu_shift, w0, w_decay_up, a0, w_a_up, w_g_up, k_k, k_a, r_k, lnx_g, lnx_b, w_pool,
            pool_scale, w_out, norm_xa_g, norm_mem_g, w_xq, w_mk, w_mv, w_xo, norm_ffn_g, w_up, w_down,
            norm_final_g):
    l = 0
    mu = mu_shift[l]
    pvec = jnp.stack([mu[:RWKV_W], mu[RWKV_W:2 * RWKV_W], mu[2 * RWKV_W:RKV_W], w0[l], a0[l], k_k[l],
                      k_a[l], r_k[l].reshape(RWKV_W), lnx_g[l], lnx_b[l]])
    pvec = jnp.concatenate([pvec, jnp.zeros((N_PVEC - pvec.shape[0], RWKV_W), f32)], axis=0)

    def lora_rows(w, first):
        return jnp.pad(w, ((first, LORA_W - first - w.shape[0]), (0, 0))).astype(bf16)

    return {
        "norm_mix_g": norm_mix_g[l], "w_in": w_in[l], "w_in_pool": w_in[l][:, SHIFT_W:].astype(bf16),
        "pvec": pvec, "mu_lora": jnp.pad(mu[RKV_W:], (0, LORA_W - LORA_RANKS)).reshape(1, LORA_W),
        "w_decay_up": lora_rows(w_decay_up[l], 0), "w_a_up": lora_rows(w_a_up[l], DECAY_RANK),
        "w_g_up": lora_rows(w_g_up[l], DECAY_RANK + AAA_RANK),
        "w_pool": w_pool[l], "pool_scale": pool_scale[l].reshape(1, POOL_W), "w_out": w_out[l],
        "norm_xa_g": norm_xa_g[l], "w_xq": w_xq[l], "w_xo": w_xo[l],
        "norm_ffn_g": norm_ffn_g[l], "w_up": w_up[l], "w_down": w_down[l],
        "norm_final_g": norm_final_g, "norm_mem_g": norm_mem_g[l], "w_mk": w_mk[l], "w_mv": w_mv[l],
    }


def kernel(x_prompt, x_sample, mem_prompt, state_wkv, state_shift, state_pool, cache_mem_k, cache_mem_v, norm_mix_g, w_in, mu_shift, w0, w_decay_up, a0, w_a_up, w_g_up, k_k, k_a, r_k, lnx_g, lnx_b, w_pool, pool_scale, w_out, norm_xa_g, norm_mem_g, w_xq, w_mk, w_mv, w_xo, norm_ffn_g, w_up, w_down, norm_final_g):
    Bp = x_prompt.shape[0]
    P = _params(norm_mix_g, w_in, mu_shift, w0, w_decay_up, a0, w_a_up, w_g_up, k_k, k_a, r_k, lnx_g, lnx_b, w_pool,
                pool_scale, w_out, norm_xa_g, norm_mem_g, w_xq, w_mk, w_mv, w_xo, norm_ffn_g, w_up, w_down,
                norm_final_g)
    mem = _rmsnorm(mem_prompt.reshape(Bp * MEM_TOKENS, D_MODEL), P["norm_mem_g"], bf16)
    mk3 = _matmul(mem, P["w_mk"], tn=512, tk=D_MODEL).reshape(Bp, MEM_TOKENS, XA_W)
    mv3 = _matmul(mem, P["w_mv"], tn=512, tk=D_MODEL).reshape(Bp, MEM_TOKENS, XA_W)
    mk_p = mk3.reshape(Bp, MEM_TOKENS, XA_HEADS, XA_HEAD_DIM)
    mv_p = mv3.reshape(Bp, MEM_TOKENS, XA_HEADS, XA_HEAD_DIM)

    yp, sh_p, pl_p, wkv_p = _block(
        x_prompt, mk3, mv3, jnp.zeros((Bp, SHIFT_W), f32), jnp.zeros((Bp, POOL_BUF, POOL_W), f32),
        jnp.zeros((Bp, N_HEADS, HEAD_DIM, HEAD_DIM), f32), 0, P, chunk=64, heads=SUB_HEADS, nsub=4, pool_tt=256)
    ys, sh_s, pl_s, wkv_s = _block(
        x_sample, cache_mem_k.reshape(cache_mem_k.shape[1:]), cache_mem_v.reshape(cache_mem_v.shape[1:]),
        state_shift.reshape(state_shift.shape[1:]), state_pool.reshape(state_pool.shape[1:]),
        state_wkv.reshape(state_wkv.shape[1:]), PAST_LEN, P, chunk=8, heads=SUB_HEADS, nsub=8, pool_tt=8)
    ex = lambda t: t[None]
    return (yp, ys, ex(wkv_p), ex(sh_p), ex(pl_p), ex(mk_p), ex(mv_p), ex(wkv_s), ex(sh_s), ex(pl_s))
```

```python
import functools

import jax
import jax.numpy as jnp
from jax import lax
from jax.experimental import pallas as pl
from jax.experimental.pallas import tpu as pltpu

D_MODEL = 4096
RWKV_W = 2048
POOL_W = 2048
HEAD_DIM = 64
N_HEADS = RWKV_W // HEAD_DIM
DECAY_RANK = 96
AAA_RANK = 96
GATE_RANK = 256
SHIFT_W = 3 * RWKV_W + DECAY_RANK + AAA_RANK + GATE_RANK
POOL_WINDOWS = (2, 4, 8, 16)
N_POOL = len(POOL_WINDOWS)
POOL_GW = POOL_W // N_POOL
POOL_BUF = max(POOL_WINDOWS) - 1
POOL_CARRY = POOL_BUF + 1
MEM_TOKENS = 256
XA_HEADS = 4
XA_HEAD_DIM = 128
XA_W = XA_HEADS * XA_HEAD_DIM
PAST_LEN = 16384
RMS_EPS = 1e-6
GN_EPS = 64e-5

RKV_W = 3 * RWKV_W
LORA_RANKS = DECAY_RANK + AAA_RANK + GATE_RANK
LORA_W = 512
SUB_HEADS = 4
SUB_L = SUB_HEADS * HEAD_DIM
N_PVEC = 16

VMEM_LIMIT = 56 * 1024 * 1024

_NN = ((1,), (0,))
_NT = ((1,), (1,))
_TN = ((0,), (0,))

f32 = jnp.float32
bf16 = jnp.bfloat16


def _dot(a, b, dims):
    return lax.dot_general(a, b, (dims, ((), ())), preferred_element_type=f32)


def _sigmoid(x):
    return 1.0 / (1.0 + jnp.exp(-x))


def _rmsnorm_kernel(x_ref, g_ref, o_ref):
    x = x_ref[...]
    y = x * lax.rsqrt(jnp.mean(x * x, axis=-1, keepdims=True) + RMS_EPS)
    o_ref[...] = (y * g_ref[...]).astype(o_ref.dtype)


def _rmsnorm(x, g, out_dtype, tm=256):
    m, d = x.shape
    return pl.pallas_call(
        _rmsnorm_kernel,
        grid=(m // tm,),
        in_specs=[pl.BlockSpec((tm, d), lambda i: (i, 0)), pl.BlockSpec((1, d), lambda i: (0, 0))],
        out_specs=pl.BlockSpec((tm, d), lambda i: (i, 0)),
        out_shape=jax.ShapeDtypeStruct((m, d), out_dtype),
        compiler_params=pltpu.CompilerParams(dimension_semantics=("parallel",), vmem_limit_bytes=VMEM_LIMIT),
        name="rmsnorm",
    )(x, g.reshape(1, d))


def _matmul_kernel(*refs, nk, nk1, act, has_res, two_lhs, b_rows_are_outputs):
    refs = list(refs)
    a_ref = refs.pop(0)
    a2_ref = refs.pop(0) if two_lhs else None
    b_ref = refs.pop(0)
    res_ref = refs.pop(0) if has_res else None

    def finish(r):
        if act == "relu2":
            r = jnp.square(jnp.maximum(r, 0.0))
        if has_res:
            r = r + res_ref[...]
        return r

    def prod(lhs_ref):
        return _dot(lhs_ref[...], b_ref[...].astype(bf16), _NT if b_rows_are_outputs else _NN)

    if nk == 1:
        (o_ref,) = refs
        o_ref[...] = finish(prod(a_ref)).astype(o_ref.dtype)
        return

    o_ref, acc_ref = refs
    k = pl.program_id(2)

    @pl.when(k == 0)
    def _():
        acc_ref[...] = jnp.zeros_like(acc_ref)

    if two_lhs:
        @pl.when(k < nk1)
        def _():
            acc_ref[...] += prod(a_ref)

        @pl.when(k >= nk1)
        def _():
            acc_ref[...] += prod(a2_ref)
    else:
        acc_ref[...] += prod(a_ref)

    @pl.when(k == nk - 1)
    def _():
        o_ref[...] = finish(acc_ref[...]).astype(o_ref.dtype)


def _matmul(a, b, res=None, act=None, out_dtype=f32, a2=None, bt=None, col0=0, n=None, tm=1024, tn=1024, tk=1024):
    m, k1 = a.shape
    kd, ntot = b.shape if bt is None else bt.shape[::-1]
    n = ntot - col0 if n is None else n
    tm, tn, tk = min(tm, m), min(tn, n), min(tk, k1)
    assert m % tm == 0 and n % tn == 0 and kd % tk == 0 and k1 % tk == 0, (a.shape, kd, ntot, col0, n, tm, tn, tk)
    nk, nk1 = kd // tk, k1 // tk
    two_lhs = a2 is not None
    if two_lhs:
        assert a2.shape == (m, kd - k1)
        in_specs = [pl.BlockSpec((tm, tk), lambda i, j, k: (i, jnp.minimum(k, nk1 - 1))),
                    pl.BlockSpec((tm, tk), lambda i, j, k: (i, jnp.maximum(k - nk1, 0)))]
        args = [a, a2]
    else:
        assert k1 == kd
        in_specs = [pl.BlockSpec((tm, tk), lambda i, j, k: (i, k))]
        args = [a]
    if bt is None:
        assert col0 % tn == 0
        in_specs.append(pl.BlockSpec((tk, tn), lambda i, j, k: (k, col0 // tn + j)))
        args.append(b)
    else:
        assert col0 % 8 == 0 and tn % 8 == 0
        in_specs.append(pl.BlockSpec((pl.Element(tn), pl.Element(tk)),
                                     lambda i, j, k: (8 * (col0 // 8 + j * (tn // 8)), k * tk)))
        args.append(bt)
    if res is not None:
        in_specs.append(pl.BlockSpec((tm, tn), lambda i, j, k: (i, j)))
        args.append(res)
    return pl.pallas_call(
        functools.partial(_matmul_kernel, nk=nk, nk1=nk1, act=act, has_res=res is not None, two_lhs=two_lhs,
                          b_rows_are_outputs=bt is not None),
        grid=(m // tm, n // tn, nk),
        in_specs=in_specs,
        out_specs=pl.BlockSpec((tm, tn), lambda i, j, k: (i, j)),
        out_shape=jax.ShapeDtypeStruct((m, n), out_dtype),
        scratch_shapes=[pltpu.VMEM((tm, tn), f32)] if nk > 1 else [],
        compiler_params=pltpu.CompilerParams(
            dimension_semantics=("parallel", "parallel", "arbitrary"), vmem_limit_bytes=VMEM_LIMIT),
        name="matmul",
    )(*args)


PV_MU_R, PV_MU_K, PV_MU_V, PV_W0, PV_A0, PV_KK, PV_KA, PV_RK, PV_LNG, PV_LNB = range(10)


def _rwkv_kernel(zr_ref, zk_ref, zv_ref, zl_ref, prkv_ref, pl_ref, pv_ref, mul_ref, wdu_ref, wau_ref, wgu_ref,
                 s0_ref, o_ref, sout_ref, s_scr, hm_scr, tm_scr, ones_scr, carry_scr,
                 *, chunk, heads, nsub, nchunks):
    C, Hg = chunk, heads
    L, R = Hg * HEAD_DIM, Hg * C
    Lb, Hb = nsub * L, nsub * Hg
    c = pl.program_id(2)

    @pl.when(c == 0)
    def _():
        for h in range(Hb):
            s_scr[:, h * HEAD_DIM:(h + 1) * HEAD_DIM] = s0_ref[0, h]
        for i in range(3):
            carry_scr[i, 0:1, :Lb] = prkv_ref[0, i:i + 1, :]
        carry_scr[3, 0:1, :LORA_W] = pl_ref[0]
        hm_scr[...] = ((lax.broadcasted_iota(jnp.int32, (R, L), 0) // C)
                       == (lax.broadcasted_iota(jnp.int32, (R, L), 1) // HEAD_DIM)).astype(bf16)
        row = lax.broadcasted_iota(jnp.int32, (R, R), 0)
        col = lax.broadcasted_iota(jnp.int32, (R, R), 1)
        same = (row // C) == (col // C)
        tm_scr[0] = (same & ((row % C) < (col % C))).astype(bf16)
        tm_scr[1] = (same & ((row % C) <= (col % C))).astype(bf16)
        ones_scr[...] = ((lax.broadcasted_iota(jnp.int32, (SUB_L, SUB_L), 0) // HEAD_DIM)
                         == (lax.broadcasted_iota(jnp.int32, (SUB_L, SUB_L), 1) // HEAD_DIM)).astype(bf16)

    def mixed(z_ref, idx, width, mu):
        z = z_ref[0]
        prev = jnp.where(lax.broadcasted_iota(jnp.int32, z.shape, 0) == 0,
                         carry_scr[idx, 0:1, :width], pltpu.roll(z, 1, 0))
        carry_scr[idx, 0:1, :width] = z[C - 1:C, :]
        return z + mu * (prev - z)

    pv = lambda i: pv_ref[i:i + 1, :]
    r = mixed(zr_ref, 0, Lb, pv(PV_MU_R))
    k = mixed(zk_ref, 1, Lb, pv(PV_MU_K))
    v = mixed(zv_ref, 2, Lb, pv(PV_MU_V))
    zl = mixed(zl_ref, 3, LORA_W, mul_ref[...])

    dec = pv(PV_W0) + _dot(jnp.tanh(zl).astype(bf16), wdu_ref[...], _NN)
    sp = jnp.maximum(-dec, 0.0) + jnp.log(1.0 + jnp.exp(-jnp.abs(dec)))
    lw_all = -jnp.exp(-sp - 0.5)
    a = _sigmoid(pv(PV_A0) + _dot(zl.astype(bf16), wau_ref[...], _NN))
    gate = _dot(_sigmoid(zl).astype(bf16), wgu_ref[...], _NN)

    ones = ones_scr[...]

    def headsum(x):
        npc = Lb // SUB_L
        xs = jnp.concatenate([x[:, p * SUB_L:(p + 1) * SUB_L] for p in range(npc)], axis=0)
        sums = _dot(xs.astype(bf16), ones, _NN)
        return jnp.concatenate([sums[p * C:(p + 1) * C, :] for p in range(npc)], axis=1)

    kkr = k * pv(PV_KK)
    kk_all = kkr * lax.rsqrt(jnp.maximum(headsum(kkr * kkr), 1e-24))
    kf_all = k * (1.0 + (a - 1.0) * pv(PV_KA))
    kka_all = kk_all * a

    tri = (lax.broadcasted_iota(jnp.int32, (C, C), 1) <= lax.broadcasted_iota(jnp.int32, (C, C), 0)).astype(bf16)
    hm = hm_scr[...]
    strict, incl = tm_scr[0], tm_scr[1]

    def stack16(x):
        if C % 16 == 0:
            return jnp.concatenate([x.astype(bf16)] * Hg, axis=0)
        return jnp.concatenate([x] * Hg, axis=0).astype(bf16)

    subs = range(nsub)
    lanes = [slice(j * L, (j + 1) * L) for j in subs]

    def cumsum_t(lw):
        h1 = lw.astype(bf16)
        r1 = lw - h1.astype(f32)
        h2 = r1.astype(bf16)
        h3 = (r1 - h2.astype(f32)).astype(bf16)
        return _dot(tri, h1, _NN) + (_dot(tri, h2, _NN) + _dot(tri, h3, _NN))

    lw = [lw_all[:, ln] for ln in lanes]
    cum = [cumsum_t(lw[j]) for j in subs]
    g = [jnp.exp(cum[j]) for j in subs]
    gi = [jnp.exp(-cum[j]) for j in subs]
    gprev = [jnp.exp(cum[j] - lw[j]) for j in subs]
    g_end = [g[j][C - 1:C, :] for j in subs]
    kk = [kk_all[:, ln] for ln in lanes]
    kka = [kka_all[:, ln] for ln in lanes]
    kf = [kf_all[:, ln] for ln in lanes]
    lhs = [jnp.concatenate([stack16(-kk[j] * gprev[j]) * hm, stack16(r[:, lanes[j]] * g[j]) * hm], axis=0)
           for j in subs]
    rhs = [jnp.concatenate([stack16(kka[j] * gi[j]), stack16(kf[j] * gi[j])], axis=0) for j in subs]
    res = [_dot(rhs[j], lhs[j], _NT).astype(bf16) for j in subs]
    mt = [res[j][:R, :R] * strict for j in subs]
    pt = [res[j][:R, R:] * incl for j in subs]
    nqt = [jnp.concatenate([res[j][R:, :R] * strict, res[j][R:, R:] * incl], axis=1) for j in subs]

    def rows_by_head(x):
        return jnp.concatenate([x[:, h * HEAD_DIM:(h + 1) * HEAD_DIM] for h in range(Hg)], axis=0)

    def lanes_by_head(x):
        return jnp.concatenate([x[h * C:(h + 1) * C, :] for h in range(Hg)], axis=1)

    vs = [rows_by_head(v[:, lanes[j]]).astype(bf16) for j in subs]
    s = [s_scr[:, ln] for ln in lanes]
    base = [_dot(s[j].astype(bf16), lhs[j], _NT) + _dot(vs[j], nqt[j], _TN) for j in subs]

    x = [base[j][:, :R] for j in subs]
    mp = mt
    nsq = C.bit_length() - 1
    for i in range(nsq):
        if i < nsq - 1:
            both = [_dot(jnp.concatenate([x[j].astype(bf16), mp[j]], axis=0), mp[j], _NN) for j in subs]
            x = [x[j] + both[j][:HEAD_DIM] for j in subs]
            mp = [both[j][HEAD_DIM:].astype(bf16) for j in subs]
        else:
            x = [x[j] + _dot(x[j].astype(bf16), mp[j], _NN) for j in subs]
    u = [x[j].astype(bf16) for j in subs]

    y = [lanes_by_head((base[j][:, R:] + _dot(u[j], pt[j], _NN)).T) for j in subs]
    for j in subs:
        gl = g_end[j] * gi[j]
        s_scr[:, lanes[j]] = (g_end[j] * s[j] + _dot(u[j], stack16(kka[j] * gl) * hm, _NN)
                              + _dot(vs[j], stack16(kf[j] * gl) * hm, _TN))
    y = jnp.concatenate(y, axis=-1) if nsub > 1 else y[0]

    mean = headsum(y) * (1.0 / HEAD_DIM)
    d = y - mean
    var = headsum(d * d) * (1.0 / HEAD_DIM)
    yn = d * lax.rsqrt(var + GN_EPS) * pv(PV_LNG) + pv(PV_LNB)
    bonus = headsum(r * kf_all * pv(PV_RK)) * v
    o_ref[0] = ((yn + bonus) * gate).astype(o_ref.dtype)

    @pl.when(c == nchunks - 1)
    def _():
        for h in range(Hb):
            sout_ref[0, h] = s_scr[:, h * HEAD_DIM:(h + 1) * HEAD_DIM]


def _rwkv(z_rkv, z_lora, shift_prev, s0, P, *, chunk, heads, nsub):
    B, T, _ = z_rkv.shape
    C, Hg = chunk, heads
    L, R = Hg * HEAD_DIM, Hg * C
    Lb, Hb = nsub * L, nsub * Hg
    G = N_HEADS // Hb
    nchunks = T // C
    kb = RWKV_W // Lb
    prkv = shift_prev[:, :RKV_W].reshape(B, 3, RWKV_W)
    plora = jnp.pad(shift_prev[:, RKV_W:], ((0, 0), (0, LORA_W - LORA_RANKS))).reshape(B, 1, LORA_W)
    zcol = lambda off: pl.BlockSpec((1, C, Lb), lambda b, g, c: (b, c, off + g))
    wspec = lambda rows: pl.BlockSpec((rows, Lb), lambda b, g, c: (0, g))
    st = pl.BlockSpec((1, Hb, HEAD_DIM, HEAD_DIM), lambda b, g, c: (b, g, 0, 0))
    return pl.pallas_call(
        functools.partial(_rwkv_kernel, chunk=C, heads=Hg, nsub=nsub, nchunks=nchunks),
        grid=(B, G, nchunks),
        in_specs=[zcol(0), zcol(kb), zcol(2 * kb),
                  pl.BlockSpec((1, C, LORA_W), lambda b, g, c: (b, c, 0)),
                  pl.BlockSpec((1, 3, Lb), lambda b, g, c: (b, 0, g)),
                  pl.BlockSpec((1, 1, LORA_W), lambda b, g, c: (b, 0, 0)),
                  wspec(N_PVEC),
                  pl.BlockSpec((1, LORA_W), lambda b, g, c: (0, 0)),
                  wspec(LORA_W), wspec(LORA_W), wspec(LORA_W), st],
        out_specs=[pl.BlockSpec((1, C, Lb), lambda b, g, c: (b, c, g)), st],
        out_shape=[jax.ShapeDtypeStruct((B, T, RWKV_W), bf16),
                   jax.ShapeDtypeStruct((B, N_HEADS, HEAD_DIM, HEAD_DIM), f32)],
        scratch_shapes=[pltpu.VMEM((HEAD_DIM, Lb), f32), pltpu.VMEM((R, L), bf16), pltpu.VMEM((2, R, R), bf16),
                        pltpu.VMEM((SUB_L, SUB_L), bf16), pltpu.VMEM((4, 8, max(Lb, LORA_W)), f32)],
        compiler_params=pltpu.CompilerParams(
            dimension_semantics=("parallel", "parallel", "arbitrary"), vmem_limit_bytes=VMEM_LIMIT),
        name="rwkv",
    )(z_rkv, z_rkv, z_rkv, z_lora, prkv, plora, P["pvec"], P["mu_lora"], P["w_decay_up"], P["w_a_up"],
      P["w_g_up"], s0)


def _pool_kernel(z0_ref, z1_ref, z2_ref, z3_ref, prev_ref, w_ref, sc_ref, o_ref, carry_scr, *, tt, start_pos):
    i = pl.program_id(1)

    @pl.when(i == 0)
    def _():
        for gi in range(N_POOL):
            carry_scr[gi, 0:1, :] = jnp.zeros((1, POOL_GW), f32)
            carry_scr[gi, 1:POOL_CARRY, :] = prev_ref[0, :, gi * POOL_GW:(gi + 1) * POOL_GW]

    pos1 = start_pos + 1 + i * tt + lax.broadcasted_iota(jnp.int32, (tt, POOL_GW), 0)
    outs = []
    for gi, (z_ref, wdw) in enumerate(zip((z0_ref, z1_ref, z2_ref, z3_ref), POOL_WINDOWS)):
        x = z_ref[0]
        ext = jnp.concatenate([carry_scr[gi], x], axis=0)
        carry_scr[gi] = ext[tt:, :]
        acc, span = ext, 1
        while span < wdw:
            acc = acc + pltpu.roll(acc, span, 0)
            span *= 2
        mean = acc[POOL_CARRY:, :] / jnp.minimum(wdw, pos1).astype(f32)
        o = _dot((mean - x).astype(bf16), w_ref[gi].astype(bf16), _NN) * sc_ref[:, gi * POOL_GW:(gi + 1) * POOL_GW]
        outs.append(o.astype(o_ref.dtype))
    o_ref[0] = jnp.concatenate(outs, axis=-1)


def _pool(z, pool_prev, P, start_pos, tt):
    B, T, _ = z.shape
    tt = min(tt, T)
    zcol = lambda gi: pl.BlockSpec((1, tt, POOL_GW), lambda b, i: (b, i, gi))
    return pl.pallas_call(
        functools.partial(_pool_kernel, tt=tt, start_pos=start_pos),
        grid=(B, T // tt),
        in_specs=[zcol(0), zcol(1), zcol(2), zcol(3),
                  pl.BlockSpec((1, POOL_BUF, POOL_W), lambda b, i: (b, 0, 0)),
                  pl.BlockSpec((N_POOL, POOL_GW, POOL_GW), lambda b, i: (0, 0, 0)),
                  pl.BlockSpec((1, POOL_W), lambda b, i: (0, 0))],
        out_specs=pl.BlockSpec((1, tt, POOL_W), lambda b, i: (b, i, 0)),
        out_shape=jax.ShapeDtypeStruct((B, T, POOL_W), bf16),
        scratch_shapes=[pltpu.VMEM((N_POOL, POOL_CARRY, POOL_GW), f32)],
        compiler_params=pltpu.CompilerParams(
            dimension_semantics=("parallel", "arbitrary"), vmem_limit_bytes=VMEM_LIMIT),
        name="pool",
    )(z, z, z, z, pool_prev, P["w_pool"], P["pool_scale"])


def _xattn_kernel(q_ref, k_ref, v_ref, o_ref, *, head_axis):
    q = q_ref[0]
    outs = []
    for h in range(XA_HEADS):
        sl = slice(h * XA_HEAD_DIM, (h + 1) * XA_HEAD_DIM)
        if head_axis:
            kh, vh = k_ref[0, :, h, :], v_ref[0, :, h, :]
        else:
            kh, vh = k_ref[0, :, sl], v_ref[0, :, sl]
        s = _dot(q[:, sl], kh.astype(bf16), _NT) * (XA_HEAD_DIM ** -0.5)
        s = s - jnp.max(s, axis=-1, keepdims=True)
        e = jnp.exp(s)
        p = e / jnp.sum(e, axis=-1, keepdims=True)
        outs.append(_dot(p.astype(bf16), vh.astype(bf16), _NN))
    o_ref[0] = jnp.concatenate(outs, axis=-1).astype(o_ref.dtype)


def _xattn(q, mk, mv, tq):
    B, T, _ = q.shape
    tq = min(tq, T)
    head_axis = mk.ndim == 4
    if head_axis:
        kv_spec = pl.BlockSpec((1, MEM_TOKENS, XA_HEADS, XA_HEAD_DIM), lambda b, i: (b, 0, 0, 0))
    else:
        kv_spec = pl.BlockSpec((1, MEM_TOKENS, XA_W), lambda b, i: (b, 0, 0))
    return pl.pallas_call(
        functools.partial(_xattn_kernel, head_axis=head_axis),
        grid=(B, T // tq),
        in_specs=[pl.BlockSpec((1, tq, XA_W), lambda b, i: (b, i, 0)), kv_spec, kv_spec],
        out_specs=pl.BlockSpec((1, tq, XA_W), lambda b, i: (b, i, 0)),
        out_shape=jax.ShapeDtypeStruct((B, T, XA_W), bf16),
        compiler_params=pltpu.CompilerParams(
            dimension_semantics=("parallel", "parallel"), vmem_limit_bytes=VMEM_LIMIT),
        name="xattn",
    )(q, mk, mv)


def _block(x, mk, mv, shift_prev, pool_prev, wkv_prev, start_pos, P, chunk, heads, nsub, pool_tt):
    B, T, _ = x.shape
    M = B * T
    x2 = x.reshape(M, D_MODEL)

    h = _rmsnorm(x2, P["norm_mix_g"], bf16)
    w_in_t = P["w_in_t"]
    z_rkv = _matmul(h, None, bt=w_in_t, n=RKV_W, tn=512, tk=D_MODEL).reshape(B, T, RKV_W)
    z_lora = _matmul(h, None, bt=w_in_t, col0=RKV_W, n=LORA_W, tn=512, tk=D_MODEL).reshape(B, T, LORA_W)
    z_pool = _matmul(h, None, bt=w_in_t, col0=SHIFT_W, n=POOL_W, tn=512, tk=D_MODEL).reshape(B, T, POOL_W)
    o_rwkv, s_new = _rwkv(z_rkv, z_lora, shift_prev, wkv_prev, P, chunk=chunk, heads=heads, nsub=nsub)
    o_pool = _pool(z_pool, pool_prev, P, start_pos, pool_tt)
    x2 = _matmul(o_rwkv.reshape(M, RWKV_W), P["w_out"], a2=o_pool.reshape(M, POOL_W), res=x2, tn=512, tk=RWKV_W)
    new_shift = jnp.concatenate([z_rkv[:, -1], z_lora[:, -1, :LORA_RANKS]], axis=-1)
    new_pool = jnp.concatenate([pool_prev, z_pool], axis=1)[:, -POOL_BUF:]

    h = _rmsnorm(x2, P["norm_xa_g"], bf16)
    q = _matmul(h, P["w_xq"], out_dtype=bf16, tn=512, tk=D_MODEL).reshape(B, T, XA_W)
    o = _xattn(q, mk, mv, tq=512)
    x2 = _matmul(o.reshape(M, XA_W), P["w_xo"], res=x2, tk=512)

    h = _rmsnorm(x2, P["norm_ffn_g"], bf16)
    hid = _matmul(h, P["w_up"], act="relu2", out_dtype=bf16, tn=512, tk=D_MODEL)
    x2 = _matmul(hid, P["w_down"], res=x2, tk=2048)
    y_out = _rmsnorm(x2, P["norm_final_g"], f32).reshape(B, T, D_MODEL)

    return y_out, new_shift, new_pool, s_new


def _params(norm_mix_g, w_in, mu_shift, w0, w_decay_up, a0, w_a_up, w_g_up, k_k, k_a, r_k, lnx_g, lnx_b, w_pool,
            pool_scale, w_out, norm_xa_g, norm_mem_g, w_xq, w_mk, w_mv, w_xo, norm_ffn_g, w_up, w_down,
            norm_final_g):
    l = 0
    mu = mu_shift[l]
    pvec = jnp.stack([mu[:RWKV_W], mu[RWKV_W:2 * RWKV_W], mu[2 * RWKV_W:RKV_W], w0[l], a0[l], k_k[l],
                      k_a[l], r_k[l].reshape(RWKV_W), lnx_g[l], lnx_b[l]])
    pvec = jnp.concatenate([pvec, jnp.zeros((N_PVEC - pvec.shape[0], RWKV_W), f32)], axis=0)

    def lora_rows(w, first):
        return jnp.pad(w, ((first, LORA_W - first - w.shape[0]), (0, 0))).astype(bf16)

    return {
        "norm_mix_g": norm_mix_g[l], "w_in_t": jnp.transpose(w_in[l]),
        "pvec": pvec, "mu_lora": jnp.pad(mu[RKV_W:], (0, LORA_W - LORA_RANKS)).reshape(1, LORA_W),
        "w_decay_up": lora_rows(w_decay_up[l], 0), "w_a_up": lora_rows(w_a_up[l], DECAY_RANK),
        "w_g_up": lora_rows(w_g_up[l], DECAY_RANK + AAA_RANK),
        "w_pool": w_pool[l], "pool_scale": pool_scale[l].reshape(1, POOL_W), "w_out": w_out[l],
        "norm_xa_g": norm_xa_g[l], "w_xq": w_xq[l], "w_xo": w_xo[l],
        "norm_ffn_g": norm_ffn_g[l], "w_up": w_up[l], "w_down": w_down[l],
        "norm_final_g": norm_final_g, "norm_mem_g": norm_mem_g[l], "w_mk": w_mk[l], "w_mv": w_mv[l],
    }


def kernel(x_prompt, x_sample, mem_prompt, state_wkv, state_shift, state_pool, cache_mem_k, cache_mem_v, norm_mix_g, w_in, mu_shift, w0, w_decay_up, a0, w_a_up, w_g_up, k_k, k_a, r_k, lnx_g, lnx_b, w_pool, pool_scale, w_out, norm_xa_g, norm_mem_g, w_xq, w_mk, w_mv, w_xo, norm_ffn_g, w_up, w_down, norm_final_g):
    Bp = x_prompt.shape[0]
    P = _params(norm_mix_g, w_in, mu_shift, w0, w_decay_up, a0, w_a_up, w_g_up, k_k, k_a, r_k, lnx_g, lnx_b, w_pool,
                pool_scale, w_out, norm_xa_g, norm_mem_g, w_xq, w_mk, w_mv, w_xo, norm_ffn_g, w_up, w_down,
                norm_final_g)
    mem = _rmsnorm(mem_prompt.reshape(Bp * MEM_TOKENS, D_MODEL), P["norm_mem_g"], bf16)
    mk3 = _matmul(mem, P["w_mk"], tn=512, tk=D_MODEL).reshape(Bp, MEM_TOKENS, XA_W)
    mv3 = _matmul(mem, P["w_mv"], tn=512, tk=D_MODEL).reshape(Bp, MEM_TOKENS, XA_W)
    mk_p = mk3.reshape(Bp, MEM_TOKENS, XA_HEADS, XA_HEAD_DIM)
    mv_p = mv3.reshape(Bp, MEM_TOKENS, XA_HEADS, XA_HEAD_DIM)

    yp, sh_p, pl_p, wkv_p = _block(
        x_prompt, mk3, mv3, jnp.zeros((Bp, SHIFT_W), f32), jnp.zeros((Bp, POOL_BUF, POOL_W), f32),
        jnp.zeros((Bp, N_HEADS, HEAD_DIM, HEAD_DIM), f32), 0, P, chunk=64, heads=SUB_HEADS, nsub=8, pool_tt=256)
    ys, sh_s, pl_s, wkv_s = _block(
        x_sample, cache_mem_k.reshape(cache_mem_k.shape[1:]), cache_mem_v.reshape(cache_mem_v.shape[1:]),
        state_shift.reshape(state_shift.shape[1:]), state_pool.reshape(state_pool.shape[1:]),
        state_wkv.reshape(state_wkv.shape[1:]), PAST_LEN, P, chunk=8, heads=SUB_HEADS, nsub=8, pool_tt=8)
    ex = lambda t: t[None]
    return (yp, ys, ex(wkv_p), ex(sh_p), ex(pl_p), ex(mk_p), ex(mv_p), ex(wkv_s), ex(sh_s), ex(pl_s))
```

```python
import functools

import jax
import jax.numpy as jnp
from jax import lax
from jax.experimental import pallas as pl
from jax.experimental.pallas import tpu as pltpu

D_MODEL = 4096
RWKV_W = 2048
POOL_W = 2048
HEAD_DIM = 64
N_HEADS = RWKV_W // HEAD_DIM
DECAY_RANK = 96
AAA_RANK = 96
GATE_RANK = 256
SHIFT_W = 3 * RWKV_W + DECAY_RANK + AAA_RANK + GATE_RANK
POOL_WINDOWS = (2, 4, 8, 16)
N_POOL = len(POOL_WINDOWS)
POOL_GW = POOL_W // N_POOL
POOL_BUF = max(POOL_WINDOWS) - 1
POOL_CARRY = POOL_BUF + 1
MEM_TOKENS = 256
XA_HEADS = 4
XA_HEAD_DIM = 128
XA_W = XA_HEADS * XA_HEAD_DIM
PAST_LEN = 16384
RMS_EPS = 1e-6
GN_EPS = 64e-5

RKV_W = 3 * RWKV_W
LORA_RANKS = DECAY_RANK + AAA_RANK + GATE_RANK
LORA_W = 512
LANE_TILE = 128


def _tile_span(first, width):
    return (first // LANE_TILE * LANE_TILE, -(-(first + width) // LANE_TILE) * LANE_TILE)


WD_SPAN = _tile_span(0, DECAY_RANK)
AD_SPAN = _tile_span(DECAY_RANK, AAA_RANK)
GD_SPAN = _tile_span(DECAY_RANK + AAA_RANK, GATE_RANK)
SUB_HEADS = 4
SUB_L = SUB_HEADS * HEAD_DIM
N_PVEC = 16

VMEM_LIMIT = 56 * 1024 * 1024

_NN = ((1,), (0,))
_NT = ((1,), (1,))
_TN = ((0,), (0,))

f32 = jnp.float32
bf16 = jnp.bfloat16


def _dot(a, b, dims):
    return lax.dot_general(a, b, (dims, ((), ())), preferred_element_type=f32)


def _sigmoid(x):
    return 1.0 / (1.0 + jnp.exp(-x))


def _rmsnorm_kernel(x_ref, g_ref, o_ref):
    x = x_ref[...]
    y = x * lax.rsqrt(jnp.mean(x * x, axis=-1, keepdims=True) + RMS_EPS)
    o_ref[...] = (y * g_ref[...]).astype(o_ref.dtype)


def _rmsnorm(x, g, out_dtype, tm=512):
    m, d = x.shape
    return pl.pallas_call(
        _rmsnorm_kernel,
        grid=(m // tm,),
        in_specs=[pl.BlockSpec((tm, d), lambda i: (i, 0)), pl.BlockSpec((1, d), lambda i: (0, 0))],
        out_specs=pl.BlockSpec((tm, d), lambda i: (i, 0)),
        out_shape=jax.ShapeDtypeStruct((m, d), out_dtype),
        compiler_params=pltpu.CompilerParams(dimension_semantics=("parallel",), vmem_limit_bytes=VMEM_LIMIT),
        name="rmsnorm",
    )(x, g.reshape(1, d))


def _matmul_kernel(*refs, nk, nk1, act, has_res, two_lhs, b_rows_are_outputs):
    refs = list(refs)
    a_ref = refs.pop(0)
    a2_ref = refs.pop(0) if two_lhs else None
    b_ref = refs.pop(0)
    res_ref = refs.pop(0) if has_res else None

    def finish(r):
        if act == "relu2":
            r = jnp.square(jnp.maximum(r, 0.0))
        if has_res:
            r = r + res_ref[...]
        return r

    def prod(lhs_ref):
        return _dot(lhs_ref[...], b_ref[...].astype(bf16), _NT if b_rows_are_outputs else _NN)

    if nk == 1:
        (o_ref,) = refs
        o_ref[...] = finish(prod(a_ref)).astype(o_ref.dtype)
        return

    o_ref, acc_ref = refs
    k = pl.program_id(2)

    @pl.when(k == 0)
    def _():
        acc_ref[...] = jnp.zeros_like(acc_ref)

    if two_lhs:
        @pl.when(k < nk1)
        def _():
            acc_ref[...] += prod(a_ref)

        @pl.when(k >= nk1)
        def _():
            acc_ref[...] += prod(a2_ref)
    else:
        acc_ref[...] += prod(a_ref)

    @pl.when(k == nk - 1)
    def _():
        o_ref[...] = finish(acc_ref[...]).astype(o_ref.dtype)


def _matmul(a, b, res=None, act=None, out_dtype=f32, a2=None, bt=None, col0=0, n=None, tm=1024, tn=1024, tk=1024):
    m, k1 = a.shape
    kd, ntot = b.shape if bt is None else bt.shape[::-1]
    n = ntot - col0 if n is None else n
    tm, tn, tk = min(tm, m), min(tn, n), min(tk, k1)
    assert m % tm == 0 and n % tn == 0 and kd % tk == 0 and k1 % tk == 0, (a.shape, kd, ntot, col0, n, tm, tn, tk)
    nk, nk1 = kd // tk, k1 // tk
    two_lhs = a2 is not None
    if two_lhs:
        assert a2.shape == (m, kd - k1)
        in_specs = [pl.BlockSpec((tm, tk), lambda i, j, k: (i, jnp.minimum(k, nk1 - 1))),
                    pl.BlockSpec((tm, tk), lambda i, j, k: (i, jnp.maximum(k - nk1, 0)))]
        args = [a, a2]
    else:
        assert k1 == kd
        in_specs = [pl.BlockSpec((tm, tk), lambda i, j, k: (i, k))]
        args = [a]
    if bt is None:
        assert col0 % tn == 0
        in_specs.append(pl.BlockSpec((tk, tn), lambda i, j, k: (k, col0 // tn + j)))
        args.append(b)
    else:
        assert col0 % 8 == 0 and tn % 8 == 0
        in_specs.append(pl.BlockSpec((pl.Element(tn), pl.Element(tk)),
                                     lambda i, j, k: (8 * (col0 // 8 + j * (tn // 8)), k * tk)))
        args.append(bt)
    if res is not None:
        in_specs.append(pl.BlockSpec((tm, tn), lambda i, j, k: (i, j)))
        args.append(res)
    return pl.pallas_call(
        functools.partial(_matmul_kernel, nk=nk, nk1=nk1, act=act, has_res=res is not None, two_lhs=two_lhs,
                          b_rows_are_outputs=bt is not None),
        grid=(m // tm, n // tn, nk),
        in_specs=in_specs,
        out_specs=pl.BlockSpec((tm, tn), lambda i, j, k: (i, j)),
        out_shape=jax.ShapeDtypeStruct((m, n), out_dtype),
        scratch_shapes=[pltpu.VMEM((tm, tn), f32)] if nk > 1 else [],
        compiler_params=pltpu.CompilerParams(
            dimension_semantics=("parallel", "parallel", "arbitrary"), vmem_limit_bytes=VMEM_LIMIT),
        name="matmul",
    )(*args)


PV_MU_R, PV_MU_K, PV_MU_V, PV_W0, PV_A0, PV_KK, PV_KA, PV_RK, PV_LNG, PV_LNB = range(10)


def _rwkv_kernel(zr_ref, zk_ref, zv_ref, zl_ref, prkv_ref, pl_ref, pv_ref, mul_ref, wdu_ref, wau_ref, wgu_ref,
                 s0_ref, o_ref, sout_ref, s_scr, hm_scr, tm_scr, ones_scr, carry_scr,
                 *, chunk, heads, nsub, nchunks):
    C, Hg = chunk, heads
    L, R = Hg * HEAD_DIM, Hg * C
    Lb, Hb = nsub * L, nsub * Hg
    c = pl.program_id(2)

    @pl.when(c == 0)
    def _():
        for h in range(Hb):
            s_scr[:, h * HEAD_DIM:(h + 1) * HEAD_DIM] = s0_ref[0, h]
        for i in range(3):
            carry_scr[i, 0:1, :Lb] = prkv_ref[0, i:i + 1, :]
        carry_scr[3, 0:1, :LORA_W] = pl_ref[0]
        hm_scr[...] = ((lax.broadcasted_iota(jnp.int32, (R, L), 0) // C)
                       == (lax.broadcasted_iota(jnp.int32, (R, L), 1) // HEAD_DIM)).astype(bf16)
        row = lax.broadcasted_iota(jnp.int32, (R, R), 0)
        col = lax.broadcasted_iota(jnp.int32, (R, R), 1)
        same = (row // C) == (col // C)
        tm_scr[0] = (same & ((row % C) < (col % C))).astype(bf16)
        tm_scr[1] = (same & ((row % C) <= (col % C))).astype(bf16)
        ones_scr[...] = ((lax.broadcasted_iota(jnp.int32, (SUB_L, SUB_L), 0) // HEAD_DIM)
                         == (lax.broadcasted_iota(jnp.int32, (SUB_L, SUB_L), 1) // HEAD_DIM)).astype(bf16)

    def mixed(z_ref, idx, width, mu):
        z = z_ref[0]
        prev = jnp.where(lax.broadcasted_iota(jnp.int32, z.shape, 0) == 0,
                         carry_scr[idx, 0:1, :width], pltpu.roll(z, 1, 0))
        carry_scr[idx, 0:1, :width] = z[C - 1:C, :]
        return z + mu * (prev - z)

    pv = lambda i: pv_ref[i:i + 1, :]
    r = mixed(zr_ref, 0, Lb, pv(PV_MU_R))
    k = mixed(zk_ref, 1, Lb, pv(PV_MU_K))
    v = mixed(zv_ref, 2, Lb, pv(PV_MU_V))
    zl = mixed(zl_ref, 3, LORA_W, mul_ref[...])

    wd_in = jnp.tanh(zl[:, WD_SPAN[0]:WD_SPAN[1]]).astype(bf16)
    dec = pv(PV_W0) + _dot(wd_in, wdu_ref[...], _NN)
    sp = jnp.maximum(-dec, 0.0) + jnp.log(1.0 + jnp.exp(-jnp.abs(dec)))
    lw_all = -jnp.exp(-sp - 0.5)
    a = _sigmoid(pv(PV_A0) + _dot(zl[:, AD_SPAN[0]:AD_SPAN[1]].astype(bf16), wau_ref[...], _NN))
    gate = _dot(_sigmoid(zl[:, GD_SPAN[0]:GD_SPAN[1]]).astype(bf16), wgu_ref[...], _NN)

    ones = ones_scr[...]

    def headsum(x):
        npc = Lb // SUB_L
        xs = jnp.concatenate([x[:, p * SUB_L:(p + 1) * SUB_L] for p in range(npc)], axis=0)
        sums = _dot(xs.astype(bf16), ones, _NN)
        return jnp.concatenate([sums[p * C:(p + 1) * C, :] for p in range(npc)], axis=1)

    kkr = k * pv(PV_KK)
    kk_all = kkr * lax.rsqrt(jnp.maximum(headsum(kkr * kkr), 1e-24))
    kf_all = k * (1.0 + (a - 1.0) * pv(PV_KA))
    kka_all = kk_all * a

    tri = (lax.broadcasted_iota(jnp.int32, (C, C), 1) <= lax.broadcasted_iota(jnp.int32, (C, C), 0)).astype(bf16)
    hm = hm_scr[...]
    strict, incl = tm_scr[0], tm_scr[1]

    def stack16(x):
        if C % 16 == 0:
            return jnp.concatenate([x.astype(bf16)] * Hg, axis=0)
        return jnp.concatenate([x] * Hg, axis=0).astype(bf16)

    subs = range(nsub)
    lanes = [slice(j * L, (j + 1) * L) for j in subs]

    def cumsum_t(lw):
        h1 = lw.astype(bf16)
        r1 = lw - h1.astype(f32)
        h2 = r1.astype(bf16)
        h3 = (r1 - h2.astype(f32)).astype(bf16)
        return _dot(tri, h1, _NN) + (_dot(tri, h2, _NN) + _dot(tri, h3, _NN))

    lw = [lw_all[:, ln] for ln in lanes]
    cum = [cumsum_t(lw[j]) for j in subs]
    g = [jnp.exp(cum[j]) for j in subs]
    gi = [jnp.exp(-cum[j]) for j in subs]
    gprev = [jnp.exp(cum[j] - lw[j]) for j in subs]
    g_end = [g[j][C - 1:C, :] for j in subs]
    kk = [kk_all[:, ln] for ln in lanes]
    kka = [kka_all[:, ln] for ln in lanes]
    kf = [kf_all[:, ln] for ln in lanes]
    lhs = [jnp.concatenate([stack16(-kk[j] * gprev[j]) * hm, stack16(r[:, lanes[j]] * g[j]) * hm], axis=0)
           for j in subs]
    rhs = [jnp.concatenate([stack16(kka[j] * gi[j]), stack16(kf[j] * gi[j])], axis=0) for j in subs]
    res = [_dot(rhs[j], lhs[j], _NT).astype(bf16) for j in subs]
    mt = [res[j][:R, :R] * strict for j in subs]
    pt = [res[j][:R, R:] * incl for j in subs]
    nqt = [jnp.concatenate([res[j][R:, :R] * strict, res[j][R:, R:] * incl], axis=1) for j in subs]

    def rows_by_head(x):
        return jnp.concatenate([x[:, h * HEAD_DIM:(h + 1) * HEAD_DIM] for h in range(Hg)], axis=0)

    def lanes_by_head(x):
        return jnp.concatenate([x[h * C:(h + 1) * C, :] for h in range(Hg)], axis=1)

    vs = [rows_by_head(v[:, lanes[j]]).astype(bf16) for j in subs]
    s = [s_scr[:, ln] for ln in lanes]
    base = [_dot(s[j].astype(bf16), lhs[j], _NT) + _dot(vs[j], nqt[j], _TN) for j in subs]

    x = [base[j][:, :R] for j in subs]
    mp = mt
    nsq = C.bit_length() - 1
    for i in range(nsq):
        if i < nsq - 1:
            both = [_dot(jnp.concatenate([x[j].astype(bf16), mp[j]], axis=0), mp[j], _NN) for j in subs]
            x = [x[j] + both[j][:HEAD_DIM] for j in subs]
            mp = [both[j][HEAD_DIM:].astype(bf16) for j in subs]
        else:
            x = [x[j] + _dot(x[j].astype(bf16), mp[j], _NN) for j in subs]
    u = [x[j].astype(bf16) for j in subs]

    y = [lanes_by_head((base[j][:, R:] + _dot(u[j], pt[j], _NN)).T) for j in subs]
    for j in subs:
        gl = g_end[j] * gi[j]
        s_scr[:, lanes[j]] = (g_end[j] * s[j] + _dot(u[j], stack16(kka[j] * gl) * hm, _NN)
                              + _dot(vs[j], stack16(kf[j] * gl) * hm, _TN))
    y = jnp.concatenate(y, axis=-1) if nsub > 1 else y[0]

    mean = headsum(y) * (1.0 / HEAD_DIM)
    d = y - mean
    var = headsum(d * d) * (1.0 / HEAD_DIM)
    yn = d * lax.rsqrt(var + GN_EPS) * pv(PV_LNG) + pv(PV_LNB)
    bonus = headsum(r * kf_all * pv(PV_RK)) * v
    o_ref[0] = ((yn + bonus) * gate).astype(o_ref.dtype)

    @pl.when(c == nchunks - 1)
    def _():
        for h in range(Hb):
            sout_ref[0, h] = s_scr[:, h * HEAD_DIM:(h + 1) * HEAD_DIM]


def _rwkv(z, shift_prev, s0, P, *, chunk, heads, nsub):
    B, T, _ = z.shape
    C, Hg = chunk, heads
    L, R = Hg * HEAD_DIM, Hg * C
    Lb, Hb = nsub * L, nsub * Hg
    G = N_HEADS // Hb
    nchunks = T // C
    kb = RWKV_W // Lb
    prkv = shift_prev[:, :RKV_W].reshape(B, 3, RWKV_W)
    plora = jnp.pad(shift_prev[:, RKV_W:], ((0, 0), (0, LORA_W - LORA_RANKS))).reshape(B, 1, LORA_W)
    zcol = lambda off: pl.BlockSpec((1, C, Lb), lambda b, g, c: (b, c, off + g))
    wspec = lambda rows: pl.BlockSpec((rows, Lb), lambda b, g, c: (0, g))
    st = pl.BlockSpec((1, Hb, HEAD_DIM, HEAD_DIM), lambda b, g, c: (b, g, 0, 0))
    return pl.pallas_call(
        functools.partial(_rwkv_kernel, chunk=C, heads=Hg, nsub=nsub, nchunks=nchunks),
        grid=(B, G, nchunks),
        in_specs=[zcol(0), zcol(kb), zcol(2 * kb),
                  pl.BlockSpec((1, C, LORA_W), lambda b, g, c: (b, c, RKV_W // LORA_W)),
                  pl.BlockSpec((1, 3, Lb), lambda b, g, c: (b, 0, g)),
                  pl.BlockSpec((1, 1, LORA_W), lambda b, g, c: (b, 0, 0)),
                  wspec(N_PVEC),
                  pl.BlockSpec((1, LORA_W), lambda b, g, c: (0, 0)),
                  wspec(WD_SPAN[1] - WD_SPAN[0]), wspec(AD_SPAN[1] - AD_SPAN[0]), wspec(GD_SPAN[1] - GD_SPAN[0]), st],
        out_specs=[pl.BlockSpec((1, C, Lb), lambda b, g, c: (b, c, g)), st],
        out_shape=[jax.ShapeDtypeStruct((B, T, RWKV_W), bf16),
                   jax.ShapeDtypeStruct((B, N_HEADS, HEAD_DIM, HEAD_DIM), f32)],
        scratch_shapes=[pltpu.VMEM((HEAD_DIM, Lb), f32), pltpu.VMEM((R, L), bf16), pltpu.VMEM((2, R, R), bf16),
                        pltpu.VMEM((SUB_L, SUB_L), bf16), pltpu.VMEM((4, 8, max(Lb, LORA_W)), f32)],
        compiler_params=pltpu.CompilerParams(
            dimension_semantics=("parallel", "parallel", "arbitrary"), vmem_limit_bytes=VMEM_LIMIT),
        name="rwkv",
    )(z, z, z, z, prkv, plora, P["pvec"], P["mu_lora"], P["w_decay_up"], P["w_a_up"],
      P["w_g_up"], s0)


def _pool_kernel(z0_ref, z1_ref, z2_ref, z3_ref, prev_ref, w_ref, sc_ref, o_ref, carry_scr, *, tt, start_pos):
    i = pl.program_id(1)

    @pl.when(i == 0)
    def _():
        for gi in range(N_POOL):
            carry_scr[gi, 0:1, :] = jnp.zeros((1, POOL_GW), f32)
            carry_scr[gi, 1:POOL_CARRY, :] = prev_ref[0, :, gi * POOL_GW:(gi + 1) * POOL_GW]

    pos1 = start_pos + 1 + i * tt + lax.broadcasted_iota(jnp.int32, (tt, POOL_GW), 0)
    outs = []
    for gi, (z_ref, wdw) in enumerate(zip((z0_ref, z1_ref, z2_ref, z3_ref), POOL_WINDOWS)):
        x = z_ref[0]
        ext = jnp.concatenate([carry_scr[gi], x], axis=0)
        carry_scr[gi] = ext[tt:, :]
        acc, span = ext, 1
        while span < wdw:
            acc = acc + pltpu.roll(acc, span, 0)
            span *= 2
        mean = acc[POOL_CARRY:, :] / jnp.minimum(wdw, pos1).astype(f32)
        o = _dot((mean - x).astype(bf16), w_ref[gi].astype(bf16), _NN) * sc_ref[:, gi * POOL_GW:(gi + 1) * POOL_GW]
        outs.append(o.astype(o_ref.dtype))
    o_ref[0] = jnp.concatenate(outs, axis=-1)


def _pool(z, pool_prev, P, start_pos, tt):
    B, T, _ = z.shape
    tt = min(tt, T)
    zcol = lambda gi: pl.BlockSpec((1, tt, POOL_GW), lambda b, i: (b, i, gi))
    return pl.pallas_call(
        functools.partial(_pool_kernel, tt=tt, start_pos=start_pos),
        grid=(B, T // tt),
        in_specs=[zcol(0), zcol(1), zcol(2), zcol(3),
                  pl.BlockSpec((1, POOL_BUF, POOL_W), lambda b, i: (b, 0, 0)),
                  pl.BlockSpec((N_POOL, POOL_GW, POOL_GW), lambda b, i: (0, 0, 0)),
                  pl.BlockSpec((1, POOL_W), lambda b, i: (0, 0))],
        out_specs=pl.BlockSpec((1, tt, POOL_W), lambda b, i: (b, i, 0)),
        out_shape=jax.ShapeDtypeStruct((B, T, POOL_W), bf16),
        scratch_shapes=[pltpu.VMEM((N_POOL, POOL_CARRY, POOL_GW), f32)],
        compiler_params=pltpu.CompilerParams(
            dimension_semantics=("parallel", "arbitrary"), vmem_limit_bytes=VMEM_LIMIT),
        name="pool",
    )(z, z, z, z, pool_prev, P["w_pool"], P["pool_scale"])


def _xattn_kernel(q_ref, k_ref, v_ref, o_ref, *, head_axis):
    q = q_ref[0]
    outs = []
    for h in range(XA_HEADS):
        sl = slice(h * XA_HEAD_DIM, (h + 1) * XA_HEAD_DIM)
        if head_axis:
            kh, vh = k_ref[0, :, h, :], v_ref[0, :, h, :]
        else:
            kh, vh = k_ref[0, :, sl], v_ref[0, :, sl]
        s = _dot(q[:, sl], kh.astype(bf16), _NT) * (XA_HEAD_DIM ** -0.5)
        e = jnp.exp(s - jnp.max(s, axis=-1, keepdims=True))
        v1 = jnp.concatenate([vh.astype(bf16), jnp.ones((MEM_TOKENS, XA_HEAD_DIM), bf16)], axis=1)
        ov = _dot(e.astype(bf16), v1, _NN)
        outs.append(ov[:, :XA_HEAD_DIM] / ov[:, XA_HEAD_DIM:])
    o_ref[0] = jnp.concatenate(outs, axis=-1).astype(o_ref.dtype)


def _xattn(q, mk, mv, tq):
    B, T, _ = q.shape
    tq = min(tq, T)
    head_axis = mk.ndim == 4
    if head_axis:
        kv_spec = pl.BlockSpec((1, MEM_TOKENS, XA_HEADS, XA_HEAD_DIM), lambda b, i: (b, 0, 0, 0))
    else:
        kv_spec = pl.BlockSpec((1, MEM_TOKENS, XA_W), lambda b, i: (b, 0, 0))
    return pl.pallas_call(
        functools.partial(_xattn_kernel, head_axis=head_axis),
        grid=(B, T // tq),
        in_specs=[pl.BlockSpec((1, tq, XA_W), lambda b, i: (b, i, 0)), kv_spec, kv_spec],
        out_specs=pl.BlockSpec((1, tq, XA_W), lambda b, i: (b, i, 0)),
        out_shape=jax.ShapeDtypeStruct((B, T, XA_W), bf16),
        compiler_params=pltpu.CompilerParams(
            dimension_semantics=("parallel", "parallel"), vmem_limit_bytes=VMEM_LIMIT),
        name="xattn",
    )(q, mk, mv)


def _block(x, mk, mv, shift_prev, pool_prev, wkv_prev, start_pos, P, chunk, heads, nsub, pool_tt):
    B, T, _ = x.shape
    M = B * T
    x2 = x.reshape(M, D_MODEL)

    h = _rmsnorm(x2, P["norm_mix_g"], bf16)
    w_in_t = P["w_in_t"]
    z = _matmul(h, None, bt=w_in_t, n=RKV_W + LORA_W, tn=512, tk=D_MODEL).reshape(B, T, RKV_W + LORA_W)
    z_pool = _matmul(h, None, bt=w_in_t, col0=SHIFT_W, n=POOL_W, tn=512, tk=D_MODEL).reshape(B, T, POOL_W)
    o_rwkv, s_new = _rwkv(z, shift_prev, wkv_prev, P, chunk=chunk, heads=heads, nsub=nsub)
    o_pool = _pool(z_pool, pool_prev, P, start_pos, pool_tt)
    x2 = _matmul(o_rwkv.reshape(M, RWKV_W), P["w_out"], a2=o_pool.reshape(M, POOL_W), res=x2, tn=512, tk=RWKV_W)
    new_shift = z[:, -1, :SHIFT_W]
    new_pool = jnp.concatenate([pool_prev, z_pool], axis=1)[:, -POOL_BUF:]

    h = _rmsnorm(x2, P["norm_xa_g"], bf16)
    q = _matmul(h, P["w_xq"], out_dtype=bf16, tn=512, tk=D_MODEL).reshape(B, T, XA_W)
    o = _xattn(q, mk, mv, tq=512)
    x2 = _matmul(o.reshape(M, XA_W), P["w_xo"], res=x2, tk=512)

    h = _rmsnorm(x2, P["norm_ffn_g"], bf16)
    hid = _matmul(h, P["w_up"], act="relu2", out_dtype=bf16, tn=512, tk=D_MODEL)
    x2 = _matmul(hid, P["w_down"], res=x2, tk=2048)
    y_out = _rmsnorm(x2, P["norm_final_g"], f32).reshape(B, T, D_MODEL)

    return y_out, new_shift, new_pool, s_new


def _params(norm_mix_g, w_in, mu_shift, w0, w_decay_up, a0, w_a_up, w_g_up, k_k, k_a, r_k, lnx_g, lnx_b, w_pool,
            pool_scale, w_out, norm_xa_g, norm_mem_g, w_xq, w_mk, w_mv, w_xo, norm_ffn_g, w_up, w_down,
            norm_final_g):
    l = 0
    mu = mu_shift[l]
    pvec = jnp.stack([mu[:RWKV_W], mu[RWKV_W:2 * RWKV_W], mu[2 * RWKV_W:RKV_W], w0[l], a0[l], k_k[l],
                      k_a[l], r_k[l].reshape(RWKV_W), lnx_g[l], lnx_b[l]])
    pvec = jnp.concatenate([pvec, jnp.zeros((N_PVEC - pvec.shape[0], RWKV_W), f32)], axis=0)

    def lora_rows(w, first, span):
        return jnp.pad(w, ((first - span[0], span[1] - first - w.shape[0]), (0, 0))).astype(bf16)

    return {
        "norm_mix_g": norm_mix_g[l], "w_in_t": jnp.transpose(w_in[l]),
        "pvec": pvec, "mu_lora": jnp.pad(mu[RKV_W:], (0, LORA_W - LORA_RANKS)).reshape(1, LORA_W),
        "w_decay_up": lora_rows(w_decay_up[l], 0, WD_SPAN), "w_a_up": lora_rows(w_a_up[l], DECAY_RANK, AD_SPAN),
        "w_g_up": lora_rows(w_g_up[l], DECAY_RANK + AAA_RANK, GD_SPAN),
        "w_pool": w_pool[l], "pool_scale": pool_scale[l].reshape(1, POOL_W), "w_out": w_out[l],
        "norm_xa_g": norm_xa_g[l], "w_xq": w_xq[l], "w_xo": w_xo[l],
        "norm_ffn_g": norm_ffn_g[l], "w_up": w_up[l], "w_down": w_down[l],
        "norm_final_g": norm_final_g, "norm_mem_g": norm_mem_g[l], "w_mk": w_mk[l], "w_mv": w_mv[l],
    }


def kernel(x_prompt, x_sample, mem_prompt, state_wkv, state_shift, state_pool, cache_mem_k, cache_mem_v, norm_mix_g, w_in, mu_shift, w0, w_decay_up, a0, w_a_up, w_g_up, k_k, k_a, r_k, lnx_g, lnx_b, w_pool, pool_scale, w_out, norm_xa_g, norm_mem_g, w_xq, w_mk, w_mv, w_xo, norm_ffn_g, w_up, w_down, norm_final_g):
    Bp = x_prompt.shape[0]
    P = _params(norm_mix_g, w_in, mu_shift, w0, w_decay_up, a0, w_a_up, w_g_up, k_k, k_a, r_k, lnx_g, lnx_b, w_pool,
                pool_scale, w_out, norm_xa_g, norm_mem_g, w_xq, w_mk, w_mv, w_xo, norm_ffn_g, w_up, w_down,
                norm_final_g)
    mem = _rmsnorm(mem_prompt.reshape(Bp * MEM_TOKENS, D_MODEL), P["norm_mem_g"], bf16)
    mk3 = _matmul(mem, P["w_mk"], tn=512, tk=D_MODEL).reshape(Bp, MEM_TOKENS, XA_W)
    mv3 = _matmul(mem, P["w_mv"], tn=512, tk=D_MODEL).reshape(Bp, MEM_TOKENS, XA_W)
    mk_p = mk3.reshape(Bp, MEM_TOKENS, XA_HEADS, XA_HEAD_DIM)
    mv_p = mv3.reshape(Bp, MEM_TOKENS, XA_HEADS, XA_HEAD_DIM)

    yp, sh_p, pl_p, wkv_p = _block(
        x_prompt, mk3, mv3, jnp.zeros((Bp, SHIFT_W), f32), jnp.zeros((Bp, POOL_BUF, POOL_W), f32),
        jnp.zeros((Bp, N_HEADS, HEAD_DIM, HEAD_DIM), f32), 0, P, chunk=64, heads=SUB_HEADS, nsub=8, pool_tt=512)
    ys, sh_s, pl_s, wkv_s = _block(
        x_sample, cache_mem_k.reshape(cache_mem_k.shape[1:]), cache_mem_v.reshape(cache_mem_v.shape[1:]),
        state_shift.reshape(state_shift.shape[1:]), state_pool.reshape(state_pool.shape[1:]),
        state_wkv.reshape(state_wkv.shape[1:]), PAST_LEN, P, chunk=8, heads=SUB_HEADS, nsub=8, pool_tt=8)
    ex = lambda t: t[None]
    return (yp, ys, ex(wkv_p), ex(sh_p), ex(pl_p), ex(mk_p), ex(mv_p), ex(wkv_s), ex(sh_s), ex(pl_s))
```

```python
import functools

import jax
import jax.numpy as jnp
from jax import lax
from jax.experimental import pallas as pl
from jax.experimental.pallas import tpu as pltpu

D_MODEL = 4096
RWKV_W = 2048
POOL_W = 2048
MIX_W = RWKV_W + POOL_W
HEAD_DIM = 64
N_HEADS = RWKV_W // HEAD_DIM
DECAY_RANK = 96
AAA_RANK = 96
GATE_RANK = 256
SHIFT_W = 3 * RWKV_W + DECAY_RANK + AAA_RANK + GATE_RANK
POOL_WINDOWS = (2, 4, 8, 16)
N_POOL = len(POOL_WINDOWS)
POOL_GW = POOL_W // N_POOL
POOL_BUF = max(POOL_WINDOWS) - 1
POOL_CARRY = POOL_BUF + 1
MEM_TOKENS = 256
XA_HEADS = 4
XA_HEAD_DIM = 128
XA_W = XA_HEADS * XA_HEAD_DIM
PAST_LEN = 16384
RMS_EPS = 1e-6
GN_EPS = 64e-5

RKV_W = 3 * RWKV_W
LORA_RANKS = DECAY_RANK + AAA_RANK + GATE_RANK
LORA_W = 512
LANE_TILE = 128


def _tile_span(first, width):
    return (first // LANE_TILE * LANE_TILE, -(-(first + width) // LANE_TILE) * LANE_TILE)


WD_SPAN = _tile_span(0, DECAY_RANK)
AD_SPAN = _tile_span(DECAY_RANK, AAA_RANK)
GD_SPAN = _tile_span(DECAY_RANK + AAA_RANK, GATE_RANK)
SUB_HEADS = 4
SUB_L = SUB_HEADS * HEAD_DIM
N_PVEC = 16

VMEM_LIMIT = 56 * 1024 * 1024

_NN = ((1,), (0,))
_NT = ((1,), (1,))
_TN = ((0,), (0,))

f32 = jnp.float32
bf16 = jnp.bfloat16


def _dot(a, b, dims):
    return lax.dot_general(a, b, (dims, ((), ())), preferred_element_type=f32)


def _sigmoid(x):
    return 1.0 / (1.0 + jnp.exp(-x))


def _rmsnorm_kernel(x_ref, g_ref, o_ref):
    x = x_ref[...]
    y = x * lax.rsqrt(jnp.mean(x * x, axis=-1, keepdims=True) + RMS_EPS)
    o_ref[...] = (y * g_ref[...]).astype(o_ref.dtype)


def _rmsnorm(x, g, out_dtype, tm=512):
    m, d = x.shape
    return pl.pallas_call(
        _rmsnorm_kernel,
        grid=(m // tm,),
        in_specs=[pl.BlockSpec((tm, d), lambda i: (i, 0)), pl.BlockSpec((1, d), lambda i: (0, 0))],
        out_specs=pl.BlockSpec((tm, d), lambda i: (i, 0)),
        out_shape=jax.ShapeDtypeStruct((m, d), out_dtype),
        compiler_params=pltpu.CompilerParams(dimension_semantics=("parallel",), vmem_limit_bytes=VMEM_LIMIT),
        name="rmsnorm",
    )(x, g.reshape(1, d))


def _matmul_kernel(*refs, nk, act, has_res, b_rows_are_outputs):
    refs = list(refs)
    a_ref = refs.pop(0)
    b_ref = refs.pop(0)
    res_ref = refs.pop(0) if has_res else None

    def finish(r):
        if act == "relu2":
            r = jnp.square(jnp.maximum(r, 0.0))
        if has_res:
            r = r + res_ref[...]
        return r

    def prod():
        return _dot(a_ref[...], b_ref[...].astype(bf16), _NT if b_rows_are_outputs else _NN)

    if nk == 1:
        (o_ref,) = refs
        o_ref[...] = finish(prod()).astype(o_ref.dtype)
        return

    o_ref, acc_ref = refs
    k = pl.program_id(2)

    @pl.when(k == 0)
    def _():
        acc_ref[...] = jnp.zeros_like(acc_ref)

    acc_ref[...] += prod()

    @pl.when(k == nk - 1)
    def _():
        o_ref[...] = finish(acc_ref[...]).astype(o_ref.dtype)


def _matmul(a, b, res=None, act=None, out_dtype=f32, bt=None, col0=0, n=None, tm=1024, tn=1024, tk=1024):
    m, kd = a.shape
    kw, ntot = b.shape if bt is None else bt.shape[::-1]
    n = ntot - col0 if n is None else n
    tm, tn, tk = min(tm, m), min(tn, n), min(tk, kd)
    assert kw == kd and m % tm == 0 and n % tn == 0 and kd % tk == 0, (a.shape, kw, ntot, col0, n, tm, tn, tk)
    nk = kd // tk
    in_specs = [pl.BlockSpec((tm, tk), lambda i, j, k: (i, k))]
    args = [a]
    if bt is None:
        assert col0 % tn == 0
        in_specs.append(pl.BlockSpec((tk, tn), lambda i, j, k: (k, col0 // tn + j)))
        args.append(b)
    else:
        assert col0 % 8 == 0 and tn % 8 == 0
        in_specs.append(pl.BlockSpec((pl.Element(tn), pl.Element(tk)),
                                     lambda i, j, k: (8 * (col0 // 8 + j * (tn // 8)), k * tk)))
        args.append(bt)
    if res is not None:
        in_specs.append(pl.BlockSpec((tm, tn), lambda i, j, k: (i, j)))
        args.append(res)
    return pl.pallas_call(
        functools.partial(_matmul_kernel, nk=nk, act=act, has_res=res is not None,
                          b_rows_are_outputs=bt is not None),
        grid=(m // tm, n // tn, nk),
        in_specs=in_specs,
        out_specs=pl.BlockSpec((tm, tn), lambda i, j, k: (i, j)),
        out_shape=jax.ShapeDtypeStruct((m, n), out_dtype),
        scratch_shapes=[pltpu.VMEM((tm, tn), f32)] if nk > 1 else [],
        compiler_params=pltpu.CompilerParams(
            dimension_semantics=("parallel", "parallel", "arbitrary"), vmem_limit_bytes=VMEM_LIMIT),
        name="matmul",
    )(*args)


PV_MU_R, PV_MU_K, PV_MU_V, PV_W0, PV_A0, PV_KK, PV_KA, PV_RK, PV_LNG, PV_LNB = range(10)


def _pool_group(x, carry_ref, w, scale, pos1, wdw, C):
    ext = jnp.concatenate([carry_ref[...], x], axis=0)
    carry_ref[...] = ext[C:, :]
    acc, span = ext, 1
    while span < wdw:
        acc = acc + pltpu.roll(acc, span, 0)
        span *= 2
    mean = acc[POOL_CARRY:, :] / jnp.minimum(wdw, pos1).astype(f32)
    return _dot((mean - x).astype(bf16), w, _NN) * scale


def _mixer_kernel(zr_ref, zk_ref, zv_ref, zl_ref, prkv_ref, pl_ref, pv_ref, mul_ref, wdu_ref, wau_ref, wgu_ref,
                  s0_ref, zp0_ref, zp1_ref, zp2_ref, zp3_ref, pprev_ref, wp_ref, psc_ref,
                  o_ref, sout_ref, s_scr, hm_scr, tm_scr, ones_scr, carry_scr, pcarry_scr,
                  *, chunk, heads, nsub, nseq, nchunks, start_pos):
    C, Hg = chunk, heads
    L, R = Hg * HEAD_DIM, Hg * C
    Lb, Hb = nsub * L, nsub * Hg
    rows = nseq * C
    c = pl.program_id(1)

    @pl.when(c == 0)
    def _():
        for bi in range(nseq):
            for h in range(Hb):
                s_scr[bi, :, h * HEAD_DIM:(h + 1) * HEAD_DIM] = s0_ref[bi, h]
            for i in range(3):
                carry_scr[bi, i, 0:1, :Lb] = prkv_ref[bi, i:i + 1, :]
            carry_scr[bi, 3, 0:1, :LORA_W] = pl_ref[bi]
            for gi in range(N_POOL):
                pcarry_scr[bi, gi, 0:1, :] = jnp.zeros((1, POOL_GW), f32)
                pcarry_scr[bi, gi, 1:POOL_CARRY, :] = pprev_ref[bi, :, gi * POOL_GW:(gi + 1) * POOL_GW]
        hm_scr[...] = ((lax.broadcasted_iota(jnp.int32, (R, L), 0) // C)
                       == (lax.broadcasted_iota(jnp.int32, (R, L), 1) // HEAD_DIM)).astype(bf16)
        row = lax.broadcasted_iota(jnp.int32, (R, R), 0)
        col = lax.broadcasted_iota(jnp.int32, (R, R), 1)
        same = (row // C) == (col // C)
        tm_scr[0] = (same & ((row % C) < (col % C))).astype(bf16)
        tm_scr[1] = (same & ((row % C) <= (col % C))).astype(bf16)
        ones_scr[...] = ((lax.broadcasted_iota(jnp.int32, (SUB_L, SUB_L), 0) // HEAD_DIM)
                         == (lax.broadcasted_iota(jnp.int32, (SUB_L, SUB_L), 1) // HEAD_DIM)).astype(bf16)

    def mixed(z_ref, idx, width, mu):
        z = z_ref[...].reshape(rows, width)
        first = jnp.concatenate([jnp.broadcast_to(carry_scr[bi, idx, 0:1, :width], (C, width))
                                 for bi in range(nseq)], axis=0)
        prev = jnp.where(lax.broadcasted_iota(jnp.int32, z.shape, 0) % C == 0, first, pltpu.roll(z, 1, 0))
        for bi in range(nseq):
            carry_scr[bi, idx, 0:1, :width] = z[(bi + 1) * C - 1:(bi + 1) * C, :]
        return z + mu * (prev - z)

    pv = lambda i: pv_ref[i:i + 1, :]
    r = mixed(zr_ref, 0, Lb, pv(PV_MU_R))
    k = mixed(zk_ref, 1, Lb, pv(PV_MU_K))
    v = mixed(zv_ref, 2, Lb, pv(PV_MU_V))
    zl = mixed(zl_ref, 3, LORA_W, mul_ref[...])

    pos1 = start_pos + 1 + c * C + lax.broadcasted_iota(jnp.int32, (C, POOL_GW), 0)
    for bi in range(nseq):
        for gi, (zp_ref, wdw) in enumerate(zip((zp0_ref, zp1_ref, zp2_ref, zp3_ref), POOL_WINDOWS)):
            cols = slice(gi * POOL_GW, (gi + 1) * POOL_GW)
            o_pool = _pool_group(zp_ref[bi], pcarry_scr.at[bi, gi], wp_ref[gi], psc_ref[:, cols], pos1, wdw, C)
            o_ref[bi, :, RWKV_W + gi * POOL_GW:RWKV_W + (gi + 1) * POOL_GW] = o_pool.astype(o_ref.dtype)

    wd_in = jnp.tanh(zl[:, WD_SPAN[0]:WD_SPAN[1]]).astype(bf16)
    dec = pv(PV_W0) + _dot(wd_in, wdu_ref[...], _NN)
    sp = jnp.maximum(-dec, 0.0) + jnp.log(1.0 + jnp.exp(-jnp.abs(dec)))
    lw_all = -jnp.exp(-sp - 0.5)
    a = _sigmoid(pv(PV_A0) + _dot(zl[:, AD_SPAN[0]:AD_SPAN[1]].astype(bf16), wau_ref[...], _NN))
    gate = _dot(_sigmoid(zl[:, GD_SPAN[0]:GD_SPAN[1]]).astype(bf16), wgu_ref[...], _NN)

    ones = ones_scr[...]

    def headsum(x):
        npc = Lb // SUB_L
        xs = jnp.concatenate([x[:, p * SUB_L:(p + 1) * SUB_L] for p in range(npc)], axis=0)
        sums = _dot(xs.astype(bf16), ones, _NN)
        return jnp.concatenate([sums[p * rows:(p + 1) * rows, :] for p in range(npc)], axis=1)

    kkr = k * pv(PV_KK)
    kk_all = kkr * lax.rsqrt(jnp.maximum(headsum(kkr * kkr), 1e-24))
    kf_all = k * (1.0 + (a - 1.0) * pv(PV_KA))
    kka_all = kk_all * a

    tri = (lax.broadcasted_iota(jnp.int32, (C, C), 1) <= lax.broadcasted_iota(jnp.int32, (C, C), 0)).astype(bf16)
    hm = hm_scr[...]
    strict, incl = tm_scr[0], tm_scr[1]

    def stack16(x):
        if C % 16 == 0:
            return jnp.concatenate([x.astype(bf16)] * Hg, axis=0)
        return jnp.concatenate([x] * Hg, axis=0).astype(bf16)

    chains = [(bi, jg) for bi in range(nseq) for jg in range(nsub)]
    subs = range(len(chains))
    seq_rows = [slice(bi * C, (bi + 1) * C) for bi, _ in chains]
    lanes = [slice(jg * L, (jg + 1) * L) for _, jg in chains]
    part = lambda x, j: x[seq_rows[j], lanes[j]]

    def cumsum_t(lw):
        h1 = lw.astype(bf16)
        r1 = lw - h1.astype(f32)
        h2 = r1.astype(bf16)
        h3 = (r1 - h2.astype(f32)).astype(bf16)
        return _dot(tri, h1, _NN) + (_dot(tri, h2, _NN) + _dot(tri, h3, _NN))

    lw = [part(lw_all, j) for j in subs]
    cum = [cumsum_t(lw[j]) for j in subs]
    g = [jnp.exp(cum[j]) for j in subs]
    gi = [jnp.exp(-cum[j]) for j in subs]
    gprev = [jnp.exp(cum[j] - lw[j]) for j in subs]
    g_end = [g[j][C - 1:C, :] for j in subs]
    kk = [part(kk_all, j) for j in subs]
    kka = [part(kka_all, j) for j in subs]
    kf = [part(kf_all, j) for j in subs]
    lhs = [jnp.concatenate([stack16(-kk[j] * gprev[j]) * hm, stack16(part(r, j) * g[j]) * hm], axis=0)
           for j in subs]
    rhs = [jnp.concatenate([stack16(kka[j] * gi[j]), stack16(kf[j] * gi[j])], axis=0) for j in subs]
    res = [_dot(rhs[j], lhs[j], _NT).astype(bf16) for j in subs]
    mt = [res[j][:R, :R] * strict for j in subs]
    pt = [res[j][:R, R:] * incl for j in subs]
    nqt = [jnp.concatenate([res[j][R:, :R] * strict, res[j][R:, R:] * incl], axis=1) for j in subs]

    def rows_by_head(x):
        return jnp.concatenate([x[:, h * HEAD_DIM:(h + 1) * HEAD_DIM] for h in range(Hg)], axis=0)

    def lanes_by_head(x):
        return jnp.concatenate([x[h * C:(h + 1) * C, :] for h in range(Hg)], axis=1)

    vs = [rows_by_head(part(v, j)).astype(bf16) for j in subs]
    s = [s_scr[bi, :, lanes[j]] for j, (bi, _) in enumerate(chains)]
    base = [_dot(s[j].astype(bf16), lhs[j], _NT) + _dot(vs[j], nqt[j], _TN) for j in subs]

    x = [base[j][:, :R] for j in subs]
    mp = mt
    nsq = C.bit_length() - 1
    for i in range(nsq):
        if i < nsq - 1:
            both = [_dot(jnp.concatenate([x[j].astype(bf16), mp[j]], axis=0), mp[j], _NN) for j in subs]
            x = [x[j] + both[j][:HEAD_DIM] for j in subs]
            mp = [both[j][HEAD_DIM:].astype(bf16) for j in subs]
        else:
            x = [x[j] + _dot(x[j].astype(bf16), mp[j], _NN) for j in subs]
    u = [x[j].astype(bf16) for j in subs]

    y = [lanes_by_head((base[j][:, R:] + _dot(u[j], pt[j], _NN)).T) for j in subs]
    for j, (bi, _) in enumerate(chains):
        gl = g_end[j] * gi[j]
        s_scr[bi, :, lanes[j]] = (g_end[j] * s[j] + _dot(u[j], stack16(kka[j] * gl) * hm, _NN)
                                  + _dot(vs[j], stack16(kf[j] * gl) * hm, _TN))
    y = jnp.concatenate([jnp.concatenate(y[bi * nsub:(bi + 1) * nsub], axis=-1) for bi in range(nseq)],
                        axis=0)

    mean = headsum(y) * (1.0 / HEAD_DIM)
    d = y - mean
    var = headsum(d * d) * (1.0 / HEAD_DIM)
    yn = d * lax.rsqrt(var + GN_EPS) * pv(PV_LNG) + pv(PV_LNB)
    bonus = headsum(r * kf_all * pv(PV_RK)) * v
    o_rwkv = ((yn + bonus) * gate).astype(o_ref.dtype)
    for bi in range(nseq):
        o_ref[bi, :, :RWKV_W] = o_rwkv[bi * C:(bi + 1) * C, :]

    @pl.when(c == nchunks - 1)
    def _():
        for bi in range(nseq):
            for h in range(Hb):
                sout_ref[bi, h] = s_scr[bi, :, h * HEAD_DIM:(h + 1) * HEAD_DIM]


def _mixer(z, z_pool, shift_prev, pool_prev, s0, P, *, start_pos, chunk, heads, nsub, nseq):
    B, T, _ = z.shape
    C, Hg = chunk, heads
    L, R = Hg * HEAD_DIM, Hg * C
    assert nsub * Hg == N_HEADS and B % nseq == 0 and T % C == 0
    nchunks = T // C
    prkv = shift_prev[:, :RKV_W].reshape(B, 3, RWKV_W)
    plora = jnp.pad(shift_prev[:, RKV_W:], ((0, 0), (0, LORA_W - LORA_RANKS))).reshape(B, 1, LORA_W)
    zcol = lambda w, j: pl.BlockSpec((nseq, C, w), lambda b, c: (b, c, j))
    whole = lambda shape: pl.BlockSpec(shape, lambda b, c: (0,) * len(shape))
    per_b = lambda shape: pl.BlockSpec((nseq,) + shape, lambda b, c: (b,) + (0,) * len(shape))
    st = per_b((N_HEADS, HEAD_DIM, HEAD_DIM))
    return pl.pallas_call(
        functools.partial(_mixer_kernel, chunk=C, heads=Hg, nsub=nsub, nseq=nseq, nchunks=nchunks,
                          start_pos=start_pos),
        grid=(B // nseq, nchunks),
        in_specs=[zcol(RWKV_W, 0), zcol(RWKV_W, 1), zcol(RWKV_W, 2), zcol(LORA_W, RKV_W // LORA_W),
                  per_b((3, RWKV_W)), per_b((1, LORA_W)), whole((N_PVEC, RWKV_W)), whole((1, LORA_W)),
                  whole((WD_SPAN[1] - WD_SPAN[0], RWKV_W)), whole((AD_SPAN[1] - AD_SPAN[0], RWKV_W)),
                  whole((GD_SPAN[1] - GD_SPAN[0], RWKV_W)), st,
                  zcol(POOL_GW, 0), zcol(POOL_GW, 1), zcol(POOL_GW, 2), zcol(POOL_GW, 3),
                  per_b((POOL_BUF, POOL_W)), whole((N_POOL, POOL_GW, POOL_GW)), whole((1, POOL_W))],
        out_specs=[pl.BlockSpec((nseq, C, MIX_W), lambda b, c: (b, c, 0)), st],
        out_shape=[jax.ShapeDtypeStruct((B, T, MIX_W), bf16),
                   jax.ShapeDtypeStruct((B, N_HEADS, HEAD_DIM, HEAD_DIM), f32)],
        scratch_shapes=[pltpu.VMEM((nseq, HEAD_DIM, RWKV_W), f32), pltpu.VMEM((R, L), bf16),
                        pltpu.VMEM((2, R, R), bf16), pltpu.VMEM((SUB_L, SUB_L), bf16),
                        pltpu.VMEM((nseq, 4, 8, RWKV_W), f32),
                        pltpu.VMEM((nseq, N_POOL, POOL_CARRY, POOL_GW), f32)],
        compiler_params=pltpu.CompilerParams(
            dimension_semantics=("parallel", "arbitrary"), vmem_limit_bytes=VMEM_LIMIT),
        name="mixer",
    )(z, z, z, z, prkv, plora, P["pvec"], P["mu_lora"], P["w_decay_up"], P["w_a_up"], P["w_g_up"], s0,
      z_pool, z_pool, z_pool, z_pool, pool_prev, P["w_pool"], P["pool_scale"])


def _xattn_kernel(q_ref, k_ref, v_ref, o_ref, *, head_axis):
    q = q_ref[0]
    outs = []
    for h in range(XA_HEADS):
        sl = slice(h * XA_HEAD_DIM, (h + 1) * XA_HEAD_DIM)
        if head_axis:
            kh, vh = k_ref[0, :, h, :], v_ref[0, :, h, :]
        else:
            kh, vh = k_ref[0, :, sl], v_ref[0, :, sl]
        s = _dot(q[:, sl], kh.astype(bf16), _NT) * (XA_HEAD_DIM ** -0.5)
        e = jnp.exp(s - jnp.max(s, axis=-1, keepdims=True))
        v1 = jnp.concatenate([vh.astype(bf16), jnp.ones((MEM_TOKENS, XA_HEAD_DIM), bf16)], axis=1)
        ov = _dot(e.astype(bf16), v1, _NN)
        outs.append(ov[:, :XA_HEAD_DIM] / ov[:, XA_HEAD_DIM:])
    o_ref[0] = jnp.concatenate(outs, axis=-1).astype(o_ref.dtype)


def _xattn(q, mk, mv, tq):
    B, T, _ = q.shape
    tq = min(tq, T)
    head_axis = mk.ndim == 4
    if head_axis:
        kv_spec = pl.BlockSpec((1, MEM_TOKENS, XA_HEADS, XA_HEAD_DIM), lambda b, i: (b, 0, 0, 0))
    else:
        kv_spec = pl.BlockSpec((1, MEM_TOKENS, XA_W), lambda b, i: (b, 0, 0))
    return pl.pallas_call(
        functools.partial(_xattn_kernel, head_axis=head_axis),
        grid=(B, T // tq),
        in_specs=[pl.BlockSpec((1, tq, XA_W), lambda b, i: (b, i, 0)), kv_spec, kv_spec],
        out_specs=pl.BlockSpec((1, tq, XA_W), lambda b, i: (b, i, 0)),
        out_shape=jax.ShapeDtypeStruct((B, T, XA_W), bf16),
        compiler_params=pltpu.CompilerParams(
            dimension_semantics=("parallel", "parallel"), vmem_limit_bytes=VMEM_LIMIT),
        name="xattn",
    )(q, mk, mv)


def _block(x, mk, mv, shift_prev, pool_prev, wkv_prev, start_pos, P, chunk, nseq):
    B, T, _ = x.shape
    M = B * T
    x2 = x.reshape(M, D_MODEL)

    h = _rmsnorm(x2, P["norm_mix_g"], bf16)
    w_in_t = P["w_in_t"]
    z = _matmul(h, None, bt=w_in_t, n=RKV_W + LORA_W, tn=512, tk=D_MODEL).reshape(B, T, RKV_W + LORA_W)
    z_pool = _matmul(h, None, bt=w_in_t, col0=SHIFT_W, n=POOL_W, tn=512, tk=D_MODEL).reshape(B, T, POOL_W)
    mix, s_new = _mixer(z, z_pool, shift_prev, pool_prev, wkv_prev, P, start_pos=start_pos, chunk=chunk,
                        heads=SUB_HEADS, nsub=N_HEADS // SUB_HEADS, nseq=nseq)
    x2 = _matmul(mix.reshape(M, MIX_W), P["w_out"], res=x2, tn=512, tk=MIX_W)
    new_shift = z[:, -1, :SHIFT_W]
    new_pool = jnp.concatenate([pool_prev, z_pool], axis=1)[:, -POOL_BUF:]

    h = _rmsnorm(x2, P["norm_xa_g"], bf16)
    q = _matmul(h, P["w_xq"], out_dtype=bf16, tn=512, tk=D_MODEL).reshape(B, T, XA_W)
    o = _xattn(q, mk, mv, tq=512)
    x2 = _matmul(o.reshape(M, XA_W), P["w_xo"], res=x2, tk=512)

    h = _rmsnorm(x2, P["norm_ffn_g"], bf16)
    hid = _matmul(h, P["w_up"], act="relu2", out_dtype=bf16, tn=512, tk=D_MODEL)
    x2 = _matmul(hid, P["w_down"], res=x2, tk=2048)
    y_out = _rmsnorm(x2, P["norm_final_g"], f32).reshape(B, T, D_MODEL)

    return y_out, new_shift, new_pool, s_new


def _params(norm_mix_g, w_in, mu_shift, w0, w_decay_up, a0, w_a_up, w_g_up, k_k, k_a, r_k, lnx_g, lnx_b, w_pool,
            pool_scale, w_out, norm_xa_g, norm_mem_g, w_xq, w_mk, w_mv, w_xo, norm_ffn_g, w_up, w_down,
            norm_final_g):
    l = 0
    mu = mu_shift[l]
    pvec = jnp.stack([mu[:RWKV_W], mu[RWKV_W:2 * RWKV_W], mu[2 * RWKV_W:RKV_W], w0[l], a0[l], k_k[l],
                      k_a[l], r_k[l].reshape(RWKV_W), lnx_g[l], lnx_b[l]])
    pvec = jnp.concatenate([pvec, jnp.zeros((N_PVEC - pvec.shape[0], RWKV_W), f32)], axis=0)

    def lora_rows(w, first, span):
        return jnp.pad(w, ((first - span[0], span[1] - first - w.shape[0]), (0, 0))).astype(bf16)

    return {
        "norm_mix_g": norm_mix_g[l], "w_in_t": jnp.transpose(w_in[l]),
        "pvec": pvec, "mu_lora": jnp.pad(mu[RKV_W:], (0, LORA_W - LORA_RANKS)).reshape(1, LORA_W),
        "w_decay_up": lora_rows(w_decay_up[l], 0, WD_SPAN), "w_a_up": lora_rows(w_a_up[l], DECAY_RANK, AD_SPAN),
        "w_g_up": lora_rows(w_g_up[l], DECAY_RANK + AAA_RANK, GD_SPAN),
        "w_pool": w_pool[l].astype(bf16), "pool_scale": pool_scale[l].reshape(1, POOL_W), "w_out": w_out[l],
        "norm_xa_g": norm_xa_g[l], "w_xq": w_xq[l], "w_xo": w_xo[l],
        "norm_ffn_g": norm_ffn_g[l], "w_up": w_up[l], "w_down": w_down[l],
        "norm_final_g": norm_final_g, "norm_mem_g": norm_mem_g[l], "w_mk": w_mk[l], "w_mv": w_mv[l],
    }


def kernel(x_prompt, x_sample, mem_prompt, state_wkv, state_shift, state_pool, cache_mem_k, cache_mem_v, norm_mix_g, w_in, mu_shift, w0, w_decay_up, a0, w_a_up, w_g_up, k_k, k_a, r_k, lnx_g, lnx_b, w_pool, pool_scale, w_out, norm_xa_g, norm_mem_g, w_xq, w_mk, w_mv, w_xo, norm_ffn_g, w_up, w_down, norm_final_g):
    Bp = x_prompt.shape[0]
    P = _params(norm_mix_g, w_in, mu_shift, w0, w_decay_up, a0, w_a_up, w_g_up, k_k, k_a, r_k, lnx_g, lnx_b, w_pool,
                pool_scale, w_out, norm_xa_g, norm_mem_g, w_xq, w_mk, w_mv, w_xo, norm_ffn_g, w_up, w_down,
                norm_final_g)
    mem = _rmsnorm(mem_prompt.reshape(Bp * MEM_TOKENS, D_MODEL), P["norm_mem_g"], bf16)
    mk3 = _matmul(mem, P["w_mk"], tn=512, tk=D_MODEL).reshape(Bp, MEM_TOKENS, XA_W)
    mv3 = _matmul(mem, P["w_mv"], tn=512, tk=D_MODEL).reshape(Bp, MEM_TOKENS, XA_W)
    mk_p = mk3.reshape(Bp, MEM_TOKENS, XA_HEADS, XA_HEAD_DIM)
    mv_p = mv3.reshape(Bp, MEM_TOKENS, XA_HEADS, XA_HEAD_DIM)

    yp, sh_p, pl_p, wkv_p = _block(
        x_prompt, mk3, mv3, jnp.zeros((Bp, SHIFT_W), f32), jnp.zeros((Bp, POOL_BUF, POOL_W), f32),
        jnp.zeros((Bp, N_HEADS, HEAD_DIM, HEAD_DIM), f32), 0, P, chunk=64, nseq=1)
    ys, sh_s, pl_s, wkv_s = _block(
        x_sample, cache_mem_k.reshape(cache_mem_k.shape[1:]), cache_mem_v.reshape(cache_mem_v.shape[1:]),
        state_shift.reshape(state_shift.shape[1:]), state_pool.reshape(state_pool.shape[1:]),
        state_wkv.reshape(state_wkv.shape[1:]), PAST_LEN, P, chunk=x_sample.shape[1], nseq=4)
    ex = lambda t: t[None]
    return (yp, ys, ex(wkv_p), ex(sh_p), ex(pl_p), ex(mk_p), ex(mv_p), ex(wkv_s), ex(sh_s), ex(pl_s))
```

```python
import functools

import jax
import jax.numpy as jnp
from jax import lax
from jax.experimental import pallas as pl
from jax.experimental.pallas import tpu as pltpu

D_MODEL = 4096
RWKV_W = 2048
POOL_W = 2048
MIX_W = RWKV_W + POOL_W
HEAD_DIM = 64
N_HEADS = RWKV_W // HEAD_DIM
DECAY_RANK = 96
AAA_RANK = 96
GATE_RANK = 256
SHIFT_W = 3 * RWKV_W + DECAY_RANK + AAA_RANK + GATE_RANK
POOL_WINDOWS = (2, 4, 8, 16)
N_POOL = len(POOL_WINDOWS)
POOL_GW = POOL_W // N_POOL
POOL_BUF = max(POOL_WINDOWS) - 1
POOL_CARRY = POOL_BUF + 1
MEM_TOKENS = 256
XA_HEADS = 4
XA_HEAD_DIM = 128
XA_W = XA_HEADS * XA_HEAD_DIM
PAST_LEN = 16384
RMS_EPS = 1e-6
GN_EPS = 64e-5

RKV_W = 3 * RWKV_W
LORA_RANKS = DECAY_RANK + AAA_RANK + GATE_RANK
LORA_W = 512
LANE_TILE = 128


def _tile_span(first, width):
    return (first // LANE_TILE * LANE_TILE, -(-(first + width) // LANE_TILE) * LANE_TILE)


WD_SPAN = _tile_span(0, DECAY_RANK)
AD_SPAN = _tile_span(DECAY_RANK, AAA_RANK)
GD_SPAN = _tile_span(DECAY_RANK + AAA_RANK, GATE_RANK)
SUB_HEADS = 4
SUB_L = SUB_HEADS * HEAD_DIM
N_PVEC = 16

VMEM_LIMIT = 56 * 1024 * 1024

_NN = ((1,), (0,))
_NT = ((1,), (1,))
_TN = ((0,), (0,))

f32 = jnp.float32
bf16 = jnp.bfloat16


def _dot(a, b, dims):
    return lax.dot_general(a, b, (dims, ((), ())), preferred_element_type=f32)


def _sigmoid(x):
    return 1.0 / (1.0 + jnp.exp(-x))


def _rmsnorm_kernel(x_ref, g_ref, o_ref):
    x = x_ref[...]
    y = x * lax.rsqrt(jnp.mean(x * x, axis=-1, keepdims=True) + RMS_EPS)
    o_ref[...] = (y * g_ref[...]).astype(o_ref.dtype)


def _rmsnorm(x, g, out_dtype, tm=512):
    m, d = x.shape
    return pl.pallas_call(
        _rmsnorm_kernel,
        grid=(m // tm,),
        in_specs=[pl.BlockSpec((tm, d), lambda i: (i, 0)), pl.BlockSpec((1, d), lambda i: (0, 0))],
        out_specs=pl.BlockSpec((tm, d), lambda i: (i, 0)),
        out_shape=jax.ShapeDtypeStruct((m, d), out_dtype),
        compiler_params=pltpu.CompilerParams(dimension_semantics=("parallel",), vmem_limit_bytes=VMEM_LIMIT),
        name="rmsnorm",
    )(x, g.reshape(1, d))


def _finish(r, res_ref, act):
    if act == "relu2":
        r = jnp.square(jnp.maximum(r, 0.0))
    if res_ref is not None:
        r = r + res_ref[...]
    return r


def _matmul_full_k_kernel(a_ref, b_ref, *rest, act, has_res, b_rows_are_outputs):
    res_ref = rest[0] if has_res else None
    o_ref, w_scr = rest[-2:]

    @pl.when(pl.program_id(1) == 0)
    def _():
        w_scr[...] = b_ref[...].astype(bf16)

    r = _dot(a_ref[...], w_scr[...], _NT if b_rows_are_outputs else _NN)
    o_ref[...] = _finish(r, res_ref, act).astype(o_ref.dtype)


def _matmul_k_tiled_kernel(a_ref, b_ref, *rest, nk, act, has_res):
    res_ref = rest[0] if has_res else None
    o_ref, acc_ref = rest[-2:]
    k = pl.program_id(2)

    @pl.when(k == 0)
    def _():
        acc_ref[...] = jnp.zeros_like(acc_ref)

    acc_ref[...] += _dot(a_ref[...], b_ref[...].astype(bf16), _NN)

    @pl.when(k == nk - 1)
    def _():
        o_ref[...] = _finish(acc_ref[...], res_ref, act).astype(o_ref.dtype)


def _matmul(a, b, res=None, act=None, out_dtype=f32, bt=None, col0=0, n=None, tm=1024, tn=1024, tk=1024):
    m, kd = a.shape
    kw, ntot = b.shape if bt is None else bt.shape[::-1]
    n = ntot - col0 if n is None else n
    tm, tn, tk = min(tm, m), min(tn, n), min(tk, kd)
    assert kw == kd and m % tm == 0 and n % tn == 0 and kd % tk == 0, (a.shape, kw, ntot, col0, n, tm, tn, tk)
    nk = kd // tk
    out_shape = jax.ShapeDtypeStruct((m, n), out_dtype)
    if nk == 1:
        if bt is None:
            assert col0 % tn == 0
            w, w_block = b, (tk, tn)
            w_spec = pl.BlockSpec(w_block, lambda j, i: (0, col0 // tn + j))
        else:
            assert col0 % 8 == 0 and tn % 8 == 0
            w, w_block = bt, (tn, tk)
            w_spec = pl.BlockSpec((pl.Element(tn), pl.Element(tk)),
                                  lambda j, i: (8 * (col0 // 8 + j * (tn // 8)), 0))
        in_specs = [pl.BlockSpec((tm, tk), lambda j, i: (i, 0)), w_spec]
        args = [a, w]
        if res is not None:
            in_specs.append(pl.BlockSpec((tm, tn), lambda j, i: (i, j)))
            args.append(res)
        return pl.pallas_call(
            functools.partial(_matmul_full_k_kernel, act=act, has_res=res is not None,
                              b_rows_are_outputs=bt is not None),
            grid=(n // tn, m // tm),
            in_specs=in_specs,
            out_specs=pl.BlockSpec((tm, tn), lambda j, i: (i, j)),
            out_shape=out_shape,
            scratch_shapes=[pltpu.VMEM(w_block, bf16)],
            compiler_params=pltpu.CompilerParams(
                dimension_semantics=("parallel", "arbitrary"), vmem_limit_bytes=VMEM_LIMIT),
            name="matmul",
        )(*args)
    assert bt is None and col0 == 0
    in_specs = [pl.BlockSpec((tm, tk), lambda i, j, k: (i, k)), pl.BlockSpec((tk, tn), lambda i, j, k: (k, j))]
    args = [a, b]
    if res is not None:
        in_specs.append(pl.BlockSpec((tm, tn), lambda i, j, k: (i, j)))
        args.append(res)
    return pl.pallas_call(
        functools.partial(_matmul_k_tiled_kernel, nk=nk, act=act, has_res=res is not None),
        grid=(m // tm, n // tn, nk),
        in_specs=in_specs,
        out_specs=pl.BlockSpec((tm, tn), lambda i, j, k: (i, j)),
        out_shape=out_shape,
        scratch_shapes=[pltpu.VMEM((tm, tn), f32)],
        compiler_params=pltpu.CompilerParams(
            dimension_semantics=("parallel", "parallel", "arbitrary"), vmem_limit_bytes=VMEM_LIMIT),
        name="matmul",
    )(*args)


PV_MU_R, PV_MU_K, PV_MU_V, PV_W0, PV_A0, PV_KK, PV_KA, PV_RK, PV_LNG, PV_LNB = range(10)


def _pool_group(x, carry_ref, w, scale, pos1, wdw, C):
    ext = jnp.concatenate([carry_ref[...], x], axis=0)
    carry_ref[...] = ext[C:, :]
    acc, span = ext, 1
    while span < wdw:
        acc = acc + pltpu.roll(acc, span, 0)
        span *= 2
    mean = acc[POOL_CARRY:, :] / jnp.minimum(wdw, pos1).astype(f32)
    return _dot((mean - x).astype(bf16), w, _NN) * scale


def _mixer_kernel(zr_ref, zk_ref, zv_ref, zl_ref, prkv_ref, pl_ref, pv_ref, mul_ref, wdu_ref, wau_ref, wgu_ref,
                  s0_ref, zp0_ref, zp1_ref, zp2_ref, zp3_ref, pprev_ref, wp_ref, psc_ref,
                  o_ref, sout_ref, s_scr, hm_scr, tm_scr, ones_scr, carry_scr, pcarry_scr,
                  *, chunk, heads, nsub, nseq, nchunks, start_pos):
    C, Hg = chunk, heads
    L, R = Hg * HEAD_DIM, Hg * C
    Lb, Hb = nsub * L, nsub * Hg
    rows = nseq * C
    c = pl.program_id(1)

    @pl.when(c == 0)
    def _():
        for bi in range(nseq):
            for h in range(Hb):
                s_scr[bi, :, h * HEAD_DIM:(h + 1) * HEAD_DIM] = s0_ref[bi, h]
            for i in range(3):
                carry_scr[bi, i, 0:1, :Lb] = prkv_ref[bi, i:i + 1, :]
            carry_scr[bi, 3, 0:1, :LORA_W] = pl_ref[bi]
            for gi in range(N_POOL):
                pcarry_scr[bi, gi, 0:1, :] = jnp.zeros((1, POOL_GW), f32)
                pcarry_scr[bi, gi, 1:POOL_CARRY, :] = pprev_ref[bi, :, gi * POOL_GW:(gi + 1) * POOL_GW]
        hm_scr[...] = ((lax.broadcasted_iota(jnp.int32, (R, L), 0) // C)
                       == (lax.broadcasted_iota(jnp.int32, (R, L), 1) // HEAD_DIM)).astype(bf16)
        row = lax.broadcasted_iota(jnp.int32, (R, R), 0)
        col = lax.broadcasted_iota(jnp.int32, (R, R), 1)
        same = (row // C) == (col // C)
        tm_scr[0] = (same & ((row % C) < (col % C))).astype(bf16)
        tm_scr[1] = (same & ((row % C) <= (col % C))).astype(bf16)
        ones_scr[...] = ((lax.broadcasted_iota(jnp.int32, (SUB_L, SUB_L), 0) // HEAD_DIM)
                         == (lax.broadcasted_iota(jnp.int32, (SUB_L, SUB_L), 1) // HEAD_DIM)).astype(bf16)

    def mixed(z_ref, idx, width, mu):
        z = z_ref[...].reshape(rows, width)
        first = jnp.concatenate([jnp.broadcast_to(carry_scr[bi, idx, 0:1, :width], (C, width))
                                 for bi in range(nseq)], axis=0)
        prev = jnp.where(lax.broadcasted_iota(jnp.int32, z.shape, 0) % C == 0, first, pltpu.roll(z, 1, 0))
        for bi in range(nseq):
            carry_scr[bi, idx, 0:1, :width] = z[(bi + 1) * C - 1:(bi + 1) * C, :]
        return z + mu * (prev - z)

    pv = lambda i: pv_ref[i:i + 1, :]
    r = mixed(zr_ref, 0, Lb, pv(PV_MU_R))
    k = mixed(zk_ref, 1, Lb, pv(PV_MU_K))
    v = mixed(zv_ref, 2, Lb, pv(PV_MU_V))
    zl = mixed(zl_ref, 3, LORA_W, mul_ref[...])

    pos1 = start_pos + 1 + c * C + lax.broadcasted_iota(jnp.int32, (C, POOL_GW), 0)
    for bi in range(nseq):
        for gi, (zp_ref, wdw) in enumerate(zip((zp0_ref, zp1_ref, zp2_ref, zp3_ref), POOL_WINDOWS)):
            cols = slice(gi * POOL_GW, (gi + 1) * POOL_GW)
            o_pool = _pool_group(zp_ref[bi], pcarry_scr.at[bi, gi], wp_ref[gi], psc_ref[:, cols], pos1, wdw, C)
            o_ref[bi, :, RWKV_W + gi * POOL_GW:RWKV_W + (gi + 1) * POOL_GW] = o_pool.astype(o_ref.dtype)

    wd_in = jnp.tanh(zl[:, WD_SPAN[0]:WD_SPAN[1]]).astype(bf16)
    dec = pv(PV_W0) + _dot(wd_in, wdu_ref[...], _NN)
    sp = jnp.maximum(-dec, 0.0) + jnp.log(1.0 + jnp.exp(-jnp.abs(dec)))
    lw_all = -jnp.exp(-sp - 0.5)
    a = _sigmoid(pv(PV_A0) + _dot(zl[:, AD_SPAN[0]:AD_SPAN[1]].astype(bf16), wau_ref[...], _NN))
    gate = _dot(_sigmoid(zl[:, GD_SPAN[0]:GD_SPAN[1]]).astype(bf16), wgu_ref[...], _NN)

    ones = ones_scr[...]

    def headsum(x):
        npc = Lb // SUB_L
        xs = jnp.concatenate([x[:, p * SUB_L:(p + 1) * SUB_L] for p in range(npc)], axis=0)
        sums = _dot(xs.astype(bf16), ones, _NN)
        return jnp.concatenate([sums[p * rows:(p + 1) * rows, :] for p in range(npc)], axis=1)

    kkr = k * pv(PV_KK)
    kk_all = kkr * lax.rsqrt(jnp.maximum(headsum(kkr * kkr), 1e-24))
    kf_all = k * (1.0 + (a - 1.0) * pv(PV_KA))
    kka_all = kk_all * a

    tri = (lax.broadcasted_iota(jnp.int32, (C, C), 1) <= lax.broadcasted_iota(jnp.int32, (C, C), 0)).astype(bf16)
    hm = hm_scr[...]
    strict, incl = tm_scr[0], tm_scr[1]

    def stack16(x):
        if C % 16 == 0:
            return jnp.concatenate([x.astype(bf16)] * Hg, axis=0)
        return jnp.concatenate([x] * Hg, axis=0).astype(bf16)

    chains = [(bi, jg) for bi in range(nseq) for jg in range(nsub)]
    subs = range(len(chains))
    seq_rows = [slice(bi * C, (bi + 1) * C) for bi, _ in chains]
    lanes = [slice(jg * L, (jg + 1) * L) for _, jg in chains]
    part = lambda x, j: x[seq_rows[j], lanes[j]]

    def cumsum_t(lw):
        h1 = lw.astype(bf16)
        r1 = lw - h1.astype(f32)
        h2 = r1.astype(bf16)
        h3 = (r1 - h2.astype(f32)).astype(bf16)
        return _dot(tri, h1, _NN) + (_dot(tri, h2, _NN) + _dot(tri, h3, _NN))

    lw = [part(lw_all, j) for j in subs]
    cum = [cumsum_t(lw[j]) for j in subs]
    g = [jnp.exp(cum[j]) for j in subs]
    gi = [jnp.exp(-cum[j]) for j in subs]
    gprev = [jnp.exp(cum[j] - lw[j]) for j in subs]
    g_end = [g[j][C - 1:C, :] for j in subs]
    kk = [part(kk_all, j) for j in subs]
    kka = [part(kka_all, j) for j in subs]
    kf = [part(kf_all, j) for j in subs]
    lhs = [jnp.concatenate([stack16(-kk[j] * gprev[j]) * hm, stack16(part(r, j) * g[j]) * hm], axis=0)
           for j in subs]
    rhs = [jnp.concatenate([stack16(kka[j] * gi[j]), stack16(kf[j] * gi[j])], axis=0) for j in subs]
    res = [_dot(rhs[j], lhs[j], _NT).astype(bf16) for j in subs]
    mt = [res[j][:R, :R] * strict for j in subs]
    pt = [res[j][:R, R:] * incl for j in subs]
    nqt = [jnp.concatenate([res[j][R:, :R] * strict, res[j][R:, R:] * incl], axis=1) for j in subs]

    def rows_by_head(x):
        return jnp.concatenate([x[:, h * HEAD_DIM:(h + 1) * HEAD_DIM] for h in range(Hg)], axis=0)

    def lanes_by_head(x):
        return jnp.concatenate([x[h * C:(h + 1) * C, :] for h in range(Hg)], axis=1)

    vs = [rows_by_head(part(v, j)).astype(bf16) for j in subs]
    s = [s_scr[bi, :, lanes[j]] for j, (bi, _) in enumerate(chains)]
    base = [_dot(s[j].astype(bf16), lhs[j], _NT) + _dot(vs[j], nqt[j], _TN) for j in subs]

    x = [base[j][:, :R] for j in subs]
    mp = mt
    nsq = C.bit_length() - 1
    for i in range(nsq):
        if i < nsq - 1:
            both = [_dot(jnp.concatenate([x[j].astype(bf16), mp[j]], axis=0), mp[j], _NN) for j in subs]
            x = [x[j] + both[j][:HEAD_DIM] for j in subs]
            mp = [both[j][HEAD_DIM:].astype(bf16) for j in subs]
        else:
            x = [x[j] + _dot(x[j].astype(bf16), mp[j], _NN) for j in subs]
    u = [x[j].astype(bf16) for j in subs]

    y = [lanes_by_head((base[j][:, R:] + _dot(u[j], pt[j], _NN)).T) for j in subs]
    for j, (bi, _) in enumerate(chains):
        gl = g_end[j] * gi[j]
        s_scr[bi, :, lanes[j]] = (g_end[j] * s[j] + _dot(u[j], stack16(kka[j] * gl) * hm, _NN)
                                  + _dot(vs[j], stack16(kf[j] * gl) * hm, _TN))
    y = jnp.concatenate([jnp.concatenate(y[bi * nsub:(bi + 1) * nsub], axis=-1) for bi in range(nseq)],
                        axis=0)

    mean = headsum(y) * (1.0 / HEAD_DIM)
    d = y - mean
    var = headsum(d * d) * (1.0 / HEAD_DIM)
    yn = d * lax.rsqrt(var + GN_EPS) * pv(PV_LNG) + pv(PV_LNB)
    bonus = headsum(r * kf_all * pv(PV_RK)) * v
    o_rwkv = ((yn + bonus) * gate).astype(o_ref.dtype)
    for bi in range(nseq):
        o_ref[bi, :, :RWKV_W] = o_rwkv[bi * C:(bi + 1) * C, :]

    @pl.when(c == nchunks - 1)
    def _():
        for bi in range(nseq):
            for h in range(Hb):
                sout_ref[bi, h] = s_scr[bi, :, h * HEAD_DIM:(h + 1) * HEAD_DIM]


def _mixer(z, z_pool, shift_prev, pool_prev, s0, P, *, start_pos, chunk, heads, nsub, nseq):
    B, T, _ = z.shape
    C, Hg = chunk, heads
    L, R = Hg * HEAD_DIM, Hg * C
    assert nsub * Hg == N_HEADS and B % nseq == 0 and T % C == 0
    nchunks = T // C
    prkv = shift_prev[:, :RKV_W].reshape(B, 3, RWKV_W)
    plora = jnp.pad(shift_prev[:, RKV_W:], ((0, 0), (0, LORA_W - LORA_RANKS))).reshape(B, 1, LORA_W)
    zcol = lambda w, j: pl.BlockSpec((nseq, C, w), lambda b, c: (b, c, j))
    whole = lambda shape: pl.BlockSpec(shape, lambda b, c: (0,) * len(shape))
    per_b = lambda shape: pl.BlockSpec((nseq,) + shape, lambda b, c: (b,) + (0,) * len(shape))
    st = per_b((N_HEADS, HEAD_DIM, HEAD_DIM))
    return pl.pallas_call(
        functools.partial(_mixer_kernel, chunk=C, heads=Hg, nsub=nsub, nseq=nseq, nchunks=nchunks,
                          start_pos=start_pos),
        grid=(B // nseq, nchunks),
        in_specs=[zcol(RWKV_W, 0), zcol(RWKV_W, 1), zcol(RWKV_W, 2), zcol(LORA_W, RKV_W // LORA_W),
                  per_b((3, RWKV_W)), per_b((1, LORA_W)), whole((N_PVEC, RWKV_W)), whole((1, LORA_W)),
                  whole((WD_SPAN[1] - WD_SPAN[0], RWKV_W)), whole((AD_SPAN[1] - AD_SPAN[0], RWKV_W)),
                  whole((GD_SPAN[1] - GD_SPAN[0], RWKV_W)), st,
                  zcol(POOL_GW, 0), zcol(POOL_GW, 1), zcol(POOL_GW, 2), zcol(POOL_GW, 3),
                  per_b((POOL_BUF, POOL_W)), whole((N_POOL, POOL_GW, POOL_GW)), whole((1, POOL_W))],
        out_specs=[pl.BlockSpec((nseq, C, MIX_W), lambda b, c: (b, c, 0)), st],
        out_shape=[jax.ShapeDtypeStruct((B, T, MIX_W), bf16),
                   jax.ShapeDtypeStruct((B, N_HEADS, HEAD_DIM, HEAD_DIM), f32)],
        scratch_shapes=[pltpu.VMEM((nseq, HEAD_DIM, RWKV_W), f32), pltpu.VMEM((R, L), bf16),
                        pltpu.VMEM((2, R, R), bf16), pltpu.VMEM((SUB_L, SUB_L), bf16),
                        pltpu.VMEM((nseq, 4, 8, RWKV_W), f32),
                        pltpu.VMEM((nseq, N_POOL, POOL_CARRY, POOL_GW), f32)],
        compiler_params=pltpu.CompilerParams(
            dimension_semantics=("parallel", "arbitrary"), vmem_limit_bytes=VMEM_LIMIT),
        name="mixer",
    )(z, z, z, z, prkv, plora, P["pvec"], P["mu_lora"], P["w_decay_up"], P["w_a_up"], P["w_g_up"], s0,
      z_pool, z_pool, z_pool, z_pool, pool_prev, P["w_pool"], P["pool_scale"])


def _xattn_kernel(q_ref, k_ref, v_ref, o_ref, *, head_axis):
    q = q_ref[0]
    outs = []
    for h in range(XA_HEADS):
        sl = slice(h * XA_HEAD_DIM, (h + 1) * XA_HEAD_DIM)
        if head_axis:
            kh, vh = k_ref[0, :, h, :], v_ref[0, :, h, :]
        else:
            kh, vh = k_ref[0, :, sl], v_ref[0, :, sl]
        s = _dot(q[:, sl], kh.astype(bf16), _NT) * (XA_HEAD_DIM ** -0.5)
        e = jnp.exp(s - jnp.max(s, axis=-1, keepdims=True))
        v1 = jnp.concatenate([vh.astype(bf16), jnp.ones((MEM_TOKENS, XA_HEAD_DIM), bf16)], axis=1)
        ov = _dot(e.astype(bf16), v1, _NN)
        outs.append(ov[:, :XA_HEAD_DIM] / ov[:, XA_HEAD_DIM:])
    o_ref[0] = jnp.concatenate(outs, axis=-1).astype(o_ref.dtype)


def _xattn(q, mk, mv, tq):
    B, T, _ = q.shape
    tq = min(tq, T)
    head_axis = mk.ndim == 4
    if head_axis:
        kv_spec = pl.BlockSpec((1, MEM_TOKENS, XA_HEADS, XA_HEAD_DIM), lambda b, i: (b, 0, 0, 0))
    else:
        kv_spec = pl.BlockSpec((1, MEM_TOKENS, XA_W), lambda b, i: (b, 0, 0))
    return pl.pallas_call(
        functools.partial(_xattn_kernel, head_axis=head_axis),
        grid=(B, T // tq),
        in_specs=[pl.BlockSpec((1, tq, XA_W), lambda b, i: (b, i, 0)), kv_spec, kv_spec],
        out_specs=pl.BlockSpec((1, tq, XA_W), lambda b, i: (b, i, 0)),
        out_shape=jax.ShapeDtypeStruct((B, T, XA_W), bf16),
        compiler_params=pltpu.CompilerParams(
            dimension_semantics=("parallel", "parallel"), vmem_limit_bytes=VMEM_LIMIT),
        name="xattn",
    )(q, mk, mv)


def _block(x, mk, mv, shift_prev, pool_prev, wkv_prev, start_pos, P, chunk, nseq):
    B, T, _ = x.shape
    M = B * T
    x2 = x.reshape(M, D_MODEL)

    h = _rmsnorm(x2, P["norm_mix_g"], bf16)
    w_in_t = P["w_in_t"]
    z = _matmul(h, None, bt=w_in_t, n=RKV_W + LORA_W, tn=512, tk=D_MODEL).reshape(B, T, RKV_W + LORA_W)
    z_pool = _matmul(h, None, bt=w_in_t, col0=SHIFT_W, n=POOL_W, tn=512, tk=D_MODEL).reshape(B, T, POOL_W)
    mix, s_new = _mixer(z, z_pool, shift_prev, pool_prev, wkv_prev, P, start_pos=start_pos, chunk=chunk,
                        heads=SUB_HEADS, nsub=N_HEADS // SUB_HEADS, nseq=nseq)
    x2 = _matmul(mix.reshape(M, MIX_W), P["w_out"], res=x2, tn=512, tk=MIX_W)
    new_shift = z[:, -1, :SHIFT_W]
    new_pool = jnp.concatenate([pool_prev, z_pool], axis=1)[:, -POOL_BUF:]

    h = _rmsnorm(x2, P["norm_xa_g"], bf16)
    q = _matmul(h, P["w_xq"], out_dtype=bf16, tn=512, tk=D_MODEL).reshape(B, T, XA_W)
    o = _xattn(q, mk, mv, tq=1024)
    x2 = _matmul(o.reshape(M, XA_W), P["w_xo"], res=x2, tk=512)

    h = _rmsnorm(x2, P["norm_ffn_g"], bf16)
    hid = _matmul(h, P["w_up"], act="relu2", out_dtype=bf16, tn=512, tk=D_MODEL)
    x2 = _matmul(hid, P["w_down"], res=x2, tk=2048)
    y_out = _rmsnorm(x2, P["norm_final_g"], f32).reshape(B, T, D_MODEL)

    return y_out, new_shift, new_pool, s_new


def _params(norm_mix_g, w_in, mu_shift, w0, w_decay_up, a0, w_a_up, w_g_up, k_k, k_a, r_k, lnx_g, lnx_b, w_pool,
            pool_scale, w_out, norm_xa_g, norm_mem_g, w_xq, w_mk, w_mv, w_xo, norm_ffn_g, w_up, w_down,
            norm_final_g):
    l = 0
    mu = mu_shift[l]
    pvec = jnp.stack([mu[:RWKV_W], mu[RWKV_W:2 * RWKV_W], mu[2 * RWKV_W:RKV_W], w0[l], a0[l], k_k[l],
                      k_a[l], r_k[l].reshape(RWKV_W), lnx_g[l], lnx_b[l]])
    pvec = jnp.concatenate([pvec, jnp.zeros((N_PVEC - pvec.shape[0], RWKV_W), f32)], axis=0)

    def lora_rows(w, first, span):
        return jnp.pad(w, ((first - span[0], span[1] - first - w.shape[0]), (0, 0))).astype(bf16)

    return {
        "norm_mix_g": norm_mix_g[l], "w_in_t": jnp.transpose(w_in[l]),
        "pvec": pvec, "mu_lora": jnp.pad(mu[RKV_W:], (0, LORA_W - LORA_RANKS)).reshape(1, LORA_W),
        "w_decay_up": lora_rows(w_decay_up[l], 0, WD_SPAN), "w_a_up": lora_rows(w_a_up[l], DECAY_RANK, AD_SPAN),
        "w_g_up": lora_rows(w_g_up[l], DECAY_RANK + AAA_RANK, GD_SPAN),
        "w_pool": w_pool[l].astype(bf16), "pool_scale": pool_scale[l].reshape(1, POOL_W), "w_out": w_out[l],
        "norm_xa_g": norm_xa_g[l], "w_xq": w_xq[l], "w_xo": w_xo[l],
        "norm_ffn_g": norm_ffn_g[l], "w_up": w_up[l], "w_down": w_down[l],
        "norm_final_g": norm_final_g, "norm_mem_g": norm_mem_g[l], "w_mk": w_mk[l], "w_mv": w_mv[l],
    }


def kernel(x_prompt, x_sample, mem_prompt, state_wkv, state_shift, state_pool, cache_mem_k, cache_mem_v, norm_mix_g, w_in, mu_shift, w0, w_decay_up, a0, w_a_up, w_g_up, k_k, k_a, r_k, lnx_g, lnx_b, w_pool, pool_scale, w_out, norm_xa_g, norm_mem_g, w_xq, w_mk, w_mv, w_xo, norm_ffn_g, w_up, w_down, norm_final_g):
    Bp = x_prompt.shape[0]
    P = _params(norm_mix_g, w_in, mu_shift, w0, w_decay_up, a0, w_a_up, w_g_up, k_k, k_a, r_k, lnx_g, lnx_b, w_pool,
                pool_scale, w_out, norm_xa_g, norm_mem_g, w_xq, w_mk, w_mv, w_xo, norm_ffn_g, w_up, w_down,
                norm_final_g)
    mem = _rmsnorm(mem_prompt.reshape(Bp * MEM_TOKENS, D_MODEL), P["norm_mem_g"], bf16)
    mk3 = _matmul(mem, P["w_mk"], tn=512, tk=D_MODEL).reshape(Bp, MEM_TOKENS, XA_W)
    mv3 = _matmul(mem, P["w_mv"], tn=512, tk=D_MODEL).reshape(Bp, MEM_TOKENS, XA_W)
    mk_p = mk3.reshape(Bp, MEM_TOKENS, XA_HEADS, XA_HEAD_DIM)
    mv_p = mv3.reshape(Bp, MEM_TOKENS, XA_HEADS, XA_HEAD_DIM)

    yp, sh_p, pl_p, wkv_p = _block(
        x_prompt, mk3, mv3, jnp.zeros((Bp, SHIFT_W), f32), jnp.zeros((Bp, POOL_BUF, POOL_W), f32),
        jnp.zeros((Bp, N_HEADS, HEAD_DIM, HEAD_DIM), f32), 0, P, chunk=64, nseq=1)
    ys, sh_s, pl_s, wkv_s = _block(
        x_sample, cache_mem_k.reshape(cache_mem_k.shape[1:]), cache_mem_v.reshape(cache_mem_v.shape[1:]),
        state_shift.reshape(state_shift.shape[1:]), state_pool.reshape(state_pool.shape[1:]),
        state_wkv.reshape(state_wkv.shape[1:]), PAST_LEN, P, chunk=x_sample.shape[1], nseq=4)
    ex = lambda t: t[None]
    return (yp, ys, ex(wkv_p), ex(sh_p), ex(pl_p), ex(mk_p), ex(mv_p), ex(wkv_s), ex(sh_s), ex(pl_s))
```

```python
import functools

import jax
import jax.numpy as jnp
from jax import lax
from jax.experimental import pallas as pl
from jax.experimental.pallas import tpu as pltpu

D_MODEL = 4096
RWKV_W = 2048
POOL_W = 2048
MIX_W = RWKV_W + POOL_W
HEAD_DIM = 64
N_HEADS = RWKV_W // HEAD_DIM
DECAY_RANK = 96
AAA_RANK = 96
GATE_RANK = 256
SHIFT_W = 3 * RWKV_W + DECAY_RANK + AAA_RANK + GATE_RANK
POOL_WINDOWS = (2, 4, 8, 16)
N_POOL = len(POOL_WINDOWS)
POOL_GW = POOL_W // N_POOL
POOL_BUF = max(POOL_WINDOWS) - 1
POOL_CARRY = POOL_BUF + 1
MEM_TOKENS = 256
XA_HEADS = 4
XA_HEAD_DIM = 128
XA_W = XA_HEADS * XA_HEAD_DIM
PAST_LEN = 16384
RMS_EPS = 1e-6
GN_EPS = 64e-5

RKV_W = 3 * RWKV_W
LORA_RANKS = DECAY_RANK + AAA_RANK + GATE_RANK
LORA_W = 512
LANE_TILE = 128


def _tile_span(first, width):
    return (first // LANE_TILE * LANE_TILE, -(-(first + width) // LANE_TILE) * LANE_TILE)


WD_SPAN = _tile_span(0, DECAY_RANK)
AD_SPAN = _tile_span(DECAY_RANK, AAA_RANK)
GD_SPAN = _tile_span(DECAY_RANK + AAA_RANK, GATE_RANK)
SUB_HEADS = 4
SUB_L = SUB_HEADS * HEAD_DIM
N_PVEC = 16

VMEM_LIMIT = 56 * 1024 * 1024

_NN = ((1,), (0,))
_NT = ((1,), (1,))
_TN = ((0,), (0,))

f32 = jnp.float32
bf16 = jnp.bfloat16


def _dot(a, b, dims):
    return lax.dot_general(a, b, (dims, ((), ())), preferred_element_type=f32)


def _sigmoid(x):
    return 1.0 / (1.0 + jnp.exp(-x))


def _rms_scale(x, g):
    return x * lax.rsqrt(jnp.mean(x * x, axis=-1, keepdims=True) + RMS_EPS) * g


def _rmsnorm_kernel(x_ref, g_ref, o_ref):
    o_ref[...] = _rms_scale(x_ref[...], g_ref[...]).astype(o_ref.dtype)


def _rmsnorm(x, g, out_dtype, tm=512):
    m, d = x.shape
    return pl.pallas_call(
        _rmsnorm_kernel,
        grid=(m // tm,),
        in_specs=[pl.BlockSpec((tm, d), lambda i: (i, 0)), pl.BlockSpec((1, d), lambda i: (0, 0))],
        out_specs=pl.BlockSpec((tm, d), lambda i: (i, 0)),
        out_shape=jax.ShapeDtypeStruct((m, d), out_dtype),
        compiler_params=pltpu.CompilerParams(dimension_semantics=("parallel",), vmem_limit_bytes=VMEM_LIMIT),
        name="rmsnorm",
    )(x, g.reshape(1, d))


def _matmul_kernel(a_ref, b_ref, *rest, nk, act, has_norm, has_res, b_rows_are_outputs):
    rest = list(rest)
    g_ref = rest.pop(0) if has_norm else None
    res_ref = rest.pop(0) if has_res else None

    def finish(r):
        if act == "relu2":
            r = jnp.square(jnp.maximum(r, 0.0))
        if has_res:
            r = r + res_ref[...]
        return r

    def prod():
        lhs = _rms_scale(a_ref[...], g_ref[...]).astype(bf16) if has_norm else a_ref[...]
        return _dot(lhs, b_ref[...].astype(bf16), _NT if b_rows_are_outputs else _NN)

    if nk == 1:
        (o_ref,) = rest
        o_ref[...] = finish(prod()).astype(o_ref.dtype)
        return

    o_ref, acc_ref = rest
    k = pl.program_id(2)

    @pl.when(k == 0)
    def _():
        acc_ref[...] = jnp.zeros_like(acc_ref)

    acc_ref[...] += prod()

    @pl.when(k == nk - 1)
    def _():
        o_ref[...] = finish(acc_ref[...]).astype(o_ref.dtype)


def _matmul(a, b, res=None, act=None, out_dtype=f32, bt=None, norm_g=None, col0=0, n=None,
            tm=1024, tn=1024, tk=1024):
    m, kd = a.shape
    kw, ntot = b.shape if bt is None else bt.shape[::-1]
    n = ntot - col0 if n is None else n
    tm, tn, tk = min(tm, m), min(tn, n), min(tk, kd)
    assert kw == kd and m % tm == 0 and n % tn == 0 and kd % tk == 0, (a.shape, kw, ntot, col0, n, tm, tn, tk)
    nk = kd // tk
    in_specs = [pl.BlockSpec((tm, tk), lambda i, j, k: (i, k))]
    args = [a]
    if bt is None:
        assert col0 % tn == 0
        in_specs.append(pl.BlockSpec((tk, tn), lambda i, j, k: (k, col0 // tn + j)))
        args.append(b)
    else:
        assert col0 % 8 == 0 and tn % 8 == 0
        in_specs.append(pl.BlockSpec((pl.Element(tn), pl.Element(tk)),
                                     lambda i, j, k: (8 * (col0 // 8 + j * (tn // 8)), k * tk)))
        args.append(bt)
    if norm_g is not None:
        assert nk == 1 and n == tn
        in_specs.append(pl.BlockSpec((1, kd), lambda i, j, k: (0, 0)))
        args.append(norm_g.reshape(1, kd))
    if res is not None:
        in_specs.append(pl.BlockSpec((tm, tn), lambda i, j, k: (i, j)))
        args.append(res)
    return pl.pallas_call(
        functools.partial(_matmul_kernel, nk=nk, act=act, has_norm=norm_g is not None, has_res=res is not None,
                          b_rows_are_outputs=bt is not None),
        grid=(m // tm, n // tn, nk),
        in_specs=in_specs,
        out_specs=pl.BlockSpec((tm, tn), lambda i, j, k: (i, j)),
        out_shape=jax.ShapeDtypeStruct((m, n), out_dtype),
        scratch_shapes=[pltpu.VMEM((tm, tn), f32)] if nk > 1 else [],
        compiler_params=pltpu.CompilerParams(
            dimension_semantics=("parallel", "parallel", "arbitrary"), vmem_limit_bytes=VMEM_LIMIT),
        name="matmul",
    )(*args)


PV_MU_R, PV_MU_K, PV_MU_V, PV_W0, PV_A0, PV_KK, PV_KA, PV_RK, PV_LNG, PV_LNB = range(10)


def _pool_group(x, carry_ref, w, scale, pos1, wdw, C):
    ext = jnp.concatenate([carry_ref[...], x], axis=0)
    carry_ref[...] = ext[C:, :]
    acc, span = ext, 1
    while span < wdw:
        acc = acc + pltpu.roll(acc, span, 0)
        span *= 2
    mean = acc[POOL_CARRY:, :] / jnp.minimum(wdw, pos1).astype(f32)
    return _dot((mean - x).astype(bf16), w, _NN) * scale


def _mixer_kernel(zr_ref, zk_ref, zv_ref, zl_ref, prkv_ref, pl_ref, pv_ref, mul_ref, wdu_ref, wau_ref, wgu_ref,
                  s0_ref, zp0_ref, zp1_ref, zp2_ref, zp3_ref, pprev_ref, wp_ref, psc_ref,
                  o_ref, sout_ref, s_scr, hm_scr, tm_scr, ones_scr, carry_scr, pcarry_scr,
                  *, chunk, heads, nsub, nseq, nchunks, start_pos):
    C, Hg = chunk, heads
    L, R = Hg * HEAD_DIM, Hg * C
    Lb, Hb = nsub * L, nsub * Hg
    rows = nseq * C
    c = pl.program_id(1)

    @pl.when(c == 0)
    def _():
        for bi in range(nseq):
            for h in range(Hb):
                s_scr[bi, :, h * HEAD_DIM:(h + 1) * HEAD_DIM] = s0_ref[bi, h]
            for i in range(3):
                carry_scr[bi, i, 0:1, :Lb] = prkv_ref[bi, i:i + 1, :]
            carry_scr[bi, 3, 0:1, :LORA_W] = pl_ref[bi]
            for gi in range(N_POOL):
                pcarry_scr[bi, gi, 0:1, :] = jnp.zeros((1, POOL_GW), f32)
                pcarry_scr[bi, gi, 1:POOL_CARRY, :] = pprev_ref[bi, :, gi * POOL_GW:(gi + 1) * POOL_GW]
        hm_scr[...] = ((lax.broadcasted_iota(jnp.int32, (R, L), 0) // C)
                       == (lax.broadcasted_iota(jnp.int32, (R, L), 1) // HEAD_DIM)).astype(bf16)
        row = lax.broadcasted_iota(jnp.int32, (R, R), 0)
        col = lax.broadcasted_iota(jnp.int32, (R, R), 1)
        same = (row // C) == (col // C)
        tm_scr[0] = (same & ((row % C) < (col % C))).astype(bf16)
        tm_scr[1] = (same & ((row % C) <= (col % C))).astype(bf16)
        ones_scr[...] = ((lax.broadcasted_iota(jnp.int32, (SUB_L, SUB_L), 0) // HEAD_DIM)
                         == (lax.broadcasted_iota(jnp.int32, (SUB_L, SUB_L), 1) // HEAD_DIM)).astype(bf16)

    def mixed(z_ref, idx, width, mu):
        z = z_ref[...].reshape(rows, width)
        first = jnp.concatenate([jnp.broadcast_to(carry_scr[bi, idx, 0:1, :width], (C, width))
                                 for bi in range(nseq)], axis=0)
        prev = jnp.where(lax.broadcasted_iota(jnp.int32, z.shape, 0) % C == 0, first, pltpu.roll(z, 1, 0))
        for bi in range(nseq):
            carry_scr[bi, idx, 0:1, :width] = z[(bi + 1) * C - 1:(bi + 1) * C, :]
        return z + mu * (prev - z)

    pv = lambda i: pv_ref[i:i + 1, :]
    r = mixed(zr_ref, 0, Lb, pv(PV_MU_R))
    k = mixed(zk_ref, 1, Lb, pv(PV_MU_K))
    v = mixed(zv_ref, 2, Lb, pv(PV_MU_V))
    zl = mixed(zl_ref, 3, LORA_W, mul_ref[...])

    pos1 = start_pos + 1 + c * C + lax.broadcasted_iota(jnp.int32, (C, POOL_GW), 0)
    for bi in range(nseq):
        for gi, (zp_ref, wdw) in enumerate(zip((zp0_ref, zp1_ref, zp2_ref, zp3_ref), POOL_WINDOWS)):
            cols = slice(gi * POOL_GW, (gi + 1) * POOL_GW)
            o_pool = _pool_group(zp_ref[bi], pcarry_scr.at[bi, gi], wp_ref[gi], psc_ref[:, cols], pos1, wdw, C)
            o_ref[bi, :, RWKV_W + gi * POOL_GW:RWKV_W + (gi + 1) * POOL_GW] = o_pool.astype(o_ref.dtype)

    wd_in = jnp.tanh(zl[:, WD_SPAN[0]:WD_SPAN[1]]).astype(bf16)
    dec = pv(PV_W0) + _dot(wd_in, wdu_ref[...], _NN)
    sp = jnp.maximum(-dec, 0.0) + jnp.log(1.0 + jnp.exp(-jnp.abs(dec)))
    lw_all = -jnp.exp(-sp - 0.5)
    a = _sigmoid(pv(PV_A0) + _dot(zl[:, AD_SPAN[0]:AD_SPAN[1]].astype(bf16), wau_ref[...], _NN))
    gate = _dot(_sigmoid(zl[:, GD_SPAN[0]:GD_SPAN[1]]).astype(bf16), wgu_ref[...], _NN)

    ones = ones_scr[...]

    def headsum(x):
        npc = Lb // SUB_L
        xs = jnp.concatenate([x[:, p * SUB_L:(p + 1) * SUB_L] for p in range(npc)], axis=0)
        sums = _dot(xs.astype(bf16), ones, _NN)
        return jnp.concatenate([sums[p * rows:(p + 1) * rows, :] for p in range(npc)], axis=1)

    kkr = k * pv(PV_KK)
    kk_all = kkr * lax.rsqrt(jnp.maximum(headsum(kkr * kkr), 1e-24))
    kf_all = k * (1.0 + (a - 1.0) * pv(PV_KA))
    kka_all = kk_all * a

    tri = (lax.broadcasted_iota(jnp.int32, (C, C), 1) <= lax.broadcasted_iota(jnp.int32, (C, C), 0)).astype(bf16)
    hm = hm_scr[...]
    strict, incl = tm_scr[0], tm_scr[1]

    def stack16(x):
        if C % 16 == 0:
            return jnp.concatenate([x.astype(bf16)] * Hg, axis=0)
        return jnp.concatenate([x] * Hg, axis=0).astype(bf16)

    chains = [(bi, jg) for bi in range(nseq) for jg in range(nsub)]
    subs = range(len(chains))
    seq_rows = [slice(bi * C, (bi + 1) * C) for bi, _ in chains]
    lanes = [slice(jg * L, (jg + 1) * L) for _, jg in chains]
    part = lambda x, j: x[seq_rows[j], lanes[j]]

    def cumsum_t(lw):
        h1 = lw.astype(bf16)
        r1 = lw - h1.astype(f32)
        h2 = r1.astype(bf16)
        h3 = (r1 - h2.astype(f32)).astype(bf16)
        return _dot(tri, h1, _NN) + (_dot(tri, h2, _NN) + _dot(tri, h3, _NN))

    lw = [part(lw_all, j) for j in subs]
    cum = [cumsum_t(lw[j]) for j in subs]
    g = [jnp.exp(cum[j]) for j in subs]
    gi = [jnp.exp(-cum[j]) for j in subs]
    gprev = [jnp.exp(cum[j] - lw[j]) for j in subs]
    g_end = [g[j][C - 1:C, :] for j in subs]
    kk = [part(kk_all, j) for j in subs]
    kka = [part(kka_all, j) for j in subs]
    kf = [part(kf_all, j) for j in subs]
    lhs = [jnp.concatenate([stack16(-kk[j] * gprev[j]) * hm, stack16(part(r, j) * g[j]) * hm], axis=0)
           for j in subs]
    rhs = [jnp.concatenate([stack16(kka[j] * gi[j]), stack16(kf[j] * gi[j])], axis=0) for j in subs]
    res = [_dot(rhs[j], lhs[j], _NT).astype(bf16) for j in subs]
    mt = [res[j][:R, :R] * strict for j in subs]
    pt = [res[j][:R, R:] * incl for j in subs]
    nqt = [jnp.concatenate([res[j][R:, :R] * strict, res[j][R:, R:] * incl], axis=1) for j in subs]

    def rows_by_head(x):
        return jnp.concatenate([x[:, h * HEAD_DIM:(h + 1) * HEAD_DIM] for h in range(Hg)], axis=0)

    def lanes_by_head(x):
        return jnp.concatenate([x[h * C:(h + 1) * C, :] for h in range(Hg)], axis=1)

    vs = [rows_by_head(part(v, j)).astype(bf16) for j in subs]
    s = [s_scr[bi, :, lanes[j]] for j, (bi, _) in enumerate(chains)]
    base = [_dot(s[j].astype(bf16), lhs[j], _NT) + _dot(vs[j], nqt[j], _TN) for j in subs]

    x = [base[j][:, :R] for j in subs]
    mp = mt
    nsq = C.bit_length() - 1
    for i in range(nsq):
        if i < nsq - 1:
            both = [_dot(jnp.concatenate([x[j].astype(bf16), mp[j]], axis=0), mp[j], _NN) for j in subs]
            x = [x[j] + both[j][:HEAD_DIM] for j in subs]
            mp = [both[j][HEAD_DIM:].astype(bf16) for j in subs]
        else:
            x = [x[j] + _dot(x[j].astype(bf16), mp[j], _NN) for j in subs]
    u = [x[j].astype(bf16) for j in subs]

    y = [lanes_by_head((base[j][:, R:] + _dot(u[j], pt[j], _NN)).T) for j in subs]
    for j, (bi, _) in enumerate(chains):
        gl = g_end[j] * gi[j]
        s_scr[bi, :, lanes[j]] = (g_end[j] * s[j] + _dot(u[j], stack16(kka[j] * gl) * hm, _NN)
                                  + _dot(vs[j], stack16(kf[j] * gl) * hm, _TN))
    y = jnp.concatenate([jnp.concatenate(y[bi * nsub:(bi + 1) * nsub], axis=-1) for bi in range(nseq)],
                        axis=0)

    mean = headsum(y) * (1.0 / HEAD_DIM)
    d = y - mean
    var = headsum(d * d) * (1.0 / HEAD_DIM)
    yn = d * lax.rsqrt(var + GN_EPS) * pv(PV_LNG) + pv(PV_LNB)
    bonus = headsum(r * kf_all * pv(PV_RK)) * v
    o_rwkv = ((yn + bonus) * gate).astype(o_ref.dtype)
    for bi in range(nseq):
        o_ref[bi, :, :RWKV_W] = o_rwkv[bi * C:(bi + 1) * C, :]

    @pl.when(c == nchunks - 1)
    def _():
        for bi in range(nseq):
            for h in range(Hb):
                sout_ref[bi, h] = s_scr[bi, :, h * HEAD_DIM:(h + 1) * HEAD_DIM]


def _mixer(z, z_pool, shift_prev, pool_prev, s0, P, *, start_pos, chunk, heads, nsub, nseq):
    B, T, _ = z.shape
    C, Hg = chunk, heads
    L, R = Hg * HEAD_DIM, Hg * C
    assert nsub * Hg == N_HEADS and B % nseq == 0 and T % C == 0
    nchunks = T // C
    prkv = shift_prev[:, :RKV_W].reshape(B, 3, RWKV_W)
    plora = jnp.pad(shift_prev[:, RKV_W:], ((0, 0), (0, LORA_W - LORA_RANKS))).reshape(B, 1, LORA_W)
    zcol = lambda w, j: pl.BlockSpec((nseq, C, w), lambda b, c: (b, c, j))
    whole = lambda shape: pl.BlockSpec(shape, lambda b, c: (0,) * len(shape))
    per_b = lambda shape: pl.BlockSpec((nseq,) + shape, lambda b, c: (b,) + (0,) * len(shape))
    st = per_b((N_HEADS, HEAD_DIM, HEAD_DIM))
    return pl.pallas_call(
        functools.partial(_mixer_kernel, chunk=C, heads=Hg, nsub=nsub, nseq=nseq, nchunks=nchunks,
                          start_pos=start_pos),
        grid=(B // nseq, nchunks),
        in_specs=[zcol(RWKV_W, 0), zcol(RWKV_W, 1), zcol(RWKV_W, 2), zcol(LORA_W, RKV_W // LORA_W),
                  per_b((3, RWKV_W)), per_b((1, LORA_W)), whole((N_PVEC, RWKV_W)), whole((1, LORA_W)),
                  whole((WD_SPAN[1] - WD_SPAN[0], RWKV_W)), whole((AD_SPAN[1] - AD_SPAN[0], RWKV_W)),
                  whole((GD_SPAN[1] - GD_SPAN[0], RWKV_W)), st,
                  zcol(POOL_GW, 0), zcol(POOL_GW, 1), zcol(POOL_GW, 2), zcol(POOL_GW, 3),
                  per_b((POOL_BUF, POOL_W)), whole((N_POOL, POOL_GW, POOL_GW)), whole((1, POOL_W))],
        out_specs=[pl.BlockSpec((nseq, C, MIX_W), lambda b, c: (b, c, 0)), st],
        out_shape=[jax.ShapeDtypeStruct((B, T, MIX_W), bf16),
                   jax.ShapeDtypeStruct((B, N_HEADS, HEAD_DIM, HEAD_DIM), f32)],
        scratch_shapes=[pltpu.VMEM((nseq, HEAD_DIM, RWKV_W), f32), pltpu.VMEM((R, L), bf16),
                        pltpu.VMEM((2, R, R), bf16), pltpu.VMEM((SUB_L, SUB_L), bf16),
                        pltpu.VMEM((nseq, 4, 8, RWKV_W), f32),
                        pltpu.VMEM((nseq, N_POOL, POOL_CARRY, POOL_GW), f32)],
        compiler_params=pltpu.CompilerParams(
            dimension_semantics=("parallel", "arbitrary"), vmem_limit_bytes=VMEM_LIMIT),
        name="mixer",
    )(z, z, z, z, prkv, plora, P["pvec"], P["mu_lora"], P["w_decay_up"], P["w_a_up"], P["w_g_up"], s0,
      z_pool, z_pool, z_pool, z_pool, pool_prev, P["w_pool"], P["pool_scale"])


def _xattn_kernel(q_ref, k_ref, v_ref, o_ref, *, head_axis):
    q = q_ref[0]
    outs = []
    for h in range(XA_HEADS):
        sl = slice(h * XA_HEAD_DIM, (h + 1) * XA_HEAD_DIM)
        if head_axis:
            kh, vh = k_ref[0, :, h, :], v_ref[0, :, h, :]
        else:
            kh, vh = k_ref[0, :, sl], v_ref[0, :, sl]
        s = _dot(q[:, sl], kh.astype(bf16), _NT) * (XA_HEAD_DIM ** -0.5)
        e = jnp.exp(s - jnp.max(s, axis=-1, keepdims=True))
        v1 = jnp.concatenate([vh.astype(bf16), jnp.ones((MEM_TOKENS, XA_HEAD_DIM), bf16)], axis=1)
        ov = _dot(e.astype(bf16), v1, _NN)
        outs.append(ov[:, :XA_HEAD_DIM] / ov[:, XA_HEAD_DIM:])
    o_ref[0] = jnp.concatenate(outs, axis=-1).astype(o_ref.dtype)


def _xattn(q, mk, mv, tq):
    B, T, _ = q.shape
    tq = min(tq, T)
    head_axis = mk.ndim == 4
    if head_axis:
        kv_spec = pl.BlockSpec((1, MEM_TOKENS, XA_HEADS, XA_HEAD_DIM), lambda b, i: (b, 0, 0, 0))
    else:
        kv_spec = pl.BlockSpec((1, MEM_TOKENS, XA_W), lambda b, i: (b, 0, 0))
    return pl.pallas_call(
        functools.partial(_xattn_kernel, head_axis=head_axis),
        grid=(B, T // tq),
        in_specs=[pl.BlockSpec((1, tq, XA_W), lambda b, i: (b, i, 0)), kv_spec, kv_spec],
        out_specs=pl.BlockSpec((1, tq, XA_W), lambda b, i: (b, i, 0)),
        out_shape=jax.ShapeDtypeStruct((B, T, XA_W), bf16),
        compiler_params=pltpu.CompilerParams(
            dimension_semantics=("parallel", "parallel"), vmem_limit_bytes=VMEM_LIMIT),
        name="xattn",
    )(q, mk, mv)


def _block(x, mk, mv, shift_prev, pool_prev, wkv_prev, start_pos, P, chunk, nseq):
    B, T, _ = x.shape
    M = B * T
    x2 = x.reshape(M, D_MODEL)

    h = _rmsnorm(x2, P["norm_mix_g"], bf16)
    w_in_t = P["w_in_t"]
    z = _matmul(h, None, bt=w_in_t, n=RKV_W + LORA_W, tn=512, tk=D_MODEL).reshape(B, T, RKV_W + LORA_W)
    z_pool = _matmul(h, None, bt=w_in_t, col0=SHIFT_W, n=POOL_W, tn=512, tk=D_MODEL).reshape(B, T, POOL_W)
    mix, s_new = _mixer(z, z_pool, shift_prev, pool_prev, wkv_prev, P, start_pos=start_pos, chunk=chunk,
                        heads=SUB_HEADS, nsub=N_HEADS // SUB_HEADS, nseq=nseq)
    x2 = _matmul(mix.reshape(M, MIX_W), P["w_out"], res=x2, tn=512, tk=MIX_W)
    new_shift = z[:, -1, :SHIFT_W]
    new_pool = jnp.concatenate([pool_prev, z_pool], axis=1)[:, -POOL_BUF:]

    q = _matmul(x2, P["w_xq"], norm_g=P["norm_xa_g"], out_dtype=bf16, tm=512, tn=XA_W, tk=D_MODEL)
    o = _xattn(q.reshape(B, T, XA_W), mk, mv, tq=512)
    x2 = _matmul(o.reshape(M, XA_W), P["w_xo"], res=x2, tk=512)

    h = _rmsnorm(x2, P["norm_ffn_g"], bf16)
    hid = _matmul(h, P["w_up"], act="relu2", out_dtype=bf16, tn=512, tk=D_MODEL)
    x2 = _matmul(hid, P["w_down"], res=x2, tk=2048)
    y_out = _rmsnorm(x2, P["norm_final_g"], f32).reshape(B, T, D_MODEL)

    return y_out, new_shift, new_pool, s_new


def _params(norm_mix_g, w_in, mu_shift, w0, w_decay_up, a0, w_a_up, w_g_up, k_k, k_a, r_k, lnx_g, lnx_b, w_pool,
            pool_scale, w_out, norm_xa_g, norm_mem_g, w_xq, w_mk, w_mv, w_xo, norm_ffn_g, w_up, w_down,
            norm_final_g):
    l = 0
    mu = mu_shift[l]
    pvec = jnp.stack([mu[:RWKV_W], mu[RWKV_W:2 * RWKV_W], mu[2 * RWKV_W:RKV_W], w0[l], a0[l], k_k[l],
                      k_a[l], r_k[l].reshape(RWKV_W), lnx_g[l], lnx_b[l]])
    pvec = jnp.concatenate([pvec, jnp.zeros((N_PVEC - pvec.shape[0], RWKV_W), f32)], axis=0)

    def lora_rows(w, first, span):
        return jnp.pad(w, ((first - span[0], span[1] - first - w.shape[0]), (0, 0))).astype(bf16)

    return {
        "norm_mix_g": norm_mix_g[l], "w_in_t": jnp.transpose(w_in[l]),
        "pvec": pvec, "mu_lora": jnp.pad(mu[RKV_W:], (0, LORA_W - LORA_RANKS)).reshape(1, LORA_W),
        "w_decay_up": lora_rows(w_decay_up[l], 0, WD_SPAN), "w_a_up": lora_rows(w_a_up[l], DECAY_RANK, AD_SPAN),
        "w_g_up": lora_rows(w_g_up[l], DECAY_RANK + AAA_RANK, GD_SPAN),
        "w_pool": w_pool[l].astype(bf16), "pool_scale": pool_scale[l].reshape(1, POOL_W), "w_out": w_out[l],
        "norm_xa_g": norm_xa_g[l], "w_xq": w_xq[l], "w_xo": w_xo[l],
        "norm_ffn_g": norm_ffn_g[l], "w_up": w_up[l], "w_down": w_down[l],
        "norm_final_g": norm_final_g, "norm_mem_g": norm_mem_g[l], "w_mk": w_mk[l], "w_mv": w_mv[l],
    }


def kernel(x_prompt, x_sample, mem_prompt, state_wkv, state_shift, state_pool, cache_mem_k, cache_mem_v, norm_mix_g, w_in, mu_shift, w0, w_decay_up, a0, w_a_up, w_g_up, k_k, k_a, r_k, lnx_g, lnx_b, w_pool, pool_scale, w_out, norm_xa_g, norm_mem_g, w_xq, w_mk, w_mv, w_xo, norm_ffn_g, w_up, w_down, norm_final_g):
    Bp = x_prompt.shape[0]
    P = _params(norm_mix_g, w_in, mu_shift, w0, w_decay_up, a0, w_a_up, w_g_up, k_k, k_a, r_k, lnx_g, lnx_b, w_pool,
                pool_scale, w_out, norm_xa_g, norm_mem_g, w_xq, w_mk, w_mv, w_xo, norm_ffn_g, w_up, w_down,
                norm_final_g)
    mem = _rmsnorm(mem_prompt.reshape(Bp * MEM_TOKENS, D_MODEL), P["norm_mem_g"], bf16)
    mk3 = _matmul(mem, P["w_mk"], tn=512, tk=D_MODEL).reshape(Bp, MEM_TOKENS, XA_W)
    mv3 = _matmul(mem, P["w_mv"], tn=512, tk=D_MODEL).reshape(Bp, MEM_TOKENS, XA_W)
    mk_p = mk3.reshape(Bp, MEM_TOKENS, XA_HEADS, XA_HEAD_DIM)
    mv_p = mv3.reshape(Bp, MEM_TOKENS, XA_HEADS, XA_HEAD_DIM)

    yp, sh_p, pl_p, wkv_p = _block(
        x_prompt, mk3, mv3, jnp.zeros((Bp, SHIFT_W), f32), jnp.zeros((Bp, POOL_BUF, POOL_W), f32),
        jnp.zeros((Bp, N_HEADS, HEAD_DIM, HEAD_DIM), f32), 0, P, chunk=64, nseq=1)
    ys, sh_s, pl_s, wkv_s = _block(
        x_sample, cache_mem_k.reshape(cache_mem_k.shape[1:]), cache_mem_v.reshape(cache_mem_v.shape[1:]),
        state_shift.reshape(state_shift.shape[1:]), state_pool.reshape(state_pool.shape[1:]),
        state_wkv.reshape(state_wkv.shape[1:]), PAST_LEN, P, chunk=x_sample.shape[1], nseq=4)
    ex = lambda t: t[None]
    return (yp, ys, ex(wkv_p), ex(sh_p), ex(pl_p), ex(mk_p), ex(mv_p), ex(wkv_s), ex(sh_s), ex(pl_s))
```

```python
import functools

import jax
import jax.numpy as jnp
from jax import lax
from jax.experimental import pallas as pl
from jax.experimental.pallas import tpu as pltpu

D_MODEL = 4096
RWKV_W = 2048
POOL_W = 2048
MIX_W = RWKV_W + POOL_W
HEAD_DIM = 64
N_HEADS = RWKV_W // HEAD_DIM
DECAY_RANK = 96
AAA_RANK = 96
GATE_RANK = 256
SHIFT_W = 3 * RWKV_W + DECAY_RANK + AAA_RANK + GATE_RANK
POOL_WINDOWS = (2, 4, 8, 16)
N_POOL = len(POOL_WINDOWS)
POOL_GW = POOL_W // N_POOL
POOL_BUF = max(POOL_WINDOWS) - 1
POOL_CARRY = POOL_BUF + 1
MEM_TOKENS = 256
XA_HEADS = 4
XA_HEAD_DIM = 128
XA_W = XA_HEADS * XA_HEAD_DIM
PAST_LEN = 16384
RMS_EPS = 1e-6
GN_EPS = 64e-5

RKV_W = 3 * RWKV_W
LORA_RANKS = DECAY_RANK + AAA_RANK + GATE_RANK
LORA_W = 512
LANE_TILE = 128


def _tile_span(first, width):
    return (first // LANE_TILE * LANE_TILE, -(-(first + width) // LANE_TILE) * LANE_TILE)


WD_SPAN = _tile_span(0, DECAY_RANK)
AD_SPAN = _tile_span(DECAY_RANK, AAA_RANK)
GD_SPAN = _tile_span(DECAY_RANK + AAA_RANK, GATE_RANK)
XA_ROW_TILE = 256
SUB_HEADS = 4
SUB_L = SUB_HEADS * HEAD_DIM
N_PVEC = 16

VMEM_LIMIT = 56 * 1024 * 1024

_NN = ((1,), (0,))
_NT = ((1,), (1,))
_TN = ((0,), (0,))

f32 = jnp.float32
bf16 = jnp.bfloat16


def _dot(a, b, dims):
    return lax.dot_general(a, b, (dims, ((), ())), preferred_element_type=f32)


def _sigmoid(x):
    return 1.0 / (1.0 + jnp.exp(-x))


def _rms_scale(x, g):
    return x * lax.rsqrt(jnp.mean(x * x, axis=-1, keepdims=True) + RMS_EPS) * g


def _rmsnorm_kernel(x_ref, g_ref, o_ref):
    o_ref[...] = _rms_scale(x_ref[...], g_ref[...]).astype(o_ref.dtype)


def _rmsnorm(x, g, out_dtype, tm=512):
    m, d = x.shape
    return pl.pallas_call(
        _rmsnorm_kernel,
        grid=(m // tm,),
        in_specs=[pl.BlockSpec((tm, d), lambda i: (i, 0)), pl.BlockSpec((1, d), lambda i: (0, 0))],
        out_specs=pl.BlockSpec((tm, d), lambda i: (i, 0)),
        out_shape=jax.ShapeDtypeStruct((m, d), out_dtype),
        compiler_params=pltpu.CompilerParams(dimension_semantics=("parallel",), vmem_limit_bytes=VMEM_LIMIT),
        name="rmsnorm",
    )(x, g.reshape(1, d))


def _matmul_kernel(a_ref, b_ref, *rest, nk, act, has_norm, has_res, b_rows_are_outputs):
    rest = list(rest)
    g_ref = rest.pop(0) if has_norm else None
    res_ref = rest.pop(0) if has_res else None

    def finish(r):
        if act == "relu2":
            r = jnp.square(jnp.maximum(r, 0.0))
        if has_res:
            r = r + res_ref[...]
        return r

    def prod():
        lhs = _rms_scale(a_ref[...], g_ref[...]).astype(bf16) if has_norm else a_ref[...]
        return _dot(lhs, b_ref[...].astype(bf16), _NT if b_rows_are_outputs else _NN)

    if nk == 1:
        (o_ref,) = rest
        o_ref[...] = finish(prod()).astype(o_ref.dtype)
        return

    o_ref, acc_ref = rest
    k = pl.program_id(2)

    @pl.when(k == 0)
    def _():
        acc_ref[...] = jnp.zeros_like(acc_ref)

    acc_ref[...] += prod()

    @pl.when(k == nk - 1)
    def _():
        o_ref[...] = finish(acc_ref[...]).astype(o_ref.dtype)


def _matmul(a, b, res=None, act=None, out_dtype=f32, bt=None, norm_g=None, col0=0, n=None,
            tm=1024, tn=1024, tk=1024):
    m, kd = a.shape
    kw, ntot = b.shape if bt is None else bt.shape[::-1]
    n = ntot - col0 if n is None else n
    tm, tn, tk = min(tm, m), min(tn, n), min(tk, kd)
    assert kw == kd and m % tm == 0 and n % tn == 0 and kd % tk == 0, (a.shape, kw, ntot, col0, n, tm, tn, tk)
    nk = kd // tk
    in_specs = [pl.BlockSpec((tm, tk), lambda i, j, k: (i, k))]
    args = [a]
    if bt is None:
        assert col0 % tn == 0
        in_specs.append(pl.BlockSpec((tk, tn), lambda i, j, k: (k, col0 // tn + j)))
        args.append(b)
    else:
        assert col0 % 8 == 0 and tn % 8 == 0
        in_specs.append(pl.BlockSpec((pl.Element(tn), pl.Element(tk)),
                                     lambda i, j, k: (8 * (col0 // 8 + j * (tn // 8)), k * tk)))
        args.append(bt)
    if norm_g is not None:
        assert nk == 1 and n == tn
        in_specs.append(pl.BlockSpec((1, kd), lambda i, j, k: (0, 0)))
        args.append(norm_g.reshape(1, kd))
    if res is not None:
        in_specs.append(pl.BlockSpec((tm, tn), lambda i, j, k: (i, j)))
        args.append(res)
    return pl.pallas_call(
        functools.partial(_matmul_kernel, nk=nk, act=act, has_norm=norm_g is not None, has_res=res is not None,
                          b_rows_are_outputs=bt is not None),
        grid=(m // tm, n // tn, nk),
        in_specs=in_specs,
        out_specs=pl.BlockSpec((tm, tn), lambda i, j, k: (i, j)),
        out_shape=jax.ShapeDtypeStruct((m, n), out_dtype),
        scratch_shapes=[pltpu.VMEM((tm, tn), f32)] if nk > 1 else [],
        compiler_params=pltpu.CompilerParams(
            dimension_semantics=("parallel", "parallel", "arbitrary"), vmem_limit_bytes=VMEM_LIMIT),
        name="matmul",
    )(*args)


PV_MU_R, PV_MU_K, PV_MU_V, PV_W0, PV_A0, PV_KK, PV_KA, PV_RK, PV_LNG, PV_LNB = range(10)


def _pool_group(x, carry_ref, w, scale, pos1, wdw, C):
    ext = jnp.concatenate([carry_ref[...], x], axis=0)
    carry_ref[...] = ext[C:, :]
    acc, span = ext, 1
    while span < wdw:
        acc = acc + pltpu.roll(acc, span, 0)
        span *= 2
    mean = acc[POOL_CARRY:, :] / jnp.minimum(wdw, pos1).astype(f32)
    return _dot((mean - x).astype(bf16), w, _NN) * scale


def _mixer_kernel(zr_ref, zk_ref, zv_ref, zl_ref, prkv_ref, pl_ref, pv_ref, mul_ref, wdu_ref, wau_ref, wgu_ref,
                  s0_ref, zp0_ref, zp1_ref, zp2_ref, zp3_ref, pprev_ref, wp_ref, psc_ref,
                  o_ref, sout_ref, s_scr, hm_scr, tm_scr, ones_scr, carry_scr, pcarry_scr,
                  *, chunk, heads, nsub, nseq, nchunks, start_pos):
    C, Hg = chunk, heads
    L, R = Hg * HEAD_DIM, Hg * C
    Lb, Hb = nsub * L, nsub * Hg
    rows = nseq * C
    c = pl.program_id(1)

    @pl.when(c == 0)
    def _():
        for bi in range(nseq):
            for h in range(Hb):
                s_scr[bi, :, h * HEAD_DIM:(h + 1) * HEAD_DIM] = s0_ref[bi, h]
            for i in range(3):
                carry_scr[bi, i, 0:1, :Lb] = prkv_ref[bi, i:i + 1, :]
            carry_scr[bi, 3, 0:1, :LORA_W] = pl_ref[bi]
            for gi in range(N_POOL):
                pcarry_scr[bi, gi, 0:1, :] = jnp.zeros((1, POOL_GW), f32)
                pcarry_scr[bi, gi, 1:POOL_CARRY, :] = pprev_ref[bi, :, gi * POOL_GW:(gi + 1) * POOL_GW]
        hm_scr[...] = ((lax.broadcasted_iota(jnp.int32, (R, L), 0) // C)
                       == (lax.broadcasted_iota(jnp.int32, (R, L), 1) // HEAD_DIM)).astype(bf16)
        row = lax.broadcasted_iota(jnp.int32, (R, R), 0)
        col = lax.broadcasted_iota(jnp.int32, (R, R), 1)
        same = (row // C) == (col // C)
        tm_scr[0] = (same & ((row % C) < (col % C))).astype(bf16)
        tm_scr[1] = (same & ((row % C) <= (col % C))).astype(bf16)
        ones_scr[...] = ((lax.broadcasted_iota(jnp.int32, (SUB_L, SUB_L), 0) // HEAD_DIM)
                         == (lax.broadcasted_iota(jnp.int32, (SUB_L, SUB_L), 1) // HEAD_DIM)).astype(bf16)

    def mixed(z_ref, idx, width, mu):
        z = z_ref[...].reshape(rows, width)
        first = jnp.concatenate([jnp.broadcast_to(carry_scr[bi, idx, 0:1, :width], (C, width))
                                 for bi in range(nseq)], axis=0)
        prev = jnp.where(lax.broadcasted_iota(jnp.int32, z.shape, 0) % C == 0, first, pltpu.roll(z, 1, 0))
        for bi in range(nseq):
            carry_scr[bi, idx, 0:1, :width] = z[(bi + 1) * C - 1:(bi + 1) * C, :]
        return z + mu * (prev - z)

    pv = lambda i: pv_ref[i:i + 1, :]
    r = mixed(zr_ref, 0, Lb, pv(PV_MU_R))
    k = mixed(zk_ref, 1, Lb, pv(PV_MU_K))
    v = mixed(zv_ref, 2, Lb, pv(PV_MU_V))
    zl = mixed(zl_ref, 3, LORA_W, mul_ref[...])

    pos1 = start_pos + 1 + c * C + lax.broadcasted_iota(jnp.int32, (C, POOL_GW), 0)
    for bi in range(nseq):
        for gi, (zp_ref, wdw) in enumerate(zip((zp0_ref, zp1_ref, zp2_ref, zp3_ref), POOL_WINDOWS)):
            cols = slice(gi * POOL_GW, (gi + 1) * POOL_GW)
            o_pool = _pool_group(zp_ref[bi], pcarry_scr.at[bi, gi], wp_ref[gi], psc_ref[:, cols], pos1, wdw, C)
            o_ref[bi, :, RWKV_W + gi * POOL_GW:RWKV_W + (gi + 1) * POOL_GW] = o_pool.astype(o_ref.dtype)

    wd_in = jnp.tanh(zl[:, WD_SPAN[0]:WD_SPAN[1]]).astype(bf16)
    dec = pv(PV_W0) + _dot(wd_in, wdu_ref[...], _NN)
    sp = jnp.maximum(-dec, 0.0) + jnp.log(1.0 + jnp.exp(-jnp.abs(dec)))
    lw_all = -jnp.exp(-sp - 0.5)
    a = _sigmoid(pv(PV_A0) + _dot(zl[:, AD_SPAN[0]:AD_SPAN[1]].astype(bf16), wau_ref[...], _NN))
    gate = _dot(_sigmoid(zl[:, GD_SPAN[0]:GD_SPAN[1]]).astype(bf16), wgu_ref[...], _NN)

    ones = ones_scr[...]

    def headsum(x):
        npc = Lb // SUB_L
        xs = jnp.concatenate([x[:, p * SUB_L:(p + 1) * SUB_L] for p in range(npc)], axis=0)
        sums = _dot(xs.astype(bf16), ones, _NN)
        return jnp.concatenate([sums[p * rows:(p + 1) * rows, :] for p in range(npc)], axis=1)

    kkr = k * pv(PV_KK)
    kk_all = kkr * lax.rsqrt(jnp.maximum(headsum(kkr * kkr), 1e-24))
    kf_all = k * (1.0 + (a - 1.0) * pv(PV_KA))
    kka_all = kk_all * a

    tri = (lax.broadcasted_iota(jnp.int32, (C, C), 1) <= lax.broadcasted_iota(jnp.int32, (C, C), 0)).astype(bf16)
    hm = hm_scr[...]
    strict, incl = tm_scr[0], tm_scr[1]

    def stack16(x):
        if C % 16 == 0:
            return jnp.concatenate([x.astype(bf16)] * Hg, axis=0)
        return jnp.concatenate([x] * Hg, axis=0).astype(bf16)

    chains = [(bi, jg) for bi in range(nseq) for jg in range(nsub)]
    subs = range(len(chains))
    seq_rows = [slice(bi * C, (bi + 1) * C) for bi, _ in chains]
    lanes = [slice(jg * L, (jg + 1) * L) for _, jg in chains]
    part = lambda x, j: x[seq_rows[j], lanes[j]]

    def cumsum_t(lw):
        h1 = lw.astype(bf16)
        r1 = lw - h1.astype(f32)
        h2 = r1.astype(bf16)
        h3 = (r1 - h2.astype(f32)).astype(bf16)
        return _dot(tri, h1, _NN) + (_dot(tri, h2, _NN) + _dot(tri, h3, _NN))

    lw = [part(lw_all, j) for j in subs]
    cum = [cumsum_t(lw[j]) for j in subs]
    g = [jnp.exp(cum[j]) for j in subs]
    gi = [jnp.exp(-cum[j]) for j in subs]
    gprev = [jnp.exp(cum[j] - lw[j]) for j in subs]
    g_end = [g[j][C - 1:C, :] for j in subs]
    kk = [part(kk_all, j) for j in subs]
    kka = [part(kka_all, j) for j in subs]
    kf = [part(kf_all, j) for j in subs]
    lhs = [jnp.concatenate([stack16(-kk[j] * gprev[j]) * hm, stack16(part(r, j) * g[j]) * hm], axis=0)
           for j in subs]
    rhs = [jnp.concatenate([stack16(kka[j] * gi[j]), stack16(kf[j] * gi[j])], axis=0) for j in subs]
    res = [_dot(rhs[j], lhs[j], _NT).astype(bf16) for j in subs]
    mt = [res[j][:R, :R] * strict for j in subs]
    pt = [res[j][:R, R:] * incl for j in subs]
    nqt = [jnp.concatenate([res[j][R:, :R] * strict, res[j][R:, R:] * incl], axis=1) for j in subs]

    def rows_by_head(x):
        return jnp.concatenate([x[:, h * HEAD_DIM:(h + 1) * HEAD_DIM] for h in range(Hg)], axis=0)

    def lanes_by_head(x):
        return jnp.concatenate([x[h * C:(h + 1) * C, :] for h in range(Hg)], axis=1)

    vs = [rows_by_head(part(v, j)).astype(bf16) for j in subs]
    s = [s_scr[bi, :, lanes[j]] for j, (bi, _) in enumerate(chains)]
    base = [_dot(s[j].astype(bf16), lhs[j], _NT) + _dot(vs[j], nqt[j], _TN) for j in subs]

    x = [base[j][:, :R] for j in subs]
    mp = mt
    nsq = C.bit_length() - 1
    for i in range(nsq):
        if i < nsq - 1:
            both = [_dot(jnp.concatenate([x[j].astype(bf16), mp[j]], axis=0), mp[j], _NN) for j in subs]
            x = [x[j] + both[j][:HEAD_DIM] for j in subs]
            mp = [both[j][HEAD_DIM:].astype(bf16) for j in subs]
        else:
            x = [x[j] + _dot(x[j].astype(bf16), mp[j], _NN) for j in subs]
    u = [x[j].astype(bf16) for j in subs]

    y = [lanes_by_head((base[j][:, R:] + _dot(u[j], pt[j], _NN)).T) for j in subs]
    for j, (bi, _) in enumerate(chains):
        gl = g_end[j] * gi[j]
        s_scr[bi, :, lanes[j]] = (g_end[j] * s[j] + _dot(u[j], stack16(kka[j] * gl) * hm, _NN)
                                  + _dot(vs[j], stack16(kf[j] * gl) * hm, _TN))
    y = jnp.concatenate([jnp.concatenate(y[bi * nsub:(bi + 1) * nsub], axis=-1) for bi in range(nseq)],
                        axis=0)

    mean = headsum(y) * (1.0 / HEAD_DIM)
    d = y - mean
    var = headsum(d * d) * (1.0 / HEAD_DIM)
    yn = d * lax.rsqrt(var + GN_EPS) * pv(PV_LNG) + pv(PV_LNB)
    bonus = headsum(r * kf_all * pv(PV_RK)) * v
    o_rwkv = ((yn + bonus) * gate).astype(o_ref.dtype)
    for bi in range(nseq):
        o_ref[bi, :, :RWKV_W] = o_rwkv[bi * C:(bi + 1) * C, :]

    @pl.when(c == nchunks - 1)
    def _():
        for bi in range(nseq):
            for h in range(Hb):
                sout_ref[bi, h] = s_scr[bi, :, h * HEAD_DIM:(h + 1) * HEAD_DIM]


def _mixer(z, z_pool, shift_prev, pool_prev, s0, P, *, start_pos, chunk, heads, nsub, nseq):
    B, T, _ = z.shape
    C, Hg = chunk, heads
    L, R = Hg * HEAD_DIM, Hg * C
    assert nsub * Hg == N_HEADS and B % nseq == 0 and T % C == 0
    nchunks = T // C
    prkv = shift_prev[:, :RKV_W].reshape(B, 3, RWKV_W)
    plora = jnp.pad(shift_prev[:, RKV_W:], ((0, 0), (0, LORA_W - LORA_RANKS))).reshape(B, 1, LORA_W)
    zcol = lambda w, j: pl.BlockSpec((nseq, C, w), lambda b, c: (b, c, j))
    whole = lambda shape: pl.BlockSpec(shape, lambda b, c: (0,) * len(shape))
    per_b = lambda shape: pl.BlockSpec((nseq,) + shape, lambda b, c: (b,) + (0,) * len(shape))
    st = per_b((N_HEADS, HEAD_DIM, HEAD_DIM))
    return pl.pallas_call(
        functools.partial(_mixer_kernel, chunk=C, heads=Hg, nsub=nsub, nseq=nseq, nchunks=nchunks,
                          start_pos=start_pos),
        grid=(B // nseq, nchunks),
        in_specs=[zcol(RWKV_W, 0), zcol(RWKV_W, 1), zcol(RWKV_W, 2), zcol(LORA_W, RKV_W // LORA_W),
                  per_b((3, RWKV_W)), per_b((1, LORA_W)), whole((N_PVEC, RWKV_W)), whole((1, LORA_W)),
                  whole((WD_SPAN[1] - WD_SPAN[0], RWKV_W)), whole((AD_SPAN[1] - AD_SPAN[0], RWKV_W)),
                  whole((GD_SPAN[1] - GD_SPAN[0], RWKV_W)), st,
                  zcol(POOL_GW, 0), zcol(POOL_GW, 1), zcol(POOL_GW, 2), zcol(POOL_GW, 3),
                  per_b((POOL_BUF, POOL_W)), whole((N_POOL, POOL_GW, POOL_GW)), whole((1, POOL_W))],
        out_specs=[pl.BlockSpec((nseq, C, MIX_W), lambda b, c: (b, c, 0)), st],
        out_shape=[jax.ShapeDtypeStruct((B, T, MIX_W), bf16),
                   jax.ShapeDtypeStruct((B, N_HEADS, HEAD_DIM, HEAD_DIM), f32)],
        scratch_shapes=[pltpu.VMEM((nseq, HEAD_DIM, RWKV_W), f32), pltpu.VMEM((R, L), bf16),
                        pltpu.VMEM((2, R, R), bf16), pltpu.VMEM((SUB_L, SUB_L), bf16),
                        pltpu.VMEM((nseq, 4, 8, RWKV_W), f32),
                        pltpu.VMEM((nseq, N_POOL, POOL_CARRY, POOL_GW), f32)],
        compiler_params=pltpu.CompilerParams(
            dimension_semantics=("parallel", "arbitrary"), vmem_limit_bytes=VMEM_LIMIT),
        name="mixer",
    )(z, z, z, z, prkv, plora, P["pvec"], P["mu_lora"], P["w_decay_up"], P["w_a_up"], P["w_g_up"], s0,
      z_pool, z_pool, z_pool, z_pool, pool_prev, P["w_pool"], P["pool_scale"])


def _attend(q, k_ref, v_ref, head_axis):
    outs = []
    for h in range(XA_HEADS):
        sl = slice(h * XA_HEAD_DIM, (h + 1) * XA_HEAD_DIM)
        if head_axis:
            kh, vh = k_ref[0, :, h, :], v_ref[0, :, h, :]
        else:
            kh, vh = k_ref[0, :, sl], v_ref[0, :, sl]
        s = _dot(q[:, sl], kh.astype(bf16), _NT) * (XA_HEAD_DIM ** -0.5)
        e = jnp.exp(s - jnp.max(s, axis=-1, keepdims=True))
        v1 = jnp.concatenate([vh.astype(bf16), jnp.ones((MEM_TOKENS, XA_HEAD_DIM), bf16)], axis=1)
        ov = _dot(e.astype(bf16), v1, _NN)
        outs.append(ov[:, :XA_HEAD_DIM] / ov[:, XA_HEAD_DIM:])
    return jnp.concatenate(outs, axis=-1)


def _xattn_kernel(q_ref, k_ref, v_ref, o_ref, *, head_axis):
    o_ref[0] = _attend(q_ref[0], k_ref, v_ref, head_axis).astype(o_ref.dtype)


def _xa_layer_kernel(x_ref, g_ref, wq_ref, k_ref, v_ref, wo_ref, gf_ref, o_ref, hn_ref):
    x = x_ref[0]
    q = _dot(_rms_scale(x, g_ref[...]).astype(bf16), wq_ref[...], _NN).astype(bf16)
    o = _attend(q, k_ref, v_ref, False).astype(bf16)
    x = x + _dot(o, wo_ref[...], _NN)
    o_ref[0] = x
    hn_ref[0] = _rms_scale(x, gf_ref[...]).astype(hn_ref.dtype)


def _xa_layer(x, mk, mv, P, tq):
    B, T, D = x.shape
    row = pl.BlockSpec((1, tq, D), lambda b, i: (b, i, 0))
    kv = pl.BlockSpec((1, MEM_TOKENS, XA_W), lambda b, i: (b, 0, 0))
    whole = lambda shape: pl.BlockSpec(shape, lambda b, i: (0, 0))
    return pl.pallas_call(
        _xa_layer_kernel,
        grid=(B, T // tq),
        in_specs=[row, whole((1, D)), whole((D, XA_W)), kv, kv, whole((XA_W, D)), whole((1, D))],
        out_specs=[row, row],
        out_shape=[jax.ShapeDtypeStruct((B, T, D), f32), jax.ShapeDtypeStruct((B, T, D), bf16)],
        compiler_params=pltpu.CompilerParams(
            dimension_semantics=("parallel", "parallel"), vmem_limit_bytes=VMEM_LIMIT),
        name="xa_layer",
    )(x, P["norm_xa_g"].reshape(1, D), P["w_xq"].astype(bf16), mk, mv, P["w_xo"].astype(bf16),
      P["norm_ffn_g"].reshape(1, D))


def _xattn(q, mk, mv, tq):
    B, T, _ = q.shape
    tq = min(tq, T)
    head_axis = mk.ndim == 4
    if head_axis:
        kv_spec = pl.BlockSpec((1, MEM_TOKENS, XA_HEADS, XA_HEAD_DIM), lambda b, i: (b, 0, 0, 0))
    else:
        kv_spec = pl.BlockSpec((1, MEM_TOKENS, XA_W), lambda b, i: (b, 0, 0))
    return pl.pallas_call(
        functools.partial(_xattn_kernel, head_axis=head_axis),
        grid=(B, T // tq),
        in_specs=[pl.BlockSpec((1, tq, XA_W), lambda b, i: (b, i, 0)), kv_spec, kv_spec],
        out_specs=pl.BlockSpec((1, tq, XA_W), lambda b, i: (b, i, 0)),
        out_shape=jax.ShapeDtypeStruct((B, T, XA_W), bf16),
        compiler_params=pltpu.CompilerParams(
            dimension_semantics=("parallel", "parallel"), vmem_limit_bytes=VMEM_LIMIT),
        name="xattn",
    )(q, mk, mv)


def _block(x, mk, mv, shift_prev, pool_prev, wkv_prev, start_pos, P, chunk, nseq):
    B, T, _ = x.shape
    M = B * T
    x2 = x.reshape(M, D_MODEL)

    h = _rmsnorm(x2, P["norm_mix_g"], bf16)
    w_in_t = P["w_in_t"]
    z = _matmul(h, None, bt=w_in_t, n=RKV_W + LORA_W, tn=512, tk=D_MODEL).reshape(B, T, RKV_W + LORA_W)
    z_pool = _matmul(h, None, bt=w_in_t, col0=SHIFT_W, n=POOL_W, tn=512, tk=D_MODEL).reshape(B, T, POOL_W)
    mix, s_new = _mixer(z, z_pool, shift_prev, pool_prev, wkv_prev, P, start_pos=start_pos, chunk=chunk,
                        heads=SUB_HEADS, nsub=N_HEADS // SUB_HEADS, nseq=nseq)
    x2 = _matmul(mix.reshape(M, MIX_W), P["w_out"], res=x2, tn=512, tk=MIX_W)
    new_shift = z[:, -1, :SHIFT_W]
    new_pool = jnp.concatenate([pool_prev, z_pool], axis=1)[:, -POOL_BUF:]

    if mk.ndim == 3 and T % XA_ROW_TILE == 0:
        x3, h3 = _xa_layer(x2.reshape(B, T, D_MODEL), mk, mv, P, XA_ROW_TILE)
        x2, h = x3.reshape(M, D_MODEL), h3.reshape(M, D_MODEL)
    else:
        q = _matmul(x2, P["w_xq"], norm_g=P["norm_xa_g"], out_dtype=bf16, tm=512, tn=XA_W, tk=D_MODEL)
        o = _xattn(q.reshape(B, T, XA_W), mk, mv, tq=512)
        x2 = _matmul(o.reshape(M, XA_W), P["w_xo"], res=x2, tk=512)
        h = _rmsnorm(x2, P["norm_ffn_g"], bf16)

    hid = _matmul(h, P["w_up"], act="relu2", out_dtype=bf16, tn=512, tk=D_MODEL)
    x2 = _matmul(hid, P["w_down"], res=x2, tk=2048)
    y_out = _rmsnorm(x2, P["norm_final_g"], f32).reshape(B, T, D_MODEL)

    return y_out, new_shift, new_pool, s_new


def _params(norm_mix_g, w_in, mu_shift, w0, w_decay_up, a0, w_a_up, w_g_up, k_k, k_a, r_k, lnx_g, lnx_b, w_pool,
            pool_scale, w_out, norm_xa_g, norm_mem_g, w_xq, w_mk, w_mv, w_xo, norm_ffn_g, w_up, w_down,
            norm_final_g):
    l = 0
    mu = mu_shift[l]
    pvec = jnp.stack([mu[:RWKV_W], mu[RWKV_W:2 * RWKV_W], mu[2 * RWKV_W:RKV_W], w0[l], a0[l], k_k[l],
                      k_a[l], r_k[l].reshape(RWKV_W), lnx_g[l], lnx_b[l]])
    pvec = jnp.concatenate([pvec, jnp.zeros((N_PVEC - pvec.shape[0], RWKV_W), f32)], axis=0)

    def lora_rows(w, first, span):
        return jnp.pad(w, ((first - span[0], span[1] - first - w.shape[0]), (0, 0))).astype(bf16)

    return {
        "norm_mix_g": norm_mix_g[l], "w_in_t": jnp.transpose(w_in[l]),
        "pvec": pvec, "mu_lora": jnp.pad(mu[RKV_W:], (0, LORA_W - LORA_RANKS)).reshape(1, LORA_W),
        "w_decay_up": lora_rows(w_decay_up[l], 0, WD_SPAN), "w_a_up": lora_rows(w_a_up[l], DECAY_RANK, AD_SPAN),
        "w_g_up": lora_rows(w_g_up[l], DECAY_RANK + AAA_RANK, GD_SPAN),
        "w_pool": w_pool[l].astype(bf16), "pool_scale": pool_scale[l].reshape(1, POOL_W), "w_out": w_out[l],
        "norm_xa_g": norm_xa_g[l], "w_xq": w_xq[l], "w_xo": w_xo[l],
        "norm_ffn_g": norm_ffn_g[l], "w_up": w_up[l], "w_down": w_down[l],
        "norm_final_g": norm_final_g, "norm_mem_g": norm_mem_g[l], "w_mk": w_mk[l], "w_mv": w_mv[l],
    }


def kernel(x_prompt, x_sample, mem_prompt, state_wkv, state_shift, state_pool, cache_mem_k, cache_mem_v, norm_mix_g, w_in, mu_shift, w0, w_decay_up, a0, w_a_up, w_g_up, k_k, k_a, r_k, lnx_g, lnx_b, w_pool, pool_scale, w_out, norm_xa_g, norm_mem_g, w_xq, w_mk, w_mv, w_xo, norm_ffn_g, w_up, w_down, norm_final_g):
    Bp = x_prompt.shape[0]
    P = _params(norm_mix_g, w_in, mu_shift, w0, w_decay_up, a0, w_a_up, w_g_up, k_k, k_a, r_k, lnx_g, lnx_b, w_pool,
                pool_scale, w_out, norm_xa_g, norm_mem_g, w_xq, w_mk, w_mv, w_xo, norm_ffn_g, w_up, w_down,
                norm_final_g)
    mem = _rmsnorm(mem_prompt.reshape(Bp * MEM_TOKENS, D_MODEL), P["norm_mem_g"], bf16)
    mk3 = _matmul(mem, P["w_mk"], tn=512, tk=D_MODEL).reshape(Bp, MEM_TOKENS, XA_W)
    mv3 = _matmul(mem, P["w_mv"], tn=512, tk=D_MODEL).reshape(Bp, MEM_TOKENS, XA_W)
    mk_p = mk3.reshape(Bp, MEM_TOKENS, XA_HEADS, XA_HEAD_DIM)
    mv_p = mv3.reshape(Bp, MEM_TOKENS, XA_HEADS, XA_HEAD_DIM)

    yp, sh_p, pl_p, wkv_p = _block(
        x_prompt, mk3, mv3, jnp.zeros((Bp, SHIFT_W), f32), jnp.zeros((Bp, POOL_BUF, POOL_W), f32),
        jnp.zeros((Bp, N_HEADS, HEAD_DIM, HEAD_DIM), f32), 0, P, chunk=64, nseq=1)
    ys, sh_s, pl_s, wkv_s = _block(
        x_sample, cache_mem_k.reshape(cache_mem_k.shape[1:]), cache_mem_v.reshape(cache_mem_v.shape[1:]),
        state_shift.reshape(state_shift.shape[1:]), state_pool.reshape(state_pool.shape[1:]),
        state_wkv.reshape(state_wkv.shape[1:]), PAST_LEN, P, chunk=x_sample.shape[1], nseq=4)
    ex = lambda t: t[None]
    return (yp, ys, ex(wkv_p), ex(sh_p), ex(pl_p), ex(mk_p), ex(mv_p), ex(wkv_s), ex(sh_s), ex(pl_s))
```

```python
import functools

import jax
import jax.numpy as jnp
from jax import lax
from jax.experimental import pallas as pl
from jax.experimental.pallas import tpu as pltpu

D_MODEL = 4096
RWKV_W = 2048
POOL_W = 2048
MIX_W = RWKV_W + POOL_W
HEAD_DIM = 64
N_HEADS = RWKV_W // HEAD_DIM
DECAY_RANK = 96
AAA_RANK = 96
GATE_RANK = 256
SHIFT_W = 3 * RWKV_W + DECAY_RANK + AAA_RANK + GATE_RANK
POOL_WINDOWS = (2, 4, 8, 16)
N_POOL = len(POOL_WINDOWS)
POOL_GW = POOL_W // N_POOL
POOL_BUF = max(POOL_WINDOWS) - 1
POOL_CARRY = POOL_BUF + 1
MEM_TOKENS = 256
XA_HEADS = 4
XA_HEAD_DIM = 128
XA_W = XA_HEADS * XA_HEAD_DIM
PAST_LEN = 16384
RMS_EPS = 1e-6
GN_EPS = 64e-5

RKV_W = 3 * RWKV_W
LORA_RANKS = DECAY_RANK + AAA_RANK + GATE_RANK
LORA_W = 512
IN_WP = RKV_W + LORA_W + POOL_W
LANE_TILE = 128


def _tile_span(first, width):
    return (first // LANE_TILE * LANE_TILE, -(-(first + width) // LANE_TILE) * LANE_TILE)


WD_SPAN = _tile_span(0, DECAY_RANK)
AD_SPAN = _tile_span(DECAY_RANK, AAA_RANK)
GD_SPAN = _tile_span(DECAY_RANK + AAA_RANK, GATE_RANK)
NORM_SLAB = 128
XA_ROW_TILE = 256
SUB_HEADS = 4
SUB_L = SUB_HEADS * HEAD_DIM
N_PVEC = 16

VMEM_LIMIT = 56 * 1024 * 1024

_NN = ((1,), (0,))
_NT = ((1,), (1,))
_TN = ((0,), (0,))

f32 = jnp.float32
bf16 = jnp.bfloat16


def _dot(a, b, dims):
    return lax.dot_general(a, b, (dims, ((), ())), preferred_element_type=f32)


def _sigmoid(x):
    return 1.0 / (1.0 + jnp.exp(-x))


def _rms_scale(x, g):
    return x * lax.rsqrt(jnp.mean(x * x, axis=-1, keepdims=True) + RMS_EPS) * g


def _rmsnorm_kernel(x_ref, g_ref, o_ref):
    o_ref[...] = _rms_scale(x_ref[...], g_ref[...]).astype(o_ref.dtype)


def _rmsnorm(x, g, out_dtype, tm=512):
    m, d = x.shape
    return pl.pallas_call(
        _rmsnorm_kernel,
        grid=(m // tm,),
        in_specs=[pl.BlockSpec((tm, d), lambda i: (i, 0)), pl.BlockSpec((1, d), lambda i: (0, 0))],
        out_specs=pl.BlockSpec((tm, d), lambda i: (i, 0)),
        out_shape=jax.ShapeDtypeStruct((m, d), out_dtype),
        compiler_params=pltpu.CompilerParams(dimension_semantics=("parallel",), vmem_limit_bytes=VMEM_LIMIT),
        name="rmsnorm",
    )(x, g.reshape(1, d))


def _matmul_kernel(a_ref, b_ref, *rest, nk, act, has_norm, has_res, b_rows_are_outputs):
    rest = list(rest)
    g_ref = rest.pop(0) if has_norm else None
    res_ref = rest.pop(0) if has_res else None

    def finish(r):
        if act == "relu2":
            r = jnp.square(jnp.maximum(r, 0.0))
        if has_res:
            r = r + res_ref[...]
        return r

    def prod():
        lhs = _rms_scale(a_ref[...], g_ref[...]).astype(bf16) if has_norm else a_ref[...]
        return _dot(lhs, b_ref[...].astype(bf16), _NT if b_rows_are_outputs else _NN)

    if nk == 1:
        (o_ref,) = rest
        o_ref[...] = finish(prod()).astype(o_ref.dtype)
        return

    o_ref, acc_ref = rest
    k = pl.program_id(2)

    @pl.when(k == 0)
    def _():
        acc_ref[...] = jnp.zeros_like(acc_ref)

    acc_ref[...] += prod()

    @pl.when(k == nk - 1)
    def _():
        o_ref[...] = finish(acc_ref[...]).astype(o_ref.dtype)


def _matmul(a, b, res=None, act=None, out_dtype=f32, bt=None, norm_g=None, col0=0, n=None,
            tm=1024, tn=1024, tk=1024):
    m, kd = a.shape
    kw, ntot = b.shape if bt is None else bt.shape[::-1]
    n = ntot - col0 if n is None else n
    tm, tn, tk = min(tm, m), min(tn, n), min(tk, kd)
    assert kw == kd and m % tm == 0 and n % tn == 0 and kd % tk == 0, (a.shape, kw, ntot, col0, n, tm, tn, tk)
    nk = kd // tk
    in_specs = [pl.BlockSpec((tm, tk), lambda i, j, k: (i, k))]
    args = [a]
    if bt is None:
        assert col0 % tn == 0
        in_specs.append(pl.BlockSpec((tk, tn), lambda i, j, k: (k, col0 // tn + j)))
        args.append(b)
    else:
        assert col0 % 8 == 0 and tn % 8 == 0
        in_specs.append(pl.BlockSpec((pl.Element(tn), pl.Element(tk)),
                                     lambda i, j, k: (8 * (col0 // 8 + j * (tn // 8)), k * tk)))
        args.append(bt)
    if norm_g is not None:
        assert nk == 1 and n == tn
        in_specs.append(pl.BlockSpec((1, kd), lambda i, j, k: (0, 0)))
        args.append(norm_g.reshape(1, kd))
    if res is not None:
        in_specs.append(pl.BlockSpec((tm, tn), lambda i, j, k: (i, j)))
        args.append(res)
    return pl.pallas_call(
        functools.partial(_matmul_kernel, nk=nk, act=act, has_norm=norm_g is not None, has_res=res is not None,
                          b_rows_are_outputs=bt is not None),
        grid=(m // tm, n // tn, nk),
        in_specs=in_specs,
        out_specs=pl.BlockSpec((tm, tn), lambda i, j, k: (i, j)),
        out_shape=jax.ShapeDtypeStruct((m, n), out_dtype),
        scratch_shapes=[pltpu.VMEM((tm, tn), f32)] if nk > 1 else [],
        compiler_params=pltpu.CompilerParams(
            dimension_semantics=("parallel", "parallel", "arbitrary"), vmem_limit_bytes=VMEM_LIMIT),
        name="matmul",
    )(*args)


def _norm_project_kernel(x_ref, g_ref, w_ref, o_ref, h_scr):
    @pl.when(pl.program_id(1) == 0)
    def _():
        for r0 in range(0, h_scr.shape[0], NORM_SLAB):
            rows = slice(r0, min(r0 + NORM_SLAB, h_scr.shape[0]))
            h_scr[rows, :] = _rms_scale(x_ref[rows, :], g_ref[...]).astype(bf16)

    o_ref[...] = _dot(h_scr[...], w_ref[...].astype(bf16), _NT)


def _norm_project(x, g, w_t, tm=1024, tn=512):
    m, d = x.shape
    tm = min(tm, m)
    nj_shift = (RKV_W + LORA_W) // tn
    nj = nj_shift + POOL_W // tn
    assert m % tm == 0 and (RKV_W + LORA_W) % tn == 0 and POOL_W % tn == 0 and SHIFT_W % 8 == 0 and tn % 8 == 0

    def w_rows(i, j):
        return (8 * jnp.where(j < nj_shift, j * (tn // 8), SHIFT_W // 8 + (j - nj_shift) * (tn // 8)), 0)

    return pl.pallas_call(
        _norm_project_kernel,
        grid=(m // tm, nj),
        in_specs=[pl.BlockSpec((tm, d), lambda i, j: (i, 0), pipeline_mode=pl.Buffered(1)),
                  pl.BlockSpec((1, d), lambda i, j: (0, 0)),
                  pl.BlockSpec((pl.Element(tn), pl.Element(d)), w_rows)],
        out_specs=pl.BlockSpec((tm, tn), lambda i, j: (i, j)),
        out_shape=jax.ShapeDtypeStruct((m, nj * tn), f32),
        scratch_shapes=[pltpu.VMEM((tm, d), bf16)],
        compiler_params=pltpu.CompilerParams(
            dimension_semantics=("parallel", "arbitrary"), vmem_limit_bytes=VMEM_LIMIT),
        name="norm_project",
    )(x, g.reshape(1, d), w_t)


PV_MU_R, PV_MU_K, PV_MU_V, PV_W0, PV_A0, PV_KK, PV_KA, PV_RK, PV_LNG, PV_LNB = range(10)


def _pool_group(x, carry_ref, w, scale, pos1, wdw, C):
    ext = jnp.concatenate([carry_ref[...], x], axis=0)
    carry_ref[...] = ext[C:, :]
    acc, span = ext, 1
    while span < wdw:
        acc = acc + pltpu.roll(acc, span, 0)
        span *= 2
    mean = acc[POOL_CARRY:, :] / jnp.minimum(wdw, pos1).astype(f32)
    return _dot((mean - x).astype(bf16), w, _NN) * scale


def _mixer_kernel(zr_ref, zk_ref, zv_ref, zl_ref, prkv_ref, pl_ref, pv_ref, mul_ref, wdu_ref, wau_ref, wgu_ref,
                  s0_ref, zp0_ref, zp1_ref, zp2_ref, zp3_ref, pprev_ref, wp_ref, psc_ref,
                  o_ref, sout_ref, s_scr, hm_scr, tm_scr, ones_scr, carry_scr, pcarry_scr,
                  *, chunk, heads, nsub, nseq, nchunks, start_pos):
    C, Hg = chunk, heads
    L, R = Hg * HEAD_DIM, Hg * C
    Lb, Hb = nsub * L, nsub * Hg
    rows = nseq * C
    c = pl.program_id(1)

    @pl.when(c == 0)
    def _():
        for bi in range(nseq):
            s_scr[bi] = s0_ref[bi]
            for i in range(3):
                carry_scr[bi, i, 0:1, :Lb] = prkv_ref[bi, i:i + 1, :]
            carry_scr[bi, 3, 0:1, :LORA_W] = pl_ref[bi]
            for gi in range(N_POOL):
                pcarry_scr[bi, gi, 0:1, :] = jnp.zeros((1, POOL_GW), f32)
                pcarry_scr[bi, gi, 1:POOL_CARRY, :] = pprev_ref[bi, :, gi * POOL_GW:(gi + 1) * POOL_GW]
        hm_scr[...] = ((lax.broadcasted_iota(jnp.int32, (R, L), 0) // C)
                       == (lax.broadcasted_iota(jnp.int32, (R, L), 1) // HEAD_DIM)).astype(bf16)
        row = lax.broadcasted_iota(jnp.int32, (R, R), 0)
        col = lax.broadcasted_iota(jnp.int32, (R, R), 1)
        same = (row // C) == (col // C)
        tm_scr[0] = (same & ((row % C) < (col % C))).astype(bf16)
        tm_scr[1] = (same & ((row % C) <= (col % C))).astype(bf16)
        ones_scr[...] = ((lax.broadcasted_iota(jnp.int32, (SUB_L, SUB_L), 0) // HEAD_DIM)
                         == (lax.broadcasted_iota(jnp.int32, (SUB_L, SUB_L), 1) // HEAD_DIM)).astype(bf16)

    def mixed(z_ref, idx, width, mu):
        z = z_ref[...].reshape(rows, width)
        first = jnp.concatenate([jnp.broadcast_to(carry_scr[bi, idx, 0:1, :width], (C, width))
                                 for bi in range(nseq)], axis=0)
        prev = jnp.where(lax.broadcasted_iota(jnp.int32, z.shape, 0) % C == 0, first, pltpu.roll(z, 1, 0))
        for bi in range(nseq):
            carry_scr[bi, idx, 0:1, :width] = z[(bi + 1) * C - 1:(bi + 1) * C, :]
        return z + mu * (prev - z)

    pv = lambda i: pv_ref[i:i + 1, :]
    r = mixed(zr_ref, 0, Lb, pv(PV_MU_R))
    k = mixed(zk_ref, 1, Lb, pv(PV_MU_K))
    v = mixed(zv_ref, 2, Lb, pv(PV_MU_V))
    zl = mixed(zl_ref, 3, LORA_W, mul_ref[...])

    pos1 = start_pos + 1 + c * C + lax.broadcasted_iota(jnp.int32, (C, POOL_GW), 0)
    for bi in range(nseq):
        for gi, (zp_ref, wdw) in enumerate(zip((zp0_ref, zp1_ref, zp2_ref, zp3_ref), POOL_WINDOWS)):
            cols = slice(gi * POOL_GW, (gi + 1) * POOL_GW)
            o_pool = _pool_group(zp_ref[bi], pcarry_scr.at[bi, gi], wp_ref[gi], psc_ref[:, cols], pos1, wdw, C)
            o_ref[bi, :, RWKV_W + gi * POOL_GW:RWKV_W + (gi + 1) * POOL_GW] = o_pool.astype(o_ref.dtype)

    wd_in = jnp.tanh(zl[:, WD_SPAN[0]:WD_SPAN[1]]).astype(bf16)
    dec = pv(PV_W0) + _dot(wd_in, wdu_ref[...], _NN)
    sp = jnp.maximum(-dec, 0.0) + jnp.log(1.0 + jnp.exp(-jnp.abs(dec)))
    lw_all = -jnp.exp(-sp - 0.5)
    a = _sigmoid(pv(PV_A0) + _dot(zl[:, AD_SPAN[0]:AD_SPAN[1]].astype(bf16), wau_ref[...], _NN))
    gate = _dot(_sigmoid(zl[:, GD_SPAN[0]:GD_SPAN[1]]).astype(bf16), wgu_ref[...], _NN)

    ones = ones_scr[...]

    def headsum(x):
        npc = Lb // SUB_L
        xs = jnp.concatenate([x[:, p * SUB_L:(p + 1) * SUB_L] for p in range(npc)], axis=0)
        sums = _dot(xs.astype(bf16), ones, _NN)
        return jnp.concatenate([sums[p * rows:(p + 1) * rows, :] for p in range(npc)], axis=1)

    kkr = k * pv(PV_KK)
    kk_all = kkr * lax.rsqrt(jnp.maximum(headsum(kkr * kkr), 1e-24))
    kf_all = k * (1.0 + (a - 1.0) * pv(PV_KA))
    kka_all = kk_all * a

    tri = (lax.broadcasted_iota(jnp.int32, (C, C), 1) <= lax.broadcasted_iota(jnp.int32, (C, C), 0)).astype(bf16)
    hm = hm_scr[...]
    strict, incl = tm_scr[0], tm_scr[1]

    def stack16(x):
        if C % 16 == 0:
            return jnp.concatenate([x.astype(bf16)] * Hg, axis=0)
        return jnp.concatenate([x] * Hg, axis=0).astype(bf16)

    chains = [(bi, jg) for bi in range(nseq) for jg in range(nsub)]
    subs = range(len(chains))
    seq_rows = [slice(bi * C, (bi + 1) * C) for bi, _ in chains]
    lanes = [slice(jg * L, (jg + 1) * L) for _, jg in chains]
    part = lambda x, j: x[seq_rows[j], lanes[j]]

    def cumsum_t(lw):
        h1 = lw.astype(bf16)
        r1 = lw - h1.astype(f32)
        h2 = r1.astype(bf16)
        h3 = (r1 - h2.astype(f32)).astype(bf16)
        return _dot(tri, h1, _NN) + (_dot(tri, h2, _NN) + _dot(tri, h3, _NN))

    lw = [part(lw_all, j) for j in subs]
    cum = [cumsum_t(lw[j]) for j in subs]
    g = [jnp.exp(cum[j]) for j in subs]
    gi = [jnp.exp(-cum[j]) for j in subs]
    gprev = [jnp.exp(cum[j] - lw[j]) for j in subs]
    g_end = [g[j][C - 1:C, :] for j in subs]
    kk = [part(kk_all, j) for j in subs]
    kka = [part(kka_all, j) for j in subs]
    kf = [part(kf_all, j) for j in subs]
    lhs = [jnp.concatenate([stack16(-kk[j] * gprev[j]) * hm, stack16(part(r, j) * g[j]) * hm], axis=0)
           for j in subs]
    rhs = [jnp.concatenate([stack16(kka[j] * gi[j]), stack16(kf[j] * gi[j])], axis=0) for j in subs]
    res = [_dot(rhs[j], lhs[j], _NT).astype(bf16) for j in subs]
    mt = [res[j][:R, :R] * strict for j in subs]
    pt = [res[j][:R, R:] * incl for j in subs]
    nqt = [jnp.concatenate([res[j][R:, :R] * strict, res[j][R:, R:] * incl], axis=1) for j in subs]

    def rows_by_head(x):
        return jnp.concatenate([x[:, h * HEAD_DIM:(h + 1) * HEAD_DIM] for h in range(Hg)], axis=0)

    def lanes_by_head(x):
        return jnp.concatenate([x[h * C:(h + 1) * C, :] for h in range(Hg)], axis=1)

    vs = [rows_by_head(part(v, j)).astype(bf16) for j in subs]
    s = [s_scr[bi, :, lanes[j]] for j, (bi, _) in enumerate(chains)]
    base = [_dot(s[j].astype(bf16), lhs[j], _NT) + _dot(vs[j], nqt[j], _TN) for j in subs]

    x = [base[j][:, :R] for j in subs]
    mp = mt
    nsq = C.bit_length() - 1
    for i in range(nsq):
        if i < nsq - 1:
            both = [_dot(jnp.concatenate([x[j].astype(bf16), mp[j]], axis=0), mp[j], _NN) for j in subs]
            x = [x[j] + both[j][:HEAD_DIM] for j in subs]
            mp = [both[j][HEAD_DIM:].astype(bf16) for j in subs]
        else:
            x = [x[j] + _dot(x[j].astype(bf16), mp[j], _NN) for j in subs]
    u = [x[j].astype(bf16) for j in subs]

    y = [lanes_by_head((base[j][:, R:] + _dot(u[j], pt[j], _NN)).T) for j in subs]
    for j, (bi, _) in enumerate(chains):
        gl = g_end[j] * gi[j]
        s_scr[bi, :, lanes[j]] = (g_end[j] * s[j] + _dot(u[j], stack16(kka[j] * gl) * hm, _NN)
                                  + _dot(vs[j], stack16(kf[j] * gl) * hm, _TN))
    y = jnp.concatenate([jnp.concatenate(y[bi * nsub:(bi + 1) * nsub], axis=-1) for bi in range(nseq)],
                        axis=0)

    mean = headsum(y) * (1.0 / HEAD_DIM)
    d = y - mean
    var = headsum(d * d) * (1.0 / HEAD_DIM)
    yn = d * lax.rsqrt(var + GN_EPS) * pv(PV_LNG) + pv(PV_LNB)
    bonus = headsum(r * kf_all * pv(PV_RK)) * v
    o_rwkv = ((yn + bonus) * gate).astype(o_ref.dtype)
    for bi in range(nseq):
        o_ref[bi, :, :RWKV_W] = o_rwkv[bi * C:(bi + 1) * C, :]

    @pl.when(c == nchunks - 1)
    def _():
        for bi in range(nseq):
            for h in range(Hb):
                sout_ref[bi, h] = s_scr[bi, :, h * HEAD_DIM:(h + 1) * HEAD_DIM]


def _mixer(z, shift_prev, pool_prev, s0, P, *, start_pos, chunk, heads, nsub, nseq):
    B, T, _ = z.shape
    C, Hg = chunk, heads
    L, R = Hg * HEAD_DIM, Hg * C
    assert nsub * Hg == N_HEADS and B % nseq == 0 and T % C == 0
    nchunks = T // C
    prkv = shift_prev[:, :RKV_W].reshape(B, 3, RWKV_W)
    plora = jnp.pad(shift_prev[:, RKV_W:], ((0, 0), (0, LORA_W - LORA_RANKS))).reshape(B, 1, LORA_W)
    zcol = lambda w, j: pl.BlockSpec((nseq, C, w), lambda b, c: (b, c, j))
    whole = lambda shape: pl.BlockSpec(shape, lambda b, c: (0,) * len(shape))
    per_b = lambda shape: pl.BlockSpec((nseq,) + shape, lambda b, c: (b,) + (0,) * len(shape))
    st = per_b((N_HEADS, HEAD_DIM, HEAD_DIM))
    pool0 = (IN_WP - POOL_W) // POOL_GW
    return pl.pallas_call(
        functools.partial(_mixer_kernel, chunk=C, heads=Hg, nsub=nsub, nseq=nseq, nchunks=nchunks,
                          start_pos=start_pos),
        grid=(B // nseq, nchunks),
        in_specs=[zcol(RWKV_W, 0), zcol(RWKV_W, 1), zcol(RWKV_W, 2), zcol(LORA_W, RKV_W // LORA_W),
                  per_b((3, RWKV_W)), per_b((1, LORA_W)), whole((N_PVEC, RWKV_W)), whole((1, LORA_W)),
                  whole((WD_SPAN[1] - WD_SPAN[0], RWKV_W)), whole((AD_SPAN[1] - AD_SPAN[0], RWKV_W)),
                  whole((GD_SPAN[1] - GD_SPAN[0], RWKV_W)), per_b((HEAD_DIM, RWKV_W)),
                  zcol(POOL_GW, pool0), zcol(POOL_GW, pool0 + 1), zcol(POOL_GW, pool0 + 2), zcol(POOL_GW, pool0 + 3),
                  per_b((POOL_BUF, POOL_W)), whole((N_POOL, POOL_GW, POOL_GW)), whole((1, POOL_W))],
        out_specs=[pl.BlockSpec((nseq, C, MIX_W), lambda b, c: (b, c, 0)), st],
        out_shape=[jax.ShapeDtypeStruct((B, T, MIX_W), bf16),
                   jax.ShapeDtypeStruct((B, N_HEADS, HEAD_DIM, HEAD_DIM), f32)],
        scratch_shapes=[pltpu.VMEM((nseq, HEAD_DIM, RWKV_W), f32), pltpu.VMEM((R, L), bf16),
                        pltpu.VMEM((2, R, R), bf16), pltpu.VMEM((SUB_L, SUB_L), bf16),
                        pltpu.VMEM((nseq, 4, 8, RWKV_W), f32),
                        pltpu.VMEM((nseq, N_POOL, POOL_CARRY, POOL_GW), f32)],
        compiler_params=pltpu.CompilerParams(
            dimension_semantics=("parallel", "arbitrary"), vmem_limit_bytes=VMEM_LIMIT),
        name="mixer",
    )(z, z, z, z, prkv, plora, P["pvec"], P["mu_lora"], P["w_decay_up"], P["w_a_up"], P["w_g_up"], s0,
      z, z, z, z, pool_prev, P["w_pool"], P["pool_scale"])


def _attend(q, k_ref, v_ref, head_axis):
    outs = []
    for h in range(XA_HEADS):
        sl = slice(h * XA_HEAD_DIM, (h + 1) * XA_HEAD_DIM)
        if head_axis:
            kh, vh = k_ref[0, :, h, :], v_ref[0, :, h, :]
        else:
            kh, vh = k_ref[0, :, sl], v_ref[0, :, sl]
        s = _dot(q[:, sl], kh.astype(bf16), _NT) * (XA_HEAD_DIM ** -0.5)
        e = jnp.exp(s - jnp.max(s, axis=-1, keepdims=True))
        v1 = jnp.concatenate([vh.astype(bf16), jnp.ones((MEM_TOKENS, XA_HEAD_DIM), bf16)], axis=1)
        ov = _dot(e.astype(bf16), v1, _NN)
        outs.append(ov[:, :XA_HEAD_DIM] / ov[:, XA_HEAD_DIM:])
    return jnp.concatenate(outs, axis=-1)


def _xattn_kernel(q_ref, k_ref, v_ref, o_ref, *, head_axis):
    o_ref[0] = _attend(q_ref[0], k_ref, v_ref, head_axis).astype(o_ref.dtype)


def _xa_layer_kernel(x_ref, g_ref, wq_ref, k_ref, v_ref, wo_ref, gf_ref, o_ref, hn_ref):
    x = x_ref[0]
    q = _dot(_rms_scale(x, g_ref[...]).astype(bf16), wq_ref[...], _NN).astype(bf16)
    o = _attend(q, k_ref, v_ref, False).astype(bf16)
    x = x + _dot(o, wo_ref[...], _NN)
    o_ref[0] = x
    hn_ref[0] = _rms_scale(x, gf_ref[...]).astype(hn_ref.dtype)


def _xa_layer(x, mk, mv, P, tq):
    B, T, D = x.shape
    row = pl.BlockSpec((1, tq, D), lambda b, i: (b, i, 0))
    kv = pl.BlockSpec((1, MEM_TOKENS, XA_W), lambda b, i: (b, 0, 0))
    whole = lambda shape: pl.BlockSpec(shape, lambda b, i: (0, 0))
    return pl.pallas_call(
        _xa_layer_kernel,
        grid=(B, T // tq),
        in_specs=[row, whole((1, D)), whole((D, XA_W)), kv, kv, whole((XA_W, D)), whole((1, D))],
        out_specs=[row, row],
        out_shape=[jax.ShapeDtypeStruct((B, T, D), f32), jax.ShapeDtypeStruct((B, T, D), bf16)],
        compiler_params=pltpu.CompilerParams(
            dimension_semantics=("parallel", "parallel"), vmem_limit_bytes=VMEM_LIMIT),
        name="xa_layer",
    )(x, P["norm_xa_g"].reshape(1, D), P["w_xq"].astype(bf16), mk, mv, P["w_xo"].astype(bf16),
      P["norm_ffn_g"].reshape(1, D))


def _xattn(q, mk, mv, tq):
    B, T, _ = q.shape
    tq = min(tq, T)
    head_axis = mk.ndim == 4
    if head_axis:
        kv_spec = pl.BlockSpec((1, MEM_TOKENS, XA_HEADS, XA_HEAD_DIM), lambda b, i: (b, 0, 0, 0))
    else:
        kv_spec = pl.BlockSpec((1, MEM_TOKENS, XA_W), lambda b, i: (b, 0, 0))
    return pl.pallas_call(
        functools.partial(_xattn_kernel, head_axis=head_axis),
        grid=(B, T // tq),
        in_specs=[pl.BlockSpec((1, tq, XA_W), lambda b, i: (b, i, 0)), kv_spec, kv_spec],
        out_specs=pl.BlockSpec((1, tq, XA_W), lambda b, i: (b, i, 0)),
        out_shape=jax.ShapeDtypeStruct((B, T, XA_W), bf16),
        compiler_params=pltpu.CompilerParams(
            dimension_semantics=("parallel", "parallel"), vmem_limit_bytes=VMEM_LIMIT),
        name="xattn",
    )(q, mk, mv)


def _block(x, mk, mv, shift_prev, pool_prev, wkv_prev, start_pos, P, chunk, nseq):
    B, T, _ = x.shape
    M = B * T
    x2 = x.reshape(M, D_MODEL)

    z = _norm_project(x2, P["norm_mix_g"], P["w_in_t"]).reshape(B, T, IN_WP)
    mix, s_new = _mixer(z, shift_prev, pool_prev, wkv_prev, P, start_pos=start_pos, chunk=chunk,
                        heads=SUB_HEADS, nsub=N_HEADS // SUB_HEADS, nseq=nseq)
    x2 = _matmul(mix.reshape(M, MIX_W), P["w_out"], res=x2, tn=512, tk=MIX_W)
    new_shift = z[:, -1, :SHIFT_W]
    new_pool = jnp.concatenate([pool_prev, z[:, :, IN_WP - POOL_W:]], axis=1)[:, -POOL_BUF:]

    if mk.ndim == 3 and T % XA_ROW_TILE == 0:
        x3, h3 = _xa_layer(x2.reshape(B, T, D_MODEL), mk, mv, P, XA_ROW_TILE)
        x2, h = x3.reshape(M, D_MODEL), h3.reshape(M, D_MODEL)
    else:
        q = _matmul(x2, P["w_xq"], norm_g=P["norm_xa_g"], out_dtype=bf16, tm=512, tn=XA_W, tk=D_MODEL)
        o = _xattn(q.reshape(B, T, XA_W), mk, mv, tq=512)
        x2 = _matmul(o.reshape(M, XA_W), P["w_xo"], res=x2, tk=512)
        h = _rmsnorm(x2, P["norm_ffn_g"], bf16)

    hid = _matmul(h, P["w_up"], act="relu2", out_dtype=bf16, tn=512, tk=D_MODEL)
    x2 = _matmul(hid, P["w_down"], res=x2, tk=2048)
    y_out = _rmsnorm(x2, P["norm_final_g"], f32).reshape(B, T, D_MODEL)

    return y_out, new_shift, new_pool, s_new


def _params(norm_mix_g, w_in, mu_shift, w0, w_decay_up, a0, w_a_up, w_g_up, k_k, k_a, r_k, lnx_g, lnx_b, w_pool,
            pool_scale, w_out, norm_xa_g, norm_mem_g, w_xq, w_mk, w_mv, w_xo, norm_ffn_g, w_up, w_down,
            norm_final_g):
    l = 0
    mu = mu_shift[l]
    pvec = jnp.stack([mu[:RWKV_W], mu[RWKV_W:2 * RWKV_W], mu[2 * RWKV_W:RKV_W], w0[l], a0[l], k_k[l],
                      k_a[l], r_k[l].reshape(RWKV_W), lnx_g[l], lnx_b[l]])
    pvec = jnp.concatenate([pvec, jnp.zeros((N_PVEC - pvec.shape[0], RWKV_W), f32)], axis=0)

    def lora_rows(w, first, span):
        return jnp.pad(w, ((first - span[0], span[1] - first - w.shape[0]), (0, 0))).astype(bf16)

    return {
        "norm_mix_g": norm_mix_g[l], "w_in_t": jnp.transpose(w_in[l]),
        "pvec": pvec, "mu_lora": jnp.pad(mu[RKV_W:], (0, LORA_W - LORA_RANKS)).reshape(1, LORA_W),
        "w_decay_up": lora_rows(w_decay_up[l], 0, WD_SPAN), "w_a_up": lora_rows(w_a_up[l], DECAY_RANK, AD_SPAN),
        "w_g_up": lora_rows(w_g_up[l], DECAY_RANK + AAA_RANK, GD_SPAN),
        "w_pool": w_pool[l].astype(bf16), "pool_scale": pool_scale[l].reshape(1, POOL_W), "w_out": w_out[l],
        "norm_xa_g": norm_xa_g[l], "w_xq": w_xq[l], "w_xo": w_xo[l],
        "norm_ffn_g": norm_ffn_g[l], "w_up": w_up[l], "w_down": w_down[l],
        "norm_final_g": norm_final_g, "norm_mem_g": norm_mem_g[l], "w_mk": w_mk[l], "w_mv": w_mv[l],
    }


def kernel(x_prompt, x_sample, mem_prompt, state_wkv, state_shift, state_pool, cache_mem_k, cache_mem_v, norm_mix_g, w_in, mu_shift, w0, w_decay_up, a0, w_a_up, w_g_up, k_k, k_a, r_k, lnx_g, lnx_b, w_pool, pool_scale, w_out, norm_xa_g, norm_mem_g, w_xq, w_mk, w_mv, w_xo, norm_ffn_g, w_up, w_down, norm_final_g):
    Bp = x_prompt.shape[0]
    P = _params(norm_mix_g, w_in, mu_shift, w0, w_decay_up, a0, w_a_up, w_g_up, k_k, k_a, r_k, lnx_g, lnx_b, w_pool,
                pool_scale, w_out, norm_xa_g, norm_mem_g, w_xq, w_mk, w_mv, w_xo, norm_ffn_g, w_up, w_down,
                norm_final_g)
    mem = _rmsnorm(mem_prompt.reshape(Bp * MEM_TOKENS, D_MODEL), P["norm_mem_g"], bf16)
    mk3 = _matmul(mem, P["w_mk"], tn=512, tk=D_MODEL).reshape(Bp, MEM_TOKENS, XA_W)
    mv3 = _matmul(mem, P["w_mv"], tn=512, tk=D_MODEL).reshape(Bp, MEM_TOKENS, XA_W)
    mk_p = mk3.reshape(Bp, MEM_TOKENS, XA_HEADS, XA_HEAD_DIM)
    mv_p = mv3.reshape(Bp, MEM_TOKENS, XA_HEADS, XA_HEAD_DIM)

    yp, sh_p, pl_p, wkv_p = _block(
        x_prompt, mk3, mv3, jnp.zeros((Bp, SHIFT_W), f32), jnp.zeros((Bp, POOL_BUF, POOL_W), f32),
        jnp.zeros((Bp, HEAD_DIM, RWKV_W), f32), 0, P, chunk=64, nseq=1)
    Bs = x_sample.shape[0]
    wkv_s0 = state_wkv.reshape(Bs, N_HEADS, HEAD_DIM, HEAD_DIM).transpose(0, 2, 1, 3).reshape(Bs, HEAD_DIM, RWKV_W)
    ys, sh_s, pl_s, wkv_s = _block(
        x_sample, cache_mem_k.reshape(cache_mem_k.shape[1:]), cache_mem_v.reshape(cache_mem_v.shape[1:]),
        state_shift.reshape(state_shift.shape[1:]), state_pool.reshape(state_pool.shape[1:]),
        wkv_s0, PAST_LEN, P, chunk=x_sample.shape[1], nseq=4)
    ex = lambda t: t[None]
    return (yp, ys, ex(wkv_p), ex(sh_p), ex(pl_p), ex(mk_p), ex(mv_p), ex(wkv_s), ex(sh_s), ex(pl_s))
```

```python
import functools

import jax
import jax.numpy as jnp
from jax import lax
from jax.experimental import pallas as pl
from jax.experimental.pallas import tpu as pltpu

D_MODEL = 4096
RWKV_W = 2048
POOL_W = 2048
MIX_W = RWKV_W + POOL_W
HEAD_DIM = 64
N_HEADS = RWKV_W // HEAD_DIM
DECAY_RANK = 96
AAA_RANK = 96
GATE_RANK = 256
SHIFT_W = 3 * RWKV_W + DECAY_RANK + AAA_RANK + GATE_RANK
POOL_WINDOWS = (2, 4, 8, 16)
N_POOL = len(POOL_WINDOWS)
POOL_GW = POOL_W // N_POOL
POOL_BUF = max(POOL_WINDOWS) - 1
POOL_CARRY = POOL_BUF + 1
MEM_TOKENS = 256
XA_HEADS = 4
XA_HEAD_DIM = 128
XA_W = XA_HEADS * XA_HEAD_DIM
PAST_LEN = 16384
RMS_EPS = 1e-6
GN_EPS = 64e-5

RKV_W = 3 * RWKV_W
LORA_RANKS = DECAY_RANK + AAA_RANK + GATE_RANK
LORA_W = 512
IN_WP = RKV_W + LORA_W + POOL_W
LANE_TILE = 128


def _tile_span(first, width):
    return (first // LANE_TILE * LANE_TILE, -(-(first + width) // LANE_TILE) * LANE_TILE)


WD_SPAN = _tile_span(0, DECAY_RANK)
AD_SPAN = _tile_span(DECAY_RANK, AAA_RANK)
GD_SPAN = _tile_span(DECAY_RANK + AAA_RANK, GATE_RANK)
NORM_SLAB = 128
XA_ROW_TILE = 256
SUB_HEADS = 4
SUB_L = SUB_HEADS * HEAD_DIM
N_PVEC = 16

VMEM_LIMIT = 56 * 1024 * 1024

_NN = ((1,), (0,))
_NT = ((1,), (1,))
_TN = ((0,), (0,))

f32 = jnp.float32
bf16 = jnp.bfloat16


def _dot(a, b, dims):
    return lax.dot_general(a, b, (dims, ((), ())), preferred_element_type=f32)


def _sigmoid(x):
    return 1.0 / (1.0 + jnp.exp(-x))


def _rms_scale(x, g):
    return x * lax.rsqrt(jnp.mean(x * x, axis=-1, keepdims=True) + RMS_EPS) * g


def _rmsnorm_kernel(x_ref, g_ref, o_ref):
    o_ref[...] = _rms_scale(x_ref[...], g_ref[...]).astype(o_ref.dtype)


def _rmsnorm(x, g, out_dtype, tm=512):
    m, d = x.shape
    return pl.pallas_call(
        _rmsnorm_kernel,
        grid=(m // tm,),
        in_specs=[pl.BlockSpec((tm, d), lambda i: (i, 0)), pl.BlockSpec((1, d), lambda i: (0, 0))],
        out_specs=pl.BlockSpec((tm, d), lambda i: (i, 0)),
        out_shape=jax.ShapeDtypeStruct((m, d), out_dtype),
        compiler_params=pltpu.CompilerParams(dimension_semantics=("parallel",), vmem_limit_bytes=VMEM_LIMIT),
        name="rmsnorm",
    )(x, g.reshape(1, d))


def _matmul_kernel(a_ref, b_ref, *rest, nk, act, has_norm, has_res):
    rest = list(rest)
    g_ref = rest.pop(0) if has_norm else None
    res_ref = rest.pop(0) if has_res else None

    def finish(r):
        if act == "relu2":
            r = jnp.square(jnp.maximum(r, 0.0))
        if has_res:
            r = r + res_ref[...]
        return r

    def prod():
        lhs = _rms_scale(a_ref[...], g_ref[...]).astype(bf16) if has_norm else a_ref[...]
        return _dot(lhs, b_ref[...].astype(bf16), _NN)

    if nk == 1:
        (o_ref,) = rest
        o_ref[...] = finish(prod()).astype(o_ref.dtype)
        return

    o_ref, acc_ref = rest
    k = pl.program_id(2)

    @pl.when(k == 0)
    def _():
        acc_ref[...] = jnp.zeros_like(acc_ref)

    acc_ref[...] += prod()

    @pl.when(k == nk - 1)
    def _():
        o_ref[...] = finish(acc_ref[...]).astype(o_ref.dtype)


def _matmul(a, b, res=None, act=None, out_dtype=f32, norm_g=None, tm=1024, tn=1024, tk=1024):
    m, kd = a.shape
    kw, n = b.shape
    tm, tn, tk = min(tm, m), min(tn, n), min(tk, kd)
    assert kw == kd and m % tm == 0 and n % tn == 0 and kd % tk == 0, (a.shape, b.shape, tm, tn, tk)
    nk = kd // tk
    a_mode = {"pipeline_mode": pl.Buffered(1)} if (nk == 1 and tm > 1024) else {}
    in_specs = [pl.BlockSpec((tm, tk), lambda i, j, k: (i, k), **a_mode),
                pl.BlockSpec((tk, tn), lambda i, j, k: (k, j))]
    args = [a, b]
    if norm_g is not None:
        assert nk == 1 and n == tn
        in_specs.append(pl.BlockSpec((1, kd), lambda i, j, k: (0, 0)))
        args.append(norm_g.reshape(1, kd))
    if res is not None:
        in_specs.append(pl.BlockSpec((tm, tn), lambda i, j, k: (i, j)))
        args.append(res)
    return pl.pallas_call(
        functools.partial(_matmul_kernel, nk=nk, act=act, has_norm=norm_g is not None, has_res=res is not None),
        grid=(m // tm, n // tn, nk),
        in_specs=in_specs,
        out_specs=pl.BlockSpec((tm, tn), lambda i, j, k: (i, j)),
        out_shape=jax.ShapeDtypeStruct((m, n), out_dtype),
        scratch_shapes=[pltpu.VMEM((tm, tn), f32)] if nk > 1 else [],
        compiler_params=pltpu.CompilerParams(
            dimension_semantics=("parallel", "parallel", "arbitrary"), vmem_limit_bytes=VMEM_LIMIT),
        name="matmul",
    )(*args)


def _norm_project_kernel(x_ref, g_ref, w_ref, o_ref, h_scr):
    @pl.when(pl.program_id(1) == 0)
    def _():
        for r0 in range(0, h_scr.shape[0], NORM_SLAB):
            rows = slice(r0, min(r0 + NORM_SLAB, h_scr.shape[0]))
            h_scr[rows, :] = _rms_scale(x_ref[rows, :], g_ref[...]).astype(bf16)

    o_ref[...] = _dot(h_scr[...], w_ref[...].astype(bf16), _NT)


def _norm_project(x, g, w_t, tm=1024, tn=512):
    m, d = x.shape
    tm = min(tm, m)
    nj_shift = (RKV_W + LORA_W) // tn
    nj = nj_shift + POOL_W // tn
    assert m % tm == 0 and (RKV_W + LORA_W) % tn == 0 and POOL_W % tn == 0 and SHIFT_W % 8 == 0 and tn % 8 == 0

    def w_rows(i, j):
        return (8 * jnp.where(j < nj_shift, j * (tn // 8), SHIFT_W // 8 + (j - nj_shift) * (tn // 8)), 0)

    return pl.pallas_call(
        _norm_project_kernel,
        grid=(m // tm, nj),
        in_specs=[pl.BlockSpec((tm, d), lambda i, j: (i, 0), pipeline_mode=pl.Buffered(1)),
                  pl.BlockSpec((1, d), lambda i, j: (0, 0)),
                  pl.BlockSpec((pl.Element(tn), pl.Element(d)), w_rows)],
        out_specs=pl.BlockSpec((tm, tn), lambda i, j: (i, j)),
        out_shape=jax.ShapeDtypeStruct((m, nj * tn), f32),
        scratch_shapes=[pltpu.VMEM((tm, d), bf16)],
        compiler_params=pltpu.CompilerParams(
            dimension_semantics=("parallel", "arbitrary"), vmem_limit_bytes=VMEM_LIMIT),
        name="norm_project",
    )(x, g.reshape(1, d), w_t)


PV_MU_R, PV_MU_K, PV_MU_V, PV_W0, PV_A0, PV_KK, PV_KA, PV_RK, PV_LNG, PV_LNB = range(10)


def _pool_group(x, carry_ref, w, scale, pos1, wdw, C):
    ext = jnp.concatenate([carry_ref[...], x], axis=0)
    carry_ref[...] = ext[C:, :]
    acc, span = ext, 1
    while span < wdw:
        acc = acc + pltpu.roll(acc, span, 0)
        span *= 2
    mean = acc[POOL_CARRY:, :] / jnp.minimum(wdw, pos1).astype(f32)
    return _dot((mean - x).astype(bf16), w, _NN) * scale


def _mixer_kernel(zr_ref, zk_ref, zv_ref, zl_ref, prkv_ref, pl_ref, pv_ref, mul_ref, wdu_ref, wau_ref, wgu_ref,
                  s0_ref, zp0_ref, zp1_ref, zp2_ref, zp3_ref, pprev_ref, wp_ref, psc_ref,
                  o_ref, sout_ref, s_scr, hm_scr, tm_scr, ones_scr, carry_scr, pcarry_scr,
                  *, chunk, heads, nsub, nseq, nchunks, start_pos):
    C, Hg = chunk, heads
    L, R = Hg * HEAD_DIM, Hg * C
    Lb, Hb = nsub * L, nsub * Hg
    rows = nseq * C
    c = pl.program_id(1)

    @pl.when(c == 0)
    def _():
        for bi in range(nseq):
            s_scr[bi] = s0_ref[bi]
            for i in range(3):
                carry_scr[bi, i, 0:1, :Lb] = prkv_ref[bi, i:i + 1, :]
            carry_scr[bi, 3, 0:1, :LORA_W] = pl_ref[bi]
            for gi in range(N_POOL):
                pcarry_scr[bi, gi, 0:1, :] = jnp.zeros((1, POOL_GW), f32)
                pcarry_scr[bi, gi, 1:POOL_CARRY, :] = pprev_ref[bi, :, gi * POOL_GW:(gi + 1) * POOL_GW]
        hm_scr[...] = ((lax.broadcasted_iota(jnp.int32, (R, L), 0) // C)
                       == (lax.broadcasted_iota(jnp.int32, (R, L), 1) // HEAD_DIM)).astype(bf16)
        row = lax.broadcasted_iota(jnp.int32, (R, R), 0)
        col = lax.broadcasted_iota(jnp.int32, (R, R), 1)
        same = (row // C) == (col // C)
        tm_scr[0] = (same & ((row % C) < (col % C))).astype(bf16)
        tm_scr[1] = (same & ((row % C) <= (col % C))).astype(bf16)
        ones_scr[...] = ((lax.broadcasted_iota(jnp.int32, (SUB_L, SUB_L), 0) // HEAD_DIM)
                         == (lax.broadcasted_iota(jnp.int32, (SUB_L, SUB_L), 1) // HEAD_DIM)).astype(bf16)

    def mixed(z_ref, idx, width, mu):
        z = z_ref[...].reshape(rows, width)
        first = jnp.concatenate([jnp.broadcast_to(carry_scr[bi, idx, 0:1, :width], (C, width))
                                 for bi in range(nseq)], axis=0)
        prev = jnp.where(lax.broadcasted_iota(jnp.int32, z.shape, 0) % C == 0, first, pltpu.roll(z, 1, 0))
        for bi in range(nseq):
            carry_scr[bi, idx, 0:1, :width] = z[(bi + 1) * C - 1:(bi + 1) * C, :]
        return z + mu * (prev - z)

    pv = lambda i: pv_ref[i:i + 1, :]
    r = mixed(zr_ref, 0, Lb, pv(PV_MU_R))
    k = mixed(zk_ref, 1, Lb, pv(PV_MU_K))
    v = mixed(zv_ref, 2, Lb, pv(PV_MU_V))
    zl = mixed(zl_ref, 3, LORA_W, mul_ref[...])

    pos1 = start_pos + 1 + c * C + lax.broadcasted_iota(jnp.int32, (C, POOL_GW), 0)
    for bi in range(nseq):
        for gi, (zp_ref, wdw) in enumerate(zip((zp0_ref, zp1_ref, zp2_ref, zp3_ref), POOL_WINDOWS)):
            cols = slice(gi * POOL_GW, (gi + 1) * POOL_GW)
            o_pool = _pool_group(zp_ref[bi], pcarry_scr.at[bi, gi], wp_ref[gi], psc_ref[:, cols], pos1, wdw, C)
            o_ref[bi, :, RWKV_W + gi * POOL_GW:RWKV_W + (gi + 1) * POOL_GW] = o_pool.astype(o_ref.dtype)

    wd_in = jnp.tanh(zl[:, WD_SPAN[0]:WD_SPAN[1]]).astype(bf16)
    dec = pv(PV_W0) + _dot(wd_in, wdu_ref[...], _NN)
    sp = jnp.maximum(-dec, 0.0) + jnp.log(1.0 + jnp.exp(-jnp.abs(dec)))
    lw_all = -jnp.exp(-sp - 0.5)
    a = _sigmoid(pv(PV_A0) + _dot(zl[:, AD_SPAN[0]:AD_SPAN[1]].astype(bf16), wau_ref[...], _NN))
    gate = _dot(_sigmoid(zl[:, GD_SPAN[0]:GD_SPAN[1]]).astype(bf16), wgu_ref[...], _NN)

    ones = ones_scr[...]

    def headsum(x):
        npc = Lb // SUB_L
        xs = jnp.concatenate([x[:, p * SUB_L:(p + 1) * SUB_L] for p in range(npc)], axis=0)
        sums = _dot(xs.astype(bf16), ones, _NN)
        return jnp.concatenate([sums[p * rows:(p + 1) * rows, :] for p in range(npc)], axis=1)

    kkr = k * pv(PV_KK)
    kk_all = kkr * lax.rsqrt(jnp.maximum(headsum(kkr * kkr), 1e-24))
    kf_all = k * (1.0 + (a - 1.0) * pv(PV_KA))
    kka_all = kk_all * a

    tri = (lax.broadcasted_iota(jnp.int32, (C, C), 1) <= lax.broadcasted_iota(jnp.int32, (C, C), 0)).astype(bf16)
    hm = hm_scr[...]
    strict, incl = tm_scr[0], tm_scr[1]

    def stack16(x):
        if C % 16 == 0:
            return jnp.concatenate([x.astype(bf16)] * Hg, axis=0)
        return jnp.concatenate([x] * Hg, axis=0).astype(bf16)

    chains = [(bi, jg) for bi in range(nseq) for jg in range(nsub)]
    subs = range(len(chains))
    seq_rows = [slice(bi * C, (bi + 1) * C) for bi, _ in chains]
    lanes = [slice(jg * L, (jg + 1) * L) for _, jg in chains]
    part = lambda x, j: x[seq_rows[j], lanes[j]]

    def cumsum_t(lw):
        h1 = lw.astype(bf16)
        r1 = lw - h1.astype(f32)
        h2 = r1.astype(bf16)
        h3 = (r1 - h2.astype(f32)).astype(bf16)
        return _dot(tri, h1, _NN) + (_dot(tri, h2, _NN) + _dot(tri, h3, _NN))

    lw = [part(lw_all, j) for j in subs]
    cum = [cumsum_t(lw[j]) for j in subs]
    g = [jnp.exp(cum[j]) for j in subs]
    gi = [jnp.exp(-cum[j]) for j in subs]
    gprev = [jnp.exp(cum[j] - lw[j]) for j in subs]
    g_end = [g[j][C - 1:C, :] for j in subs]
    kk = [part(kk_all, j) for j in subs]
    kka = [part(kka_all, j) for j in subs]
    kf = [part(kf_all, j) for j in subs]
    lhs = [jnp.concatenate([stack16(-kk[j] * gprev[j]) * hm, stack16(part(r, j) * g[j]) * hm], axis=0)
           for j in subs]
    rhs = [jnp.concatenate([stack16(kka[j] * gi[j]), stack16(kf[j] * gi[j])], axis=0) for j in subs]
    res = [_dot(rhs[j], lhs[j], _NT).astype(bf16) for j in subs]
    mt = [res[j][:R, :R] * strict for j in subs]
    pt = [res[j][:R, R:] * incl for j in subs]
    nqt = [jnp.concatenate([res[j][R:, :R] * strict, res[j][R:, R:] * incl], axis=1) for j in subs]

    def rows_by_head(x):
        return jnp.concatenate([x[:, h * HEAD_DIM:(h + 1) * HEAD_DIM] for h in range(Hg)], axis=0)

    def lanes_by_head(x):
        return jnp.concatenate([x[h * C:(h + 1) * C, :] for h in range(Hg)], axis=1)

    vs = [rows_by_head(part(v, j)).astype(bf16) for j in subs]
    s = [s_scr[bi, :, lanes[j]] for j, (bi, _) in enumerate(chains)]
    base = [_dot(s[j].astype(bf16), lhs[j], _NT) + _dot(vs[j], nqt[j], _TN) for j in subs]

    x = [base[j][:, :R] for j in subs]
    mp = mt
    nsq = C.bit_length() - 1
    for i in range(nsq):
        if i < nsq - 1:
            both = [_dot(jnp.concatenate([x[j].astype(bf16), mp[j]], axis=0), mp[j], _NN) for j in subs]
            x = [x[j] + both[j][:HEAD_DIM] for j in subs]
            mp = [both[j][HEAD_DIM:].astype(bf16) for j in subs]
        else:
            x = [x[j] + _dot(x[j].astype(bf16), mp[j], _NN) for j in subs]
    u = [x[j].astype(bf16) for j in subs]

    y = [lanes_by_head((base[j][:, R:] + _dot(u[j], pt[j], _NN)).T) for j in subs]
    for j, (bi, _) in enumerate(chains):
        gl = g_end[j] * gi[j]
        s_scr[bi, :, lanes[j]] = (g_end[j] * s[j] + _dot(u[j], stack16(kka[j] * gl) * hm, _NN)
                                  + _dot(vs[j], stack16(kf[j] * gl) * hm, _TN))
    y = jnp.concatenate([jnp.concatenate(y[bi * nsub:(bi + 1) * nsub], axis=-1) for bi in range(nseq)],
                        axis=0)

    mean = headsum(y) * (1.0 / HEAD_DIM)
    d = y - mean
    var = headsum(d * d) * (1.0 / HEAD_DIM)
    yn = d * lax.rsqrt(var + GN_EPS) * pv(PV_LNG) + pv(PV_LNB)
    bonus = headsum(r * kf_all * pv(PV_RK)) * v
    o_rwkv = ((yn + bonus) * gate).astype(o_ref.dtype)
    for bi in range(nseq):
        o_ref[bi, :, :RWKV_W] = o_rwkv[bi * C:(bi + 1) * C, :]

    @pl.when(c == nchunks - 1)
    def _():
        for bi in range(nseq):
            for h in range(Hb):
                sout_ref[bi, h] = s_scr[bi, :, h * HEAD_DIM:(h + 1) * HEAD_DIM]


def _mixer(z, shift_prev, pool_prev, s0, P, *, start_pos, chunk, heads, nsub, nseq):
    B, T, _ = z.shape
    C, Hg = chunk, heads
    L, R = Hg * HEAD_DIM, Hg * C
    assert nsub * Hg == N_HEADS and B % nseq == 0 and T % C == 0
    nchunks = T // C
    prkv = shift_prev[:, :RKV_W].reshape(B, 3, RWKV_W)
    plora = jnp.pad(shift_prev[:, RKV_W:], ((0, 0), (0, LORA_W - LORA_RANKS))).reshape(B, 1, LORA_W)
    zcol = lambda w, j: pl.BlockSpec((nseq, C, w), lambda b, c: (b, c, j))
    whole = lambda shape: pl.BlockSpec(shape, lambda b, c: (0,) * len(shape))
    per_b = lambda shape: pl.BlockSpec((nseq,) + shape, lambda b, c: (b,) + (0,) * len(shape))
    st = per_b((N_HEADS, HEAD_DIM, HEAD_DIM))
    pool0 = (IN_WP - POOL_W) // POOL_GW
    return pl.pallas_call(
        functools.partial(_mixer_kernel, chunk=C, heads=Hg, nsub=nsub, nseq=nseq, nchunks=nchunks,
                          start_pos=start_pos),
        grid=(B // nseq, nchunks),
        in_specs=[zcol(RWKV_W, 0), zcol(RWKV_W, 1), zcol(RWKV_W, 2), zcol(LORA_W, RKV_W // LORA_W),
                  per_b((3, RWKV_W)), per_b((1, LORA_W)), whole((N_PVEC, RWKV_W)), whole((1, LORA_W)),
                  whole((WD_SPAN[1] - WD_SPAN[0], RWKV_W)), whole((AD_SPAN[1] - AD_SPAN[0], RWKV_W)),
                  whole((GD_SPAN[1] - GD_SPAN[0], RWKV_W)), per_b((HEAD_DIM, RWKV_W)),
                  zcol(POOL_GW, pool0), zcol(POOL_GW, pool0 + 1), zcol(POOL_GW, pool0 + 2), zcol(POOL_GW, pool0 + 3),
                  per_b((POOL_BUF, POOL_W)), whole((N_POOL, POOL_GW, POOL_GW)), whole((1, POOL_W))],
        out_specs=[pl.BlockSpec((nseq, C, MIX_W), lambda b, c: (b, c, 0)), st],
        out_shape=[jax.ShapeDtypeStruct((B, T, MIX_W), bf16),
                   jax.ShapeDtypeStruct((B, N_HEADS, HEAD_DIM, HEAD_DIM), f32)],
        scratch_shapes=[pltpu.VMEM((nseq, HEAD_DIM, RWKV_W), f32), pltpu.VMEM((R, L), bf16),
                        pltpu.VMEM((2, R, R), bf16), pltpu.VMEM((SUB_L, SUB_L), bf16),
                        pltpu.VMEM((nseq, 4, 8, RWKV_W), f32),
                        pltpu.VMEM((nseq, N_POOL, POOL_CARRY, POOL_GW), f32)],
        compiler_params=pltpu.CompilerParams(
            dimension_semantics=("parallel", "arbitrary"), vmem_limit_bytes=VMEM_LIMIT),
        name="mixer",
    )(z, z, z, z, prkv, plora, P["pvec"], P["mu_lora"], P["w_decay_up"], P["w_a_up"], P["w_g_up"], s0,
      z, z, z, z, pool_prev, P["w_pool"], P["pool_scale"])


def _attend(q, k_ref, v_ref, head_axis):
    outs = []
    for h in range(XA_HEADS):
        sl = slice(h * XA_HEAD_DIM, (h + 1) * XA_HEAD_DIM)
        if head_axis:
            kh, vh = k_ref[0, :, h, :], v_ref[0, :, h, :]
        else:
            kh, vh = k_ref[0, :, sl], v_ref[0, :, sl]
        s = _dot(q[:, sl], kh.astype(bf16), _NT) * (XA_HEAD_DIM ** -0.5)
        e = jnp.exp(s - jnp.max(s, axis=-1, keepdims=True))
        v1 = jnp.concatenate([vh.astype(bf16), jnp.ones((MEM_TOKENS, XA_HEAD_DIM), bf16)], axis=1)
        ov = _dot(e.astype(bf16), v1, _NN)
        outs.append(ov[:, :XA_HEAD_DIM] / ov[:, XA_HEAD_DIM:])
    return jnp.concatenate(outs, axis=-1)


def _xattn_kernel(q_ref, k_ref, v_ref, o_ref, *, head_axis):
    o_ref[0] = _attend(q_ref[0], k_ref, v_ref, head_axis).astype(o_ref.dtype)


def _xa_layer_kernel(x_ref, g_ref, wq_ref, k_ref, v_ref, wo_ref, gf_ref, o_ref, hn_ref):
    x = x_ref[0]
    q = _dot(_rms_scale(x, g_ref[...]).astype(bf16), wq_ref[...], _NN).astype(bf16)
    o = _attend(q, k_ref, v_ref, False).astype(bf16)
    x = x + _dot(o, wo_ref[...], _NN)
    o_ref[0] = x
    hn_ref[0] = _rms_scale(x, gf_ref[...]).astype(hn_ref.dtype)


def _xa_layer(x, mk, mv, P, tq):
    B, T, D = x.shape
    row = pl.BlockSpec((1, tq, D), lambda b, i: (b, i, 0))
    kv = pl.BlockSpec((1, MEM_TOKENS, XA_W), lambda b, i: (b, 0, 0))
    whole = lambda shape: pl.BlockSpec(shape, lambda b, i: (0, 0))
    return pl.pallas_call(
        _xa_layer_kernel,
        grid=(B, T // tq),
        in_specs=[row, whole((1, D)), whole((D, XA_W)), kv, kv, whole((XA_W, D)), whole((1, D))],
        out_specs=[row, row],
        out_shape=[jax.ShapeDtypeStruct((B, T, D), f32), jax.ShapeDtypeStruct((B, T, D), bf16)],
        compiler_params=pltpu.CompilerParams(
            dimension_semantics=("parallel", "parallel"), vmem_limit_bytes=VMEM_LIMIT),
        name="xa_layer",
    )(x, P["norm_xa_g"].reshape(1, D), P["w_xq"].astype(bf16), mk, mv, P["w_xo"].astype(bf16),
      P["norm_ffn_g"].reshape(1, D))


def _xattn(q, mk, mv, tq):
    B, T, _ = q.shape
    tq = min(tq, T)
    head_axis = mk.ndim == 4
    if head_axis:
        kv_spec = pl.BlockSpec((1, MEM_TOKENS, XA_HEADS, XA_HEAD_DIM), lambda b, i: (b, 0, 0, 0))
    else:
        kv_spec = pl.BlockSpec((1, MEM_TOKENS, XA_W), lambda b, i: (b, 0, 0))
    return pl.pallas_call(
        functools.partial(_xattn_kernel, head_axis=head_axis),
        grid=(B, T // tq),
        in_specs=[pl.BlockSpec((1, tq, XA_W), lambda b, i: (b, i, 0)), kv_spec, kv_spec],
        out_specs=pl.BlockSpec((1, tq, XA_W), lambda b, i: (b, i, 0)),
        out_shape=jax.ShapeDtypeStruct((B, T, XA_W), bf16),
        compiler_params=pltpu.CompilerParams(
            dimension_semantics=("parallel", "parallel"), vmem_limit_bytes=VMEM_LIMIT),
        name="xattn",
    )(q, mk, mv)


def _block(x, mk, mv, shift_prev, pool_prev, wkv_prev, start_pos, P, chunk, nseq):
    B, T, _ = x.shape
    M = B * T
    x2 = x.reshape(M, D_MODEL)

    z = _norm_project(x2, P["norm_mix_g"], P["w_in_t"]).reshape(B, T, IN_WP)
    mix, s_new = _mixer(z, shift_prev, pool_prev, wkv_prev, P, start_pos=start_pos, chunk=chunk,
                        heads=SUB_HEADS, nsub=N_HEADS // SUB_HEADS, nseq=nseq)
    x2 = _matmul(mix.reshape(M, MIX_W), P["w_out"], res=x2, tm=2048, tn=512, tk=MIX_W)
    new_shift = z[:, -1, :SHIFT_W]
    new_pool = jnp.concatenate([pool_prev, z[:, :, IN_WP - POOL_W:]], axis=1)[:, -POOL_BUF:]

    if mk.ndim == 3 and T % XA_ROW_TILE == 0:
        x3, h3 = _xa_layer(x2.reshape(B, T, D_MODEL), mk, mv, P, XA_ROW_TILE)
        x2, h = x3.reshape(M, D_MODEL), h3.reshape(M, D_MODEL)
    else:
        q = _matmul(x2, P["w_xq"], norm_g=P["norm_xa_g"], out_dtype=bf16, tm=512, tn=XA_W, tk=D_MODEL)
        o = _xattn(q.reshape(B, T, XA_W), mk, mv, tq=512)
        x2 = _matmul(o.reshape(M, XA_W), P["w_xo"], res=x2, tk=512)
        h = _rmsnorm(x2, P["norm_ffn_g"], bf16)

    hid = _matmul(h, P["w_up"], act="relu2", out_dtype=bf16, tm=2048, tn=512, tk=D_MODEL)
    x2 = _matmul(hid, P["w_down"], res=x2, tk=2048)
    y_out = _rmsnorm(x2, P["norm_final_g"], f32).reshape(B, T, D_MODEL)

    return y_out, new_shift, new_pool, s_new


def _params(norm_mix_g, w_in, mu_shift, w0, w_decay_up, a0, w_a_up, w_g_up, k_k, k_a, r_k, lnx_g, lnx_b, w_pool,
            pool_scale, w_out, norm_xa_g, norm_mem_g, w_xq, w_mk, w_mv, w_xo, norm_ffn_g, w_up, w_down,
            norm_final_g):
    l = 0
    mu = mu_shift[l]
    pvec = jnp.stack([mu[:RWKV_W], mu[RWKV_W:2 * RWKV_W], mu[2 * RWKV_W:RKV_W], w0[l], a0[l], k_k[l],
                      k_a[l], r_k[l].reshape(RWKV_W), lnx_g[l], lnx_b[l]])
    pvec = jnp.concatenate([pvec, jnp.zeros((N_PVEC - pvec.shape[0], RWKV_W), f32)], axis=0)

    def lora_rows(w, first, span):
        return jnp.pad(w, ((first - span[0], span[1] - first - w.shape[0]), (0, 0))).astype(bf16)

    return {
        "norm_mix_g": norm_mix_g[l], "w_in_t": jnp.transpose(w_in[l]),
        "pvec": pvec, "mu_lora": jnp.pad(mu[RKV_W:], (0, LORA_W - LORA_RANKS)).reshape(1, LORA_W),
        "w_decay_up": lora_rows(w_decay_up[l], 0, WD_SPAN), "w_a_up": lora_rows(w_a_up[l], DECAY_RANK, AD_SPAN),
        "w_g_up": lora_rows(w_g_up[l], DECAY_RANK + AAA_RANK, GD_SPAN),
        "w_pool": w_pool[l].astype(bf16), "pool_scale": pool_scale[l].reshape(1, POOL_W), "w_out": w_out[l],
        "norm_xa_g": norm_xa_g[l], "w_xq": w_xq[l], "w_xo": w_xo[l],
        "norm_ffn_g": norm_ffn_g[l], "w_up": w_up[l], "w_down": w_down[l],
        "norm_final_g": norm_final_g, "norm_mem_g": norm_mem_g[l], "w_mk": w_mk[l], "w_mv": w_mv[l],
    }


def kernel(x_prompt, x_sample, mem_prompt, state_wkv, state_shift, state_pool, cache_mem_k, cache_mem_v, norm_mix_g, w_in, mu_shift, w0, w_decay_up, a0, w_a_up, w_g_up, k_k, k_a, r_k, lnx_g, lnx_b, w_pool, pool_scale, w_out, norm_xa_g, norm_mem_g, w_xq, w_mk, w_mv, w_xo, norm_ffn_g, w_up, w_down, norm_final_g):
    Bp = x_prompt.shape[0]
    P = _params(norm_mix_g, w_in, mu_shift, w0, w_decay_up, a0, w_a_up, w_g_up, k_k, k_a, r_k, lnx_g, lnx_b, w_pool,
                pool_scale, w_out, norm_xa_g, norm_mem_g, w_xq, w_mk, w_mv, w_xo, norm_ffn_g, w_up, w_down,
                norm_final_g)
    mem = _rmsnorm(mem_prompt.reshape(Bp * MEM_TOKENS, D_MODEL), P["norm_mem_g"], bf16)
    mk3 = _matmul(mem, P["w_mk"], tn=512, tk=D_MODEL).reshape(Bp, MEM_TOKENS, XA_W)
    mv3 = _matmul(mem, P["w_mv"], tn=512, tk=D_MODEL).reshape(Bp, MEM_TOKENS, XA_W)
    mk_p = mk3.reshape(Bp, MEM_TOKENS, XA_HEADS, XA_HEAD_DIM)
    mv_p = mv3.reshape(Bp, MEM_TOKENS, XA_HEADS, XA_HEAD_DIM)

    yp, sh_p, pl_p, wkv_p = _block(
        x_prompt, mk3, mv3, jnp.zeros((Bp, SHIFT_W), f32), jnp.zeros((Bp, POOL_BUF, POOL_W), f32),
        jnp.zeros((Bp, HEAD_DIM, RWKV_W), f32), 0, P, chunk=64, nseq=1)
    Bs = x_sample.shape[0]
    wkv_s0 = state_wkv.reshape(Bs, N_HEADS, HEAD_DIM, HEAD_DIM).transpose(0, 2, 1, 3).reshape(Bs, HEAD_DIM, RWKV_W)
    ys, sh_s, pl_s, wkv_s = _block(
        x_sample, cache_mem_k.reshape(cache_mem_k.shape[1:]), cache_mem_v.reshape(cache_mem_v.shape[1:]),
        state_shift.reshape(state_shift.shape[1:]), state_pool.reshape(state_pool.shape[1:]),
        wkv_s0, PAST_LEN, P, chunk=x_sample.shape[1], nseq=4)
    ex = lambda t: t[None]
    return (yp, ys, ex(wkv_p), ex(sh_p), ex(pl_p), ex(mk_p), ex(mv_p), ex(wkv_s), ex(sh_s), ex(pl_s))
```

```python
import functools

import jax
import jax.numpy as jnp
from jax import lax
from jax.experimental import pallas as pl
from jax.experimental.pallas import tpu as pltpu

D_MODEL = 4096
RWKV_W = 2048
POOL_W = 2048
MIX_W = RWKV_W + POOL_W
HEAD_DIM = 64
N_HEADS = RWKV_W // HEAD_DIM
DECAY_RANK = 96
AAA_RANK = 96
GATE_RANK = 256
SHIFT_W = 3 * RWKV_W + DECAY_RANK + AAA_RANK + GATE_RANK
POOL_WINDOWS = (2, 4, 8, 16)
N_POOL = len(POOL_WINDOWS)
POOL_GW = POOL_W // N_POOL
POOL_BUF = max(POOL_WINDOWS) - 1
POOL_CARRY = POOL_BUF + 1
MEM_TOKENS = 256
XA_HEADS = 4
XA_HEAD_DIM = 128
XA_W = XA_HEADS * XA_HEAD_DIM
PAST_LEN = 16384
RMS_EPS = 1e-6
GN_EPS = 64e-5

RKV_W = 3 * RWKV_W
LORA_RANKS = DECAY_RANK + AAA_RANK + GATE_RANK
LORA_W = 512
IN_WP = RKV_W + LORA_W + POOL_W
LANE_TILE = 128


def _tile_span(first, width):
    return (first // LANE_TILE * LANE_TILE, -(-(first + width) // LANE_TILE) * LANE_TILE)


WD_SPAN = _tile_span(0, DECAY_RANK)
AD_SPAN = _tile_span(DECAY_RANK, AAA_RANK)
GD_SPAN = _tile_span(DECAY_RANK + AAA_RANK, GATE_RANK)
NORM_SLAB = 128
XA_ROW_TILE = 256
SUB_HEADS = 4
SUB_L = SUB_HEADS * HEAD_DIM
N_PVEC = 16

VMEM_LIMIT = 56 * 1024 * 1024

_NN = ((1,), (0,))
_NT = ((1,), (1,))
_TN = ((0,), (0,))

f32 = jnp.float32
bf16 = jnp.bfloat16


def _dot(a, b, dims):
    return lax.dot_general(a, b, (dims, ((), ())), preferred_element_type=f32)


def _sigmoid(x):
    return 1.0 / (1.0 + jnp.exp(-x))


def _rms_scale(x, g):
    return x * lax.rsqrt(jnp.mean(x * x, axis=-1, keepdims=True) + RMS_EPS) * g


def _rmsnorm_kernel(x_ref, g_ref, o_ref):
    o_ref[...] = _rms_scale(x_ref[...], g_ref[...]).astype(o_ref.dtype)


def _rmsnorm(x, g, out_dtype, tm=512):
    m, d = x.shape
    return pl.pallas_call(
        _rmsnorm_kernel,
        grid=(m // tm,),
        in_specs=[pl.BlockSpec((tm, d), lambda i: (i, 0)), pl.BlockSpec((1, d), lambda i: (0, 0))],
        out_specs=pl.BlockSpec((tm, d), lambda i: (i, 0)),
        out_shape=jax.ShapeDtypeStruct((m, d), out_dtype),
        compiler_params=pltpu.CompilerParams(dimension_semantics=("parallel",), vmem_limit_bytes=VMEM_LIMIT),
        name="rmsnorm",
    )(x, g.reshape(1, d))


def _matmul_kernel(a_ref, b_ref, *rest, nk, act, has_norm, has_res):
    rest = list(rest)
    g_ref = rest.pop(0) if has_norm else None
    res_ref = rest.pop(0) if has_res else None

    def finish(r):
        if act == "relu2":
            r = jnp.square(jnp.maximum(r, 0.0))
        if has_res:
            r = r + res_ref[...]
        return r

    def prod():
        lhs = _rms_scale(a_ref[...], g_ref[...]).astype(bf16) if has_norm else a_ref[...]
        return _dot(lhs, b_ref[...].astype(bf16), _NN)

    if nk == 1:
        (o_ref,) = rest
        o_ref[...] = finish(prod()).astype(o_ref.dtype)
        return

    o_ref, acc_ref = rest
    k = pl.program_id(2)

    @pl.when(k == 0)
    def _():
        acc_ref[...] = jnp.zeros_like(acc_ref)

    acc_ref[...] += prod()

    @pl.when(k == nk - 1)
    def _():
        o_ref[...] = finish(acc_ref[...]).astype(o_ref.dtype)


def _matmul(a, b, res=None, act=None, out_dtype=f32, norm_g=None, tm=1024, tn=1024, tk=1024):
    m, kd = a.shape
    kw, n = b.shape
    tm, tn, tk = min(tm, m), min(tn, n), min(tk, kd)
    assert kw == kd and m % tm == 0 and n % tn == 0 and kd % tk == 0, (a.shape, b.shape, tm, tn, tk)
    nk = kd // tk
    a_mode = {"pipeline_mode": pl.Buffered(1)} if (nk == 1 and tm > 1024) else {}
    in_specs = [pl.BlockSpec((tm, tk), lambda i, j, k: (i, k), **a_mode),
                pl.BlockSpec((tk, tn), lambda i, j, k: (k, j))]
    args = [a, b]
    if norm_g is not None:
        assert nk == 1 and n == tn
        in_specs.append(pl.BlockSpec((1, kd), lambda i, j, k: (0, 0)))
        args.append(norm_g.reshape(1, kd))
    if res is not None:
        in_specs.append(pl.BlockSpec((tm, tn), lambda i, j, k: (i, j)))
        args.append(res)
    return pl.pallas_call(
        functools.partial(_matmul_kernel, nk=nk, act=act, has_norm=norm_g is not None, has_res=res is not None),
        grid=(m // tm, n // tn, nk),
        in_specs=in_specs,
        out_specs=pl.BlockSpec((tm, tn), lambda i, j, k: (i, j)),
        out_shape=jax.ShapeDtypeStruct((m, n), out_dtype),
        scratch_shapes=[pltpu.VMEM((tm, tn), f32)] if nk > 1 else [],
        compiler_params=pltpu.CompilerParams(
            dimension_semantics=("parallel", "parallel", "arbitrary"), vmem_limit_bytes=VMEM_LIMIT),
        name="matmul",
    )(*args)


def _norm_project_kernel(x_ref, g_ref, w_ref, o_ref, h_scr):
    @pl.when(pl.program_id(1) == 0)
    def _():
        for r0 in range(0, h_scr.shape[0], NORM_SLAB):
            rows = slice(r0, min(r0 + NORM_SLAB, h_scr.shape[0]))
            h_scr[rows, :] = _rms_scale(x_ref[rows, :], g_ref[...]).astype(bf16)

    o_ref[...] = _dot(h_scr[...], w_ref[...].astype(bf16), _NT)


def _norm_project(x, g, w_t, tm=1024, tn=512):
    m, d = x.shape
    tm = min(tm, m)
    nj_shift = (RKV_W + LORA_W) // tn
    nj = nj_shift + POOL_W // tn
    assert m % tm == 0 and (RKV_W + LORA_W) % tn == 0 and POOL_W % tn == 0 and SHIFT_W % 8 == 0 and tn % 8 == 0

    def w_rows(i, j):
        return (8 * jnp.where(j < nj_shift, j * (tn // 8), SHIFT_W // 8 + (j - nj_shift) * (tn // 8)), 0)

    return pl.pallas_call(
        _norm_project_kernel,
        grid=(m // tm, nj),
        in_specs=[pl.BlockSpec((tm, d), lambda i, j: (i, 0), pipeline_mode=pl.Buffered(1)),
                  pl.BlockSpec((1, d), lambda i, j: (0, 0)),
                  pl.BlockSpec((pl.Element(tn), pl.Element(d)), w_rows)],
        out_specs=pl.BlockSpec((tm, tn), lambda i, j: (i, j)),
        out_shape=jax.ShapeDtypeStruct((m, nj * tn), f32),
        scratch_shapes=[pltpu.VMEM((tm, d), bf16)],
        compiler_params=pltpu.CompilerParams(
            dimension_semantics=("parallel", "arbitrary"), vmem_limit_bytes=VMEM_LIMIT),
        name="norm_project",
    )(x, g.reshape(1, d), w_t)


PV_MU_R, PV_MU_K, PV_MU_V, PV_W0, PV_A0, PV_KK, PV_KA, PV_RK, PV_LNG, PV_LNB = range(10)


def _pool_group(x, carry_ref, w, scale, pos1, wdw, C):
    ext = jnp.concatenate([carry_ref[...], x], axis=0)
    carry_ref[...] = ext[C:, :]
    acc, span = ext, 1
    while span < wdw:
        acc = acc + pltpu.roll(acc, span, 0)
        span *= 2
    mean = acc[POOL_CARRY:, :] / jnp.minimum(wdw, pos1).astype(f32)
    return _dot((mean - x).astype(bf16), w, _NN) * scale


def _mixer_kernel(zr_ref, zk_ref, zv_ref, zl_ref, prkv_ref, pl_ref, pv_ref, mul_ref, wdu_ref, wau_ref, wgu_ref,
                  s0_ref, zp0_ref, zp1_ref, zp2_ref, zp3_ref, pprev_ref, wp_ref, psc_ref,
                  o_ref, sout_ref, s_scr, hm_scr, tm_scr, ones_scr, carry_scr, pcarry_scr,
                  *, chunk, heads, nsub, nseq, nchunks, start_pos):
    C, Hg = chunk, heads
    L, R = Hg * HEAD_DIM, Hg * C
    Lb, Hb = nsub * L, nsub * Hg
    rows = nseq * C
    c = pl.program_id(1)

    @pl.when(c == 0)
    def _():
        for bi in range(nseq):
            s_scr[bi] = s0_ref[bi]
            for i in range(3):
                carry_scr[bi, i, 0:1, :Lb] = prkv_ref[bi, i:i + 1, :]
            carry_scr[bi, 3, 0:1, :LORA_W] = pl_ref[bi]
            for gi in range(N_POOL):
                pcarry_scr[bi, gi, 0:1, :] = jnp.zeros((1, POOL_GW), f32)
                pcarry_scr[bi, gi, 1:POOL_CARRY, :] = pprev_ref[bi, :, gi * POOL_GW:(gi + 1) * POOL_GW]
        hm_scr[...] = ((lax.broadcasted_iota(jnp.int32, (R, L), 0) // C)
                       == (lax.broadcasted_iota(jnp.int32, (R, L), 1) // HEAD_DIM)).astype(bf16)
        row = lax.broadcasted_iota(jnp.int32, (R, R), 0)
        col = lax.broadcasted_iota(jnp.int32, (R, R), 1)
        same = (row // C) == (col // C)
        tm_scr[0] = (same & ((row % C) < (col % C))).astype(bf16)
        tm_scr[1] = (same & ((row % C) <= (col % C))).astype(bf16)
        ones_scr[...] = ((lax.broadcasted_iota(jnp.int32, (SUB_L, SUB_L), 0) // HEAD_DIM)
                         == (lax.broadcasted_iota(jnp.int32, (SUB_L, SUB_L), 1) // HEAD_DIM)).astype(bf16)

    def mixed(z_ref, idx, width, mu):
        z = z_ref[...].reshape(rows, width)
        first = jnp.concatenate([jnp.broadcast_to(carry_scr[bi, idx, 0:1, :width], (C, width))
                                 for bi in range(nseq)], axis=0)
        prev = jnp.where(lax.broadcasted_iota(jnp.int32, z.shape, 0) % C == 0, first, pltpu.roll(z, 1, 0))
        for bi in range(nseq):
            carry_scr[bi, idx, 0:1, :width] = z[(bi + 1) * C - 1:(bi + 1) * C, :]
        return z + mu * (prev - z)

    pv = lambda i: pv_ref[i:i + 1, :]
    r = mixed(zr_ref, 0, Lb, pv(PV_MU_R))
    k = mixed(zk_ref, 1, Lb, pv(PV_MU_K))
    v = mixed(zv_ref, 2, Lb, pv(PV_MU_V))
    zl = mixed(zl_ref, 3, LORA_W, mul_ref[...])

    pos1 = start_pos + 1 + c * C + lax.broadcasted_iota(jnp.int32, (C, POOL_GW), 0)
    for bi in range(nseq):
        for gi, (zp_ref, wdw) in enumerate(zip((zp0_ref, zp1_ref, zp2_ref, zp3_ref), POOL_WINDOWS)):
            cols = slice(gi * POOL_GW, (gi + 1) * POOL_GW)
            o_pool = _pool_group(zp_ref[bi], pcarry_scr.at[bi, gi], wp_ref[gi], psc_ref[:, cols], pos1, wdw, C)
            o_ref[bi, :, RWKV_W + gi * POOL_GW:RWKV_W + (gi + 1) * POOL_GW] = o_pool.astype(o_ref.dtype)

    wd_in = jnp.tanh(zl[:, WD_SPAN[0]:WD_SPAN[1]]).astype(bf16)
    dec = pv(PV_W0) + _dot(wd_in, wdu_ref[...], _NN)
    sp = jnp.maximum(-dec, 0.0) + jnp.log(1.0 + jnp.exp(-jnp.abs(dec)))
    lw_all = -jnp.exp(-sp - 0.5)
    a = _sigmoid(pv(PV_A0) + _dot(zl[:, AD_SPAN[0]:AD_SPAN[1]].astype(bf16), wau_ref[...], _NN))
    gate = _dot(_sigmoid(zl[:, GD_SPAN[0]:GD_SPAN[1]]).astype(bf16), wgu_ref[...], _NN)

    ones = ones_scr[...]

    def headsum(x):
        npc = Lb // SUB_L
        xs = jnp.concatenate([x[:, p * SUB_L:(p + 1) * SUB_L] for p in range(npc)], axis=0)
        sums = _dot(xs.astype(bf16), ones, _NN)
        return jnp.concatenate([sums[p * rows:(p + 1) * rows, :] for p in range(npc)], axis=1)

    kkr = k * pv(PV_KK)
    kk_all = kkr * lax.rsqrt(jnp.maximum(headsum(kkr * kkr), 1e-24))
    kf_all = k * (1.0 + (a - 1.0) * pv(PV_KA))
    kka_all = kk_all * a

    tri = (lax.broadcasted_iota(jnp.int32, (C, C), 1) <= lax.broadcasted_iota(jnp.int32, (C, C), 0)).astype(bf16)
    hm = hm_scr[...]
    strict, incl = tm_scr[0], tm_scr[1]

    def stack16(x):
        if C % 16 == 0:
            return jnp.concatenate([x.astype(bf16)] * Hg, axis=0)
        return jnp.concatenate([x] * Hg, axis=0).astype(bf16)

    chains = [(bi, jg) for bi in range(nseq) for jg in range(nsub)]
    subs = range(len(chains))
    seq_rows = [slice(bi * C, (bi + 1) * C) for bi, _ in chains]
    lanes = [slice(jg * L, (jg + 1) * L) for _, jg in chains]
    part = lambda x, j: x[seq_rows[j], lanes[j]]

    def cumsum_t(lw):
        h1 = lw.astype(bf16)
        r1 = lw - h1.astype(f32)
        h2 = r1.astype(bf16)
        h3 = (r1 - h2.astype(f32)).astype(bf16)
        return _dot(tri, h1, _NN) + (_dot(tri, h2, _NN) + _dot(tri, h3, _NN))

    lw = [part(lw_all, j) for j in subs]
    cum = [cumsum_t(lw[j]) for j in subs]
    g = [jnp.exp(cum[j]) for j in subs]
    gi = [jnp.exp(-cum[j]) for j in subs]
    gprev = [jnp.exp(cum[j] - lw[j]) for j in subs]
    g_end = [g[j][C - 1:C, :] for j in subs]
    kk = [part(kk_all, j) for j in subs]
    kka = [part(kka_all, j) for j in subs]
    kf = [part(kf_all, j) for j in subs]
    lhs = [jnp.concatenate([stack16(-kk[j] * gprev[j]) * hm, stack16(part(r, j) * g[j]) * hm], axis=0)
           for j in subs]
    rhs = [jnp.concatenate([stack16(kka[j] * gi[j]), stack16(kf[j] * gi[j])], axis=0) for j in subs]
    res = [_dot(rhs[j], lhs[j], _NT).astype(bf16) for j in subs]
    mt = [res[j][:R, :R] * strict for j in subs]
    pt = [res[j][:R, R:] * incl for j in subs]
    nqt = [jnp.concatenate([res[j][R:, :R] * strict, res[j][R:, R:] * incl], axis=1) for j in subs]

    def rows_by_head(x):
        return jnp.concatenate([x[:, h * HEAD_DIM:(h + 1) * HEAD_DIM] for h in range(Hg)], axis=0)

    def lanes_by_head(x):
        return jnp.concatenate([x[h * C:(h + 1) * C, :] for h in range(Hg)], axis=1)

    vs = [rows_by_head(part(v, j)).astype(bf16) for j in subs]
    s = [s_scr[bi, :, lanes[j]] for j, (bi, _) in enumerate(chains)]
    base = [_dot(s[j].astype(bf16), lhs[j], _NT) + _dot(vs[j], nqt[j], _TN) for j in subs]

    x = [base[j][:, :R] for j in subs]
    mp = mt
    nsq = C.bit_length() - 1
    for i in range(nsq):
        if i < nsq - 1:
            both = [_dot(jnp.concatenate([x[j].astype(bf16), mp[j]], axis=0), mp[j], _NN) for j in subs]
            x = [x[j] + both[j][:HEAD_DIM] for j in subs]
            mp = [both[j][HEAD_DIM:].astype(bf16) for j in subs]
        else:
            x = [x[j] + _dot(x[j].astype(bf16), mp[j], _NN) for j in subs]
    u = [x[j].astype(bf16) for j in subs]

    y = [lanes_by_head((base[j][:, R:] + _dot(u[j], pt[j], _NN)).T) for j in subs]
    for j, (bi, _) in enumerate(chains):
        gl = g_end[j] * gi[j]
        s_scr[bi, :, lanes[j]] = (g_end[j] * s[j] + _dot(u[j], stack16(kka[j] * gl) * hm, _NN)
                                  + _dot(vs[j], stack16(kf[j] * gl) * hm, _TN))
    y = jnp.concatenate([jnp.concatenate(y[bi * nsub:(bi + 1) * nsub], axis=-1) for bi in range(nseq)],
                        axis=0)

    mean = headsum(y) * (1.0 / HEAD_DIM)
    d = y - mean
    var = headsum(d * d) * (1.0 / HEAD_DIM)
    yn = d * lax.rsqrt(var + GN_EPS) * pv(PV_LNG) + pv(PV_LNB)
    bonus = headsum(r * kf_all * pv(PV_RK)) * v
    o_rwkv = ((yn + bonus) * gate).astype(o_ref.dtype)
    for bi in range(nseq):
        o_ref[bi, :, :RWKV_W] = o_rwkv[bi * C:(bi + 1) * C, :]

    @pl.when(c == nchunks - 1)
    def _():
        for bi in range(nseq):
            for h in range(Hb):
                sout_ref[bi, h] = s_scr[bi, :, h * HEAD_DIM:(h + 1) * HEAD_DIM]


def _mixer(z, shift_prev, pool_prev, s0, P, *, start_pos, chunk, heads, nsub, nseq):
    B, T, _ = z.shape
    C, Hg = chunk, heads
    L, R = Hg * HEAD_DIM, Hg * C
    assert nsub * Hg == N_HEADS and B % nseq == 0 and T % C == 0
    nchunks = T // C
    prkv = shift_prev[:, :RKV_W].reshape(B, 3, RWKV_W)
    plora = jnp.pad(shift_prev[:, RKV_W:], ((0, 0), (0, LORA_W - LORA_RANKS))).reshape(B, 1, LORA_W)
    zcol = lambda w, j: pl.BlockSpec((nseq, C, w), lambda b, c: (b, c, j))
    whole = lambda shape: pl.BlockSpec(shape, lambda b, c: (0,) * len(shape))
    per_b = lambda shape: pl.BlockSpec((nseq,) + shape, lambda b, c: (b,) + (0,) * len(shape))
    st = per_b((N_HEADS, HEAD_DIM, HEAD_DIM))
    pool0 = (IN_WP - POOL_W) // POOL_GW
    return pl.pallas_call(
        functools.partial(_mixer_kernel, chunk=C, heads=Hg, nsub=nsub, nseq=nseq, nchunks=nchunks,
                          start_pos=start_pos),
        grid=(B // nseq, nchunks),
        in_specs=[zcol(RWKV_W, 0), zcol(RWKV_W, 1), zcol(RWKV_W, 2), zcol(LORA_W, RKV_W // LORA_W),
                  per_b((3, RWKV_W)), per_b((1, LORA_W)), whole((N_PVEC, RWKV_W)), whole((1, LORA_W)),
                  whole((WD_SPAN[1] - WD_SPAN[0], RWKV_W)), whole((AD_SPAN[1] - AD_SPAN[0], RWKV_W)),
                  whole((GD_SPAN[1] - GD_SPAN[0], RWKV_W)), per_b((HEAD_DIM, RWKV_W)),
                  zcol(POOL_GW, pool0), zcol(POOL_GW, pool0 + 1), zcol(POOL_GW, pool0 + 2), zcol(POOL_GW, pool0 + 3),
                  per_b((POOL_BUF, POOL_W)), whole((N_POOL, POOL_GW, POOL_GW)), whole((1, POOL_W))],
        out_specs=[pl.BlockSpec((nseq, C, MIX_W), lambda b, c: (b, c, 0)), st],
        out_shape=[jax.ShapeDtypeStruct((B, T, MIX_W), bf16),
                   jax.ShapeDtypeStruct((B, N_HEADS, HEAD_DIM, HEAD_DIM), f32)],
        scratch_shapes=[pltpu.VMEM((nseq, HEAD_DIM, RWKV_W), f32), pltpu.VMEM((R, L), bf16),
                        pltpu.VMEM((2, R, R), bf16), pltpu.VMEM((SUB_L, SUB_L), bf16),
                        pltpu.VMEM((nseq, 4, 8, RWKV_W), f32),
                        pltpu.VMEM((nseq, N_POOL, POOL_CARRY, POOL_GW), f32)],
        compiler_params=pltpu.CompilerParams(
            dimension_semantics=("parallel", "arbitrary"), vmem_limit_bytes=VMEM_LIMIT),
        name="mixer",
    )(z, z, z, z, prkv, plora, P["pvec"], P["mu_lora"], P["w_decay_up"], P["w_a_up"], P["w_g_up"], s0,
      z, z, z, z, pool_prev, P["w_pool"], P["pool_scale"])


def _attend(q, k_ref, v_ref, head_axis):
    outs = []
    for h in range(XA_HEADS):
        sl = slice(h * XA_HEAD_DIM, (h + 1) * XA_HEAD_DIM)
        if head_axis:
            kh, vh = k_ref[0, :, h, :], v_ref[0, :, h, :]
        else:
            kh, vh = k_ref[0, :, sl], v_ref[0, :, sl]
        s = _dot(q[:, sl], kh.astype(bf16), _NT) * (XA_HEAD_DIM ** -0.5)
        e = jnp.exp(s - jnp.max(s, axis=-1, keepdims=True))
        v1 = jnp.concatenate([vh.astype(bf16), jnp.ones((MEM_TOKENS, XA_HEAD_DIM), bf16)], axis=1)
        ov = _dot(e.astype(bf16), v1, _NN)
        outs.append(ov[:, :XA_HEAD_DIM] / ov[:, XA_HEAD_DIM:])
    return jnp.concatenate(outs, axis=-1)


def _xattn_kernel(q_ref, k_ref, v_ref, o_ref, *, head_axis):
    o_ref[0] = _attend(q_ref[0], k_ref, v_ref, head_axis).astype(o_ref.dtype)


def _xa_layer_kernel(x_ref, g_ref, wq_ref, k_ref, v_ref, wo_ref, gf_ref, o_ref, hn_ref):
    x = x_ref[0]
    q = _dot(_rms_scale(x, g_ref[...]).astype(bf16), wq_ref[...], _NN).astype(bf16)
    o = _attend(q, k_ref, v_ref, False).astype(bf16)
    x = x + _dot(o, wo_ref[...], _NN)
    o_ref[0] = x
    hn_ref[0] = _rms_scale(x, gf_ref[...]).astype(hn_ref.dtype)


def _xa_layer(x, mk, mv, P, tq):
    B, T, D = x.shape
    row = pl.BlockSpec((1, tq, D), lambda b, i: (b, i, 0))
    kv = pl.BlockSpec((1, MEM_TOKENS, XA_W), lambda b, i: (b, 0, 0))
    whole = lambda shape: pl.BlockSpec(shape, lambda b, i: (0, 0))
    return pl.pallas_call(
        _xa_layer_kernel,
        grid=(B, T // tq),
        in_specs=[row, whole((1, D)), whole((D, XA_W)), kv, kv, whole((XA_W, D)), whole((1, D))],
        out_specs=[row, row],
        out_shape=[jax.ShapeDtypeStruct((B, T, D), f32), jax.ShapeDtypeStruct((B, T, D), bf16)],
        compiler_params=pltpu.CompilerParams(
            dimension_semantics=("parallel", "parallel"), vmem_limit_bytes=VMEM_LIMIT),
        name="xa_layer",
    )(x, P["norm_xa_g"].reshape(1, D), P["w_xq"].astype(bf16), mk, mv, P["w_xo"].astype(bf16),
      P["norm_ffn_g"].reshape(1, D))


def _xattn(q, mk, mv, tq):
    B, T, _ = q.shape
    tq = min(tq, T)
    head_axis = mk.ndim == 4
    if head_axis:
        kv_spec = pl.BlockSpec((1, MEM_TOKENS, XA_HEADS, XA_HEAD_DIM), lambda b, i: (b, 0, 0, 0))
    else:
        kv_spec = pl.BlockSpec((1, MEM_TOKENS, XA_W), lambda b, i: (b, 0, 0))
    return pl.pallas_call(
        functools.partial(_xattn_kernel, head_axis=head_axis),
        grid=(B, T // tq),
        in_specs=[pl.BlockSpec((1, tq, XA_W), lambda b, i: (b, i, 0)), kv_spec, kv_spec],
        out_specs=pl.BlockSpec((1, tq, XA_W), lambda b, i: (b, i, 0)),
        out_shape=jax.ShapeDtypeStruct((B, T, XA_W), bf16),
        compiler_params=pltpu.CompilerParams(
            dimension_semantics=("parallel", "parallel"), vmem_limit_bytes=VMEM_LIMIT),
        name="xattn",
    )(q, mk, mv)


def _block(x, mk, mv, shift_prev, pool_prev, wkv_prev, start_pos, P, chunk, nseq):
    B, T, _ = x.shape
    M = B * T
    x2 = x.reshape(M, D_MODEL)

    z = _norm_project(x2, P["norm_mix_g"], P["w_in_t"]).reshape(B, T, IN_WP)
    mix, s_new = _mixer(z, shift_prev, pool_prev, wkv_prev, P, start_pos=start_pos, chunk=chunk,
                        heads=SUB_HEADS, nsub=N_HEADS // SUB_HEADS, nseq=nseq)
    x2 = _matmul(mix.reshape(M, MIX_W), P["w_out"], res=x2, tm=2048, tn=512, tk=MIX_W)
    new_shift = z[:, -1, :SHIFT_W]
    new_pool = jnp.concatenate([pool_prev, z[:, :, IN_WP - POOL_W:]], axis=1)[:, -POOL_BUF:]

    if mk.ndim == 3 and T % XA_ROW_TILE == 0:
        x3, h3 = _xa_layer(x2.reshape(B, T, D_MODEL), mk, mv, P, XA_ROW_TILE)
        x2, h = x3.reshape(M, D_MODEL), h3.reshape(M, D_MODEL)
    else:
        q = _matmul(x2, P["w_xq"], norm_g=P["norm_xa_g"], out_dtype=bf16, tm=512, tn=XA_W, tk=D_MODEL)
        o = _xattn(q.reshape(B, T, XA_W), mk, mv, tq=512)
        x2 = _matmul(o.reshape(M, XA_W), P["w_xo"], res=x2, tk=512)
        h = _rmsnorm(x2, P["norm_ffn_g"], bf16)

    hid = _matmul(h, P["w_up"], act="relu2", out_dtype=bf16, tm=2048, tn=512, tk=D_MODEL)
    x2 = _matmul(hid, P["w_down"], res=x2, tn=512, tk=4096)
    y_out = _rmsnorm(x2, P["norm_final_g"], f32).reshape(B, T, D_MODEL)

    return y_out, new_shift, new_pool, s_new


def _params(norm_mix_g, w_in, mu_shift, w0, w_decay_up, a0, w_a_up, w_g_up, k_k, k_a, r_k, lnx_g, lnx_b, w_pool,
            pool_scale, w_out, norm_xa_g, norm_mem_g, w_xq, w_mk, w_mv, w_xo, norm_ffn_g, w_up, w_down,
            norm_final_g):
    l = 0
    mu = mu_shift[l]
    pvec = jnp.stack([mu[:RWKV_W], mu[RWKV_W:2 * RWKV_W], mu[2 * RWKV_W:RKV_W], w0[l], a0[l], k_k[l],
                      k_a[l], r_k[l].reshape(RWKV_W), lnx_g[l], lnx_b[l]])
    pvec = jnp.concatenate([pvec, jnp.zeros((N_PVEC - pvec.shape[0], RWKV_W), f32)], axis=0)

    def lora_rows(w, first, span):
        return jnp.pad(w, ((first - span[0], span[1] - first - w.shape[0]), (0, 0))).astype(bf16)

    return {
        "norm_mix_g": norm_mix_g[l], "w_in_t": jnp.transpose(w_in[l]),
        "pvec": pvec, "mu_lora": jnp.pad(mu[RKV_W:], (0, LORA_W - LORA_RANKS)).reshape(1, LORA_W),
        "w_decay_up": lora_rows(w_decay_up[l], 0, WD_SPAN), "w_a_up": lora_rows(w_a_up[l], DECAY_RANK, AD_SPAN),
        "w_g_up": lora_rows(w_g_up[l], DECAY_RANK + AAA_RANK, GD_SPAN),
        "w_pool": w_pool[l].astype(bf16), "pool_scale": pool_scale[l].reshape(1, POOL_W), "w_out": w_out[l],
        "norm_xa_g": norm_xa_g[l], "w_xq": w_xq[l], "w_xo": w_xo[l],
        "norm_ffn_g": norm_ffn_g[l], "w_up": w_up[l], "w_down": w_down[l],
        "norm_final_g": norm_final_g, "norm_mem_g": norm_mem_g[l], "w_mk": w_mk[l], "w_mv": w_mv[l],
    }


def kernel(x_prompt, x_sample, mem_prompt, state_wkv, state_shift, state_pool, cache_mem_k, cache_mem_v, norm_mix_g, w_in, mu_shift, w0, w_decay_up, a0, w_a_up, w_g_up, k_k, k_a, r_k, lnx_g, lnx_b, w_pool, pool_scale, w_out, norm_xa_g, norm_mem_g, w_xq, w_mk, w_mv, w_xo, norm_ffn_g, w_up, w_down, norm_final_g):
    Bp = x_prompt.shape[0]
    P = _params(norm_mix_g, w_in, mu_shift, w0, w_decay_up, a0, w_a_up, w_g_up, k_k, k_a, r_k, lnx_g, lnx_b, w_pool,
                pool_scale, w_out, norm_xa_g, norm_mem_g, w_xq, w_mk, w_mv, w_xo, norm_ffn_g, w_up, w_down,
                norm_final_g)
    mem = _rmsnorm(mem_prompt.reshape(Bp * MEM_TOKENS, D_MODEL), P["norm_mem_g"], bf16)
    mk3 = _matmul(mem, P["w_mk"], tn=512, tk=D_MODEL).reshape(Bp, MEM_TOKENS, XA_W)
    mv3 = _matmul(mem, P["w_mv"], tn=512, tk=D_MODEL).reshape(Bp, MEM_TOKENS, XA_W)
    mk_p = mk3.reshape(Bp, MEM_TOKENS, XA_HEADS, XA_HEAD_DIM)
    mv_p = mv3.reshape(Bp, MEM_TOKENS, XA_HEADS, XA_HEAD_DIM)

    yp, sh_p, pl_p, wkv_p = _block(
        x_prompt, mk3, mv3, jnp.zeros((Bp, SHIFT_W), f32), jnp.zeros((Bp, POOL_BUF, POOL_W), f32),
        jnp.zeros((Bp, HEAD_DIM, RWKV_W), f32), 0, P, chunk=64, nseq=1)
    Bs = x_sample.shape[0]
    wkv_s0 = state_wkv.reshape(Bs, N_HEADS, HEAD_DIM, HEAD_DIM).transpose(0, 2, 1, 3).reshape(Bs, HEAD_DIM, RWKV_W)
    ys, sh_s, pl_s, wkv_s = _block(
        x_sample, cache_mem_k.reshape(cache_mem_k.shape[1:]), cache_mem_v.reshape(cache_mem_v.shape[1:]),
        state_shift.reshape(state_shift.shape[1:]), state_pool.reshape(state_pool.shape[1:]),
        wkv_s0, PAST_LEN, P, chunk=x_sample.shape[1], nseq=4)
    ex = lambda t: t[None]
    return (yp, ys, ex(wkv_p), ex(sh_p), ex(pl_p), ex(mk_p), ex(mv_p), ex(wkv_s), ex(sh_s), ex(pl_s))
```
